```python
import jax, jax.numpy as jnp
from jax import lax
import numpy as np


D_MODEL = 1024
BATCH = 8
SEQ = 4096
DEPTH = 4

HEAD_DIM = 64
FOX_HEADS = 4
NSA_HEADS = 8
NSA_KV_GROUPS = 2
DIL_HEADS = 4
MIX_WIDTH = (FOX_HEADS + NSA_HEADS + DIL_HEADS) * HEAD_DIM
DIL_PATTERNS = ((128, 1), (512, 4), (2048, 16))
ROPE_THETA = 500000.0
ROPE_DIM = HEAD_DIM // 4
Q_BLOCK = 128
CMP_BLOCK = 32
CMP_STRIDE = 16
CMP_HIDDEN = 4 * HEAD_DIM
SEL_BLOCK = 64
SEL_TOPK = 16
NSA_WINDOW = 512
D_FF = 2816
CONV_WIDTH = 3
RMS_EPS = 1e-6
NEG_INF = -1e30
FORCE_SCORE = 1e9
ATTN_SCALE = HEAD_DIM ** -0.5
KV_W = NSA_KV_GROUPS * HEAD_DIM
IN_SPLITS = (FOX_HEADS * HEAD_DIM,) * 3 + (FOX_HEADS,) + (NSA_HEADS * HEAD_DIM,) + (KV_W,) * 6 + (3 * NSA_HEADS,) + (DIL_HEADS * HEAD_DIM,) * 3
N_IN = sum(IN_SPLITS)

kernel_name = 'hybrid_fox_nsa_dilated_block'


def rms_norm(x, g):
    xf = x.astype(jnp.float32)
    y = xf * lax.rsqrt(jnp.mean(xf * xf, axis=-1, keepdims=True) + RMS_EPS)
    return (y * g.astype(jnp.float32)).astype(x.dtype)


def rope_tables(positions):
    half = ROPE_DIM // 2
    inv_freq = ROPE_THETA ** (-2.0 * jnp.arange(half, dtype=jnp.float32) / ROPE_DIM)
    ang = positions.astype(jnp.float32)[..., None] * inv_freq
    return jnp.cos(ang)[:, :, None, :], jnp.sin(ang)[:, :, None, :]


def partial_rope(x, cos, sin):
    half = ROPE_DIM // 2
    cos = cos.astype(x.dtype)
    sin = sin.astype(x.dtype)
    x1 = x[..., :half]
    x2 = x[..., half:ROPE_DIM]
    return jnp.concatenate([x1 * cos - x2 * sin, x2 * cos + x1 * sin, x[..., ROPE_DIM:]], axis=-1)


def masked_softmax(scores, mask):
    s = jnp.where(mask, scores, NEG_INF)
    m = jnp.max(s, axis=-1, keepdims=True)
    e = jnp.where(mask, jnp.exp(s - m), 0.0)
    den = jnp.maximum(jnp.sum(e, axis=-1, keepdims=True), 1e-30)
    return e / den, m + jnp.log(den)


def sweep_query_blocks(fn, seq):
    out = lax.map(fn, jnp.arange(seq // Q_BLOCK, dtype=jnp.int32) * Q_BLOCK)
    out = jnp.moveaxis(out, 0, 1)
    return out.reshape((out.shape[0], seq) + out.shape[3:])


def fox_attention(q, k, v, log_f):
    S = q.shape[1]
    c = jnp.cumsum(log_f, axis=1)
    c_k = jnp.transpose(c, (0, 2, 1))[:, :, None, :]
    kpos = jnp.arange(S)

    def block(start):
        qb = lax.dynamic_slice_in_dim(q, start, Q_BLOCK, axis=1)
        cb = lax.dynamic_slice_in_dim(c, start, Q_BLOCK, axis=1)
        qpos = start + jnp.arange(Q_BLOCK)
        s = jnp.einsum('bqhd,bkhd->bhqk', qb, k, preferred_element_type=jnp.float32) * ATTN_SCALE
        s = s + jnp.transpose(cb, (0, 2, 1))[..., None] - c_k
        p, _ = masked_softmax(s, kpos[None, :] <= qpos[:, None])
        return jnp.einsum('bhqk,bkhd->bqhd', p.astype(v.dtype), v)

    return sweep_query_blocks(block, S)


def dilated_attention(q, k, v):
    S = q.shape[1]

    def block(start):
        qb = lax.dynamic_slice_in_dim(q, start, Q_BLOCK, axis=1)
        qpos = start + jnp.arange(Q_BLOCK)
        outs, lses = [], []
        for window, dilation in DIL_PATTERNS:
            offs = jnp.arange(window // dilation + 1) * dilation
            idx = qpos[:, None] - offs[None, :]
            valid = idx >= 0
            idx = jnp.maximum(idx, 0)
            kg = jnp.take(k, idx, axis=1)
            vg = jnp.take(v, idx, axis=1)
            s = jnp.einsum('bqhd,bqjhd->bhqj', qb, kg, preferred_element_type=jnp.float32) * ATTN_SCALE
            p, lse = masked_softmax(s, valid[None, None])
            outs.append(jnp.einsum('bhqj,bqjhd->bqhd', p.astype(v.dtype), vg))
            lses.append(lse[..., 0])
        w = jax.nn.softmax(jnp.stack(lses, 0), axis=0)
        w = jnp.transpose(w, (0, 1, 3, 2))[..., None].astype(v.dtype)
        return jnp.sum(w * jnp.stack(outs, 0), axis=0)

    return sweep_query_blocks(block, S)


def nsa_compress(kv, pos_emb, w1, w2):
    S = kv.shape[1]
    n_cmp = (S - CMP_BLOCK) // CMP_STRIDE + 1
    idx = jnp.arange(n_cmp)[:, None] * CMP_STRIDE + jnp.arange(CMP_BLOCK)[None, :]
    win = jnp.take(kv, idx, axis=1) + pos_emb[None, None, :, None, :]
    B, N, L, G, hd = win.shape
    win = jnp.transpose(win, (0, 1, 3, 2, 4)).reshape(B, N, G, L * hd)
    return jax.nn.gelu(win @ w1) @ w2


def nsa_attention(q, kc, vc, ks, vs, kw, vw, gates, cmp_pos_k, cmp_w1_k, cmp_w2_k, cmp_pos_v, cmp_w1_v, cmp_w2_v):
    B, S, H, hd = q.shape
    G = NSA_KV_GROUPS
    R = H // G
    k_cmp = nsa_compress(kc, cmp_pos_k, cmp_w1_k, cmp_w2_k)
    v_cmp = nsa_compress(vc, cmp_pos_v, cmp_w1_v, cmp_w2_v)
    n_cmp = k_cmp.shape[1]
    cmp_start = jnp.arange(n_cmp) * CMP_STRIDE
    cmp_end = cmp_start + CMP_BLOCK - 1
    n_sel = S // SEL_BLOCK
    top_k = min(SEL_TOPK, n_sel)
    sel_start = jnp.arange(n_sel) * SEL_BLOCK
    overlap = ((cmp_start[:, None] < sel_start[None, :] + SEL_BLOCK) & (cmp_start[:, None] + CMP_BLOCK > sel_start[None, :])).astype(jnp.float32)
    ks_blk = jnp.transpose(ks.reshape(B, n_sel, SEL_BLOCK, G, hd), (0, 3, 1, 2, 4))
    vs_blk = jnp.transpose(vs.reshape(B, n_sel, SEL_BLOCK, G, hd), (0, 3, 1, 2, 4))
    pad = ((0, 0), (NSA_WINDOW, 0), (0, 0), (0, 0))
    kw_pad = jnp.pad(kw, pad)
    vw_pad = jnp.pad(vw, pad)
    gather_blocks = jax.vmap(jax.vmap(lambda blocks, idx: blocks[idx]))
    blk_id = jnp.arange(n_sel)[None, :]

    def block(start):
        qb = lax.dynamic_slice_in_dim(q, start, Q_BLOCK, axis=1).reshape(B, Q_BLOCK, G, R, hd)
        gb = lax.dynamic_slice_in_dim(gates, start, Q_BLOCK, axis=1).reshape(B, Q_BLOCK, G, R, 3)
        qpos = start + jnp.arange(Q_BLOCK)
        s_c = jnp.einsum('bqgrd,bngd->bgrqn', qb, k_cmp, preferred_element_type=jnp.float32) * ATTN_SCALE
        p_c, _ = masked_softmax(s_c, cmp_end[None, :] <= qpos[:, None])
        o_c = jnp.einsum('bgrqn,bngd->bqgrd', p_c.astype(v_cmp.dtype), v_cmp)
        imp = jnp.einsum('bgrqn,nj->bgqj', p_c, overlap)
        cur = (qpos // SEL_BLOCK)[:, None]
        forced = (blk_id == 0) | (blk_id == cur) | (blk_id == cur - 1)
        future = sel_start[None, :] > qpos[:, None]
        imp = jnp.where(future, NEG_INF, jnp.where(forced, FORCE_SCORE, imp))
        _, sel = lax.top_k(imp, top_k)
        k_sel = gather_blocks(ks_blk, sel).reshape(B, G, Q_BLOCK, top_k * SEL_BLOCK, hd)
        v_sel = gather_blocks(vs_blk, sel).reshape(B, G, Q_BLOCK, top_k * SEL_BLOCK, hd)
        kpos_sel = (sel[..., None] * SEL_BLOCK + jnp.arange(SEL_BLOCK)).reshape(B, G, Q_BLOCK, top_k * SEL_BLOCK)
        mask_s = (kpos_sel <= qpos[None, None, :, None])[:, :, None]
        s_s = jnp.einsum('bqgrd,bgqkd->bgrqk', qb, k_sel, preferred_element_type=jnp.float32) * ATTN_SCALE
        p_s, _ = masked_softmax(s_s, mask_s)
        o_s = jnp.einsum('bgrqk,bgqkd->bqgrd', p_s.astype(v_sel.dtype), v_sel)
        kwb = lax.dynamic_slice_in_dim(kw_pad, start, Q_BLOCK + NSA_WINDOW, axis=1)
        vwb = lax.dynamic_slice_in_dim(vw_pad, start, Q_BLOCK + NSA_WINDOW, axis=1)
        kpos_w = (start - NSA_WINDOW + jnp.arange(Q_BLOCK + NSA_WINDOW))[None, :]
        mask_w = (kpos_w <= qpos[:, None]) & (kpos_w > qpos[:, None] - NSA_WINDOW) & (kpos_w >= 0)
        s_w = jnp.einsum('bqgrd,bkgd->bgrqk', qb, kwb, preferred_element_type=jnp.float32) * ATTN_SCALE
        p_w, _ = masked_softmax(s_w, mask_w)
        o_w = jnp.einsum('bgrqk,bkgd->bqgrd', p_w.astype(vwb.dtype), vwb)
        o = gb[..., 0:1] * o_c + gb[..., 1:2] * o_s + gb[..., 2:3] * o_w
        return o.reshape(B, Q_BLOCK, H, hd)

    return sweep_query_blocks(block, S)


def hybrid_mixer(h, cos, sin, w_in, b_forget, b_nsa_gate, cmp_pos_k, cmp_w1_k, cmp_w2_k, cmp_pos_v, cmp_w1_v, cmp_w2_v, w_out):
    B, S, _ = h.shape
    proj = h @ w_in
    splits = np.cumsum(IN_SPLITS)[:-1].tolist()
    (fq, fk, fv, ff, nq, kc, vc, ks, vs, kw, vw, ng, dq, dk, dv) = jnp.split(proj, splits, axis=-1)

    def heads(t, n):
        return t.reshape(B, S, n, HEAD_DIM)

    log_f = jax.nn.log_sigmoid(ff.astype(jnp.float32) + b_forget.astype(jnp.float32))
    o_fox = fox_attention(heads(fq, FOX_HEADS), heads(fk, FOX_HEADS), heads(fv, FOX_HEADS), log_f)
    rope = lambda t, n: partial_rope(heads(t, n), cos, sin)
    gates = jax.nn.sigmoid(ng + b_nsa_gate).reshape(B, S, NSA_HEADS, 3)
    o_nsa = nsa_attention(rope(nq, NSA_HEADS), rope(kc, NSA_KV_GROUPS), heads(vc, NSA_KV_GROUPS),
                          rope(ks, NSA_KV_GROUPS), heads(vs, NSA_KV_GROUPS),
                          rope(kw, NSA_KV_GROUPS), heads(vw, NSA_KV_GROUPS), gates,
                          cmp_pos_k, cmp_w1_k, cmp_w2_k, cmp_pos_v, cmp_w1_v, cmp_w2_v)
    o_dil = dilated_attention(rope(dq, DIL_HEADS), rope(dk, DIL_HEADS), heads(dv, DIL_HEADS))
    mix = jnp.concatenate([o_fox.reshape(B, S, -1), o_nsa.reshape(B, S, -1), o_dil.reshape(B, S, -1)], axis=-1)
    return mix @ w_out


def conv_ffn(h, w_up, conv_w, conv_b, w_down):
    S = h.shape[1]
    u = h @ w_up
    u_pad = jnp.pad(u, ((0, 0), (CONV_WIDTH - 1, 0), (0, 0)))
    c = conv_b + u_pad[:, 0:S] * conv_w[0]
    for j in range(1, CONV_WIDTH):
        c = c + u_pad[:, j:j + S] * conv_w[j]
    a, b = jnp.split(c, 2, axis=-1)
    return (jax.nn.silu(a) * b) @ w_down


def setup_inputs(seed: int = 0) -> dict:
    key = jax.random.key(seed)
    k = jax.random.split(key, 24)
    nrm = lambda kk, shape, scale: jax.random.normal(kk, shape, jnp.float32) * scale
    x = nrm(k[0], (BATCH, SEQ, D_MODEL), 1.0)
    offset = jax.random.randint(k[1], (BATCH, 1), 0, 1024, dtype=jnp.int32)
    positions = offset + jnp.arange(SEQ, dtype=jnp.int32)[None, :]
    return {
        'x': x,
        'positions': positions,
        'attn_pre_norm': 1.0 + nrm(k[2], (DEPTH, D_MODEL), 0.05),
        'attn_post_norm': 1.0 + nrm(k[3], (DEPTH, D_MODEL), 0.05),
        'ffn_pre_norm': 1.0 + nrm(k[4], (DEPTH, D_MODEL), 0.05),
        'ffn_post_norm': 1.0 + nrm(k[5], (DEPTH, D_MODEL), 0.05),
        'w_in': nrm(k[6], (DEPTH, D_MODEL, N_IN), D_MODEL ** -0.5),
        'b_forget': jax.random.uniform(k[7], (DEPTH, FOX_HEADS), jnp.float32, 1.0, 5.0),
        'b_nsa_gate': nrm(k[8], (DEPTH, 3 * NSA_HEADS), 0.01),
        'cmp_pos_k': nrm(k[9], (DEPTH, CMP_BLOCK, HEAD_DIM), 0.1),
        'cmp_w1_k': nrm(k[10], (DEPTH, CMP_BLOCK * HEAD_DIM, CMP_HIDDEN), (CMP_BLOCK * HEAD_DIM) ** -0.5),
        'cmp_w2_k': nrm(k[11], (DEPTH, CMP_HIDDEN, HEAD_DIM), CMP_HIDDEN ** -0.5),
        'cmp_pos_v': nrm(k[12], (DEPTH, CMP_BLOCK, HEAD_DIM), 0.1),
        'cmp_w1_v': nrm(k[13], (DEPTH, CMP_BLOCK * HEAD_DIM, CMP_HIDDEN), (CMP_BLOCK * HEAD_DIM) ** -0.5),
        'cmp_w2_v': nrm(k[14], (DEPTH, CMP_HIDDEN, HEAD_DIM), CMP_HIDDEN ** -0.5),
        'w_out': nrm(k[15], (DEPTH, MIX_WIDTH, D_MODEL), MIX_WIDTH ** -0.5),
        'w_up': nrm(k[16], (DEPTH, D_MODEL, 2 * D_FF), D_MODEL ** -0.5),
        'conv_w': nrm(k[17], (DEPTH, CONV_WIDTH, 2 * D_FF), CONV_WIDTH ** -0.5),
        'conv_b': nrm(k[18], (DEPTH, 2 * D_FF), 0.01),
        'w_down': nrm(k[19], (DEPTH, D_FF, D_MODEL), D_FF ** -0.5),
    }


def reference(x, positions, attn_pre_norm, attn_post_norm, ffn_pre_norm, ffn_post_norm, w_in, b_forget, b_nsa_gate,
              cmp_pos_k, cmp_w1_k, cmp_w2_k, cmp_pos_v, cmp_w1_v, cmp_w2_v, w_out, w_up, conv_w, conv_b, w_down):
    cos, sin = rope_tables(positions)
    for l in range(DEPTH):
        h = rms_norm(x, attn_pre_norm[l])
        mix = hybrid_mixer(h, cos, sin, w_in[l], b_forget[l], b_nsa_gate[l], cmp_pos_k[l], cmp_w1_k[l], cmp_w2_k[l],
                           cmp_pos_v[l], cmp_w1_v[l], cmp_w2_v[l], w_out[l])
        x = x + rms_norm(mix, attn_post_norm[l])
        h = rms_norm(x, ffn_pre_norm[l])
        x = x + rms_norm(conv_ffn(h, w_up[l], conv_w[l], conv_b[l], w_down[l]), ffn_post_norm[l])
    return x
```

```python
import functools
import math

import numpy as np
import jax
import jax.numpy as jnp
from jax import lax
from jax.experimental import pallas as pl
from jax.experimental.pallas import tpu as pltpu

HEAD_DIM = 64
FOX_HEADS = 4
NSA_HEADS = 8
NSA_GROUPS = 2
NSA_REP = NSA_HEADS // NSA_GROUPS
DIL_HEADS = 4
DIL_DILATIONS = (1, 4, 16)
DIL_SPAN = 128
ROPE_THETA = 500000.0
ROPE_DIM = HEAD_DIM // 4
ROPE_HALF = ROPE_DIM // 2
CMP_BLOCK = 32
CMP_STRIDE = 16
SEL_BLOCK = 64
SEL_TOPK = 16
NSA_WINDOW = 512
CONV_WIDTH = 3
RMS_EPS = 1e-6
NEG_INF = -1e30
FORCE_SCORE = 1e9
ATTN_SCALE = HEAD_DIM ** -0.5
LANES = 128
VMEM_LIMIT = 56 * 1024 * 1024

F32 = jnp.float32
BF16 = jnp.bfloat16


def _dot_nt(a, b):
    return lax.dot_general(a, b, (((1,), (1,)), ((), ())), preferred_element_type=F32)


def _dot(a, b):
    return jnp.dot(a, b, preferred_element_type=F32)


def _rms(x, g):
    return x * lax.rsqrt(jnp.mean(x * x, axis=-1, keepdims=True) + RMS_EPS) * g


def _params(*sem):
    return pltpu.CompilerParams(dimension_semantics=sem, vmem_limit_bytes=VMEM_LIMIT)


def _online_update(s, v, m, l, acc):
    m_new = jnp.maximum(m, jnp.max(s, axis=-1, keepdims=True))
    alpha = jnp.exp(m - m_new)
    p = jnp.exp(s - m_new)
    l = alpha * l + jnp.sum(p, axis=-1, keepdims=True)
    acc = alpha * acc + _dot(p.astype(BF16), v)
    return m_new, l, acc


_ROPE_CHUNKS = {"fox": (), "nq": (0, 1, 2, 3), "nkv": (0, 2, 4), "dil": (0, 1, 2, 3)}
_SEG_WIDTH = {"fox": 768, "nq": 512, "nkv": 768, "dil": 768}
_SEG_ORDER = ("fox", "nq", "nkv", "dil")
_AUX_OFFSET = sum(_SEG_WIDTH.values())
_W_IN_COLS = _AUX_OFFSET + 2 * LANES


def _inproj_kernel(x_ref, g_ref, w_ref, rc_ref, rs1_ref, rs2_ref, bias_ref,
                   fox_ref, nq_ref, nkv_ref, dil_ref, aux_ref, carry_ref, *, tm):
    @pl.when(pl.program_id(1) == 0)
    def _():
        carry_ref[...] = jnp.zeros_like(carry_ref)

    h = _rms(x_ref[...], g_ref[...]).astype(BF16)
    rc, rs1, rs2 = rc_ref[...], rs1_ref[...], rs2_ref[...]
    outs = {"fox": fox_ref, "nq": nq_ref, "nkv": nkv_ref, "dil": dil_ref}
    col = 0
    for name in _SEG_ORDER:
        for c0 in range(0, _SEG_WIDTH[name], 2 * LANES):
            y = _dot(h, w_ref[:, col + c0:col + c0 + 2 * LANES])
            for half in range(2):
                chunk = c0 // LANES + half
                yc = y[:, half * LANES:(half + 1) * LANES]
                if chunk in _ROPE_CHUNKS[name]:
                    yc = (yc * rc + pltpu.roll(yc, ROPE_HALF, 1) * rs1
                          + pltpu.roll(yc, LANES - ROPE_HALF, 1) * rs2)
                outs[name][:, chunk * LANES:(chunk + 1) * LANES] = yc.astype(BF16)
        col += _SEG_WIDTH[name]

    ya = _dot(h, w_ref[:, _AUX_OFFSET:_AUX_OFFSET + 2 * LANES])[:, :LANES] + bias_ref[...]
    lane = lax.broadcasted_iota(jnp.int32, (tm, LANES), 1)
    logf = jnp.minimum(ya, 0.0) - jnp.log1p(jnp.exp(-jnp.abs(ya)))
    logf = jnp.where(lane < FOX_HEADS, logf, 0.0)
    row = lax.broadcasted_iota(jnp.int32, (tm, tm), 0)
    colm = lax.broadcasted_iota(jnp.int32, (tm, tm), 1)
    tril = jnp.where(colm <= row, 1.0, 0.0).astype(F32)
    csum = jnp.dot(tril, logf, preferred_element_type=F32,
                   precision=lax.Precision.HIGHEST) + carry_ref[...]
    carry_ref[...] = csum[tm - 1:tm, :]
    gate = jax.nn.sigmoid(ya)
    aux_ref[...] = jnp.where(lane < FOX_HEADS, csum,
                             jnp.where(lane < FOX_HEADS + 3 * NSA_HEADS, gate, 0.0))


def _inproj(x, g, w, rc, rs1, rs2, bias, *, batch, seq, tm):
    n, d = x.shape
    nt = seq // tm
    tok = lambda width: pl.BlockSpec((tm, width), lambda b, i: (b * nt + i, 0))
    const = lambda shape: pl.BlockSpec(shape, lambda b, i: (0, 0))
    return pl.pallas_call(
        functools.partial(_inproj_kernel, tm=tm),
        grid=(batch, nt),
        in_specs=[tok(d), const((1, d)), const(w.shape), tok(LANES), tok(LANES), tok(LANES),
                  const((1, LANES))],
        out_specs=[tok(768), tok(512), tok(768), tok(768), tok(LANES)],
        out_shape=[jax.ShapeDtypeStruct((n, 768), BF16), jax.ShapeDtypeStruct((n, 512), BF16),
                   jax.ShapeDtypeStruct((n, 768), BF16), jax.ShapeDtypeStruct((n, 768), BF16),
                   jax.ShapeDtypeStruct((n, LANES), F32)],
        scratch_shapes=[pltpu.VMEM((1, LANES), F32)],
        compiler_params=_params("arbitrary", "arbitrary"),
        name="inproj",
    )(x, g, w, rc, rs1, rs2, bias)


def _fox_kernel(q_ref, k_ref, v_ref, cq_ref, ck_ref, o_ref, *, tq):
    i = pl.program_id(1)
    width = FOX_HEADS * HEAD_DIM
    q = q_ref[...]
    lane = lax.broadcasted_iota(jnp.int32, (1, width), 1)
    row = lax.broadcasted_iota(jnp.int32, (tq, tq), 0)
    colm = lax.broadcasted_iota(jnp.int32, (tq, tq), 1)
    causal = colm <= row
    out = jnp.zeros((tq, width), F32)
    for h in range(FOX_HEADS):
        head = (lane >= h * HEAD_DIM) & (lane < (h + 1) * HEAD_DIM)
        qh = jnp.where(head, q, jnp.zeros_like(q))
        cq = cq_ref[:, h:h + 1]

        def scores(j):
            start = pl.multiple_of(j * tq, tq)
            k = k_ref[pl.ds(start, tq), :]
            v = v_ref[pl.ds(start, tq), :]
            ck = ck_ref[h:h + 1, pl.ds(start, tq)]
            return _dot_nt(qh, k) + (cq - ck), v

        s, v = scores(i)
        s = jnp.where(causal, s, NEG_INF)
        state = _online_update(s, v, jnp.full((tq, 1), NEG_INF, F32), jnp.zeros((tq, 1), F32),
                               jnp.zeros((tq, width), F32))

        def body(j, st):
            s, v = scores(j)
            return _online_update(s, v, *st)

        m, l, acc = lax.fori_loop(0, i, body, state)
        out = jnp.where(head, acc / l, out)
    o_ref[...] = out.astype(o_ref.dtype)


def _fox(fox, aux, ck_t, *, batch, seq, tq):
    n = fox.shape[0]
    nt = seq // tq
    width = FOX_HEADS * HEAD_DIM
    return pl.pallas_call(
        functools.partial(_fox_kernel, tq=tq),
        grid=(batch, nt),
        in_specs=[pl.BlockSpec((tq, width), lambda b, i: (b * nt + i, 0)),
                  pl.BlockSpec((seq, width), lambda b, i: (b, 1)),
                  pl.BlockSpec((seq, width), lambda b, i: (b, 2)),
                  pl.BlockSpec((tq, LANES), lambda b, i: (b * nt + i, 0)),
                  pl.BlockSpec((None, 8, seq), lambda b, i: (b, 0, 0))],
        out_specs=pl.BlockSpec((tq, width), lambda b, i: (b * nt + i, 0)),
        out_shape=jax.ShapeDtypeStruct((n, width), BF16),
        compiler_params=_params("arbitrary", "arbitrary"),
        name="fox",
    )(fox, fox, fox, aux, ck_t)


def _gelu_tanh(x):
    return 0.5 * x * (1.0 + jnp.tanh(math.sqrt(2.0 / math.pi) * (x + 0.044715 * (x * x * x))))


def _compress_kernel(xk_ref, xv_ref, w1k_ref, w2k_ref, pk_ref, w1v_ref, w2v_ref, pv_ref,
                     kc_ref, vc_ref):
    half = CMP_STRIDE * HEAD_DIM

    def run(x_ref, w1_ref, w2_ref, p_ref, o_ref):
        w1 = w1_ref[...]
        posb = _dot(p_ref[...].astype(BF16), w1)[0:1, :]
        out = None
        for g in range(NSA_GROUPS):
            x = x_ref[g]
            n16 = x.shape[0]
            y1 = _dot(x, w1[:half, :])
            y2 = _dot(x, w1[half:, :])
            hid = y1 + pltpu.roll(y2, n16 - 1, 0) + posb
            a = _gelu_tanh(hid).astype(BF16)
            og = _dot(a, w2_ref[g])
            out = og if out is None else out + og
        o_ref[...] = out.astype(o_ref.dtype)

    run(xk_ref, w1k_ref, w2k_ref, pk_ref, kc_ref)
    run(xv_ref, w1v_ref, w2v_ref, pv_ref, vc_ref)


def _compress(xk, xv, w1k, w2k, pk, w1v, w2v, pv):
    batch, groups, n16, feat = xk.shape
    xspec = pl.BlockSpec((None, groups, n16, feat), lambda b: (b, 0, 0, 0))
    full = lambda a: pl.BlockSpec(a.shape, lambda b: (0,) * a.ndim)
    ospec = pl.BlockSpec((None, n16, LANES), lambda b: (b, 0, 0))
    return pl.pallas_call(
        _compress_kernel,
        grid=(batch,),
        in_specs=[xspec, xspec, full(w1k), full(w2k), full(pk), full(w1v), full(w2v), full(pv)],
        out_specs=[ospec, ospec],
        out_shape=[jax.ShapeDtypeStruct((batch, n16, LANES), BF16)] * 2,
        compiler_params=_params("arbitrary"),
        name="nsa_compress",
    )(xk, xv, w1k, w2k, pk, w1v, w2v, pv)


def _nsa_kernel(q_ref, kcmp_ref, vcmp_ref, ks_ref, vs_ref, kw_ref, vw_ref, aux_ref, o_ref,
                *, tq, tk, seq):
    i = pl.program_id(1)
    start = i * tq
    n_cmp = kcmp_ref.shape[0]
    wspan = NSA_WINDOW + tq

    lane1 = lax.broadcasted_iota(jnp.int32, (1, LANES), 1)
    qpos_col = start + lax.broadcasted_iota(jnp.int32, (tq, 1), 0)
    qpos_row = start + lax.broadcasted_iota(jnp.int32, (1, tq), 1)

    kcmp = kcmp_ref[...]
    vcmp = vcmp_ref[...]
    cmp_end = lax.broadcasted_iota(jnp.int32, (1, n_cmp), 1) * CMP_STRIDE + (CMP_BLOCK - 1)
    cmp_ok = cmp_end <= qpos_col
    cs = lax.broadcasted_iota(jnp.int32, (n_cmp, LANES), 0) * CMP_STRIDE
    ss = lax.broadcasted_iota(jnp.int32, (n_cmp, LANES), 1) * SEL_BLOCK
    overlap = jnp.where((cs < ss + SEL_BLOCK) & (cs + CMP_BLOCK > ss), 1.0, 0.0).astype(F32)

    blk = lax.broadcasted_iota(jnp.int32, (LANES, tq), 0)
    cur = qpos_row // SEL_BLOCK
    forced = (blk == 0) | (blk == cur) | (blk == cur - 1)
    future = blk * SEL_BLOCK > qpos_row

    wstart = pl.multiple_of(jnp.maximum(start - NSA_WINDOW, 0), tq)
    kwin = kw_ref[pl.ds(wstart, wspan), :]
    vwin = vw_ref[pl.ds(wstart, wspan), :]
    kpos_w = wstart + lax.broadcasted_iota(jnp.int32, (1, wspan), 1)
    win_ok = (kpos_w <= qpos_col) & (kpos_w > qpos_col - NSA_WINDOW)

    jd = start // tk
    col_tk = lax.broadcasted_iota(jnp.int32, (1, tk), 1)
    key_blk_iota = lax.broadcasted_iota(jnp.int32, (tk, LANES), 0) // SEL_BLOCK
    key_lane = lax.broadcasted_iota(jnp.int32, (tk, LANES), 1)
    aux = aux_ref[...]

    acc_out = [None] * NSA_REP
    for g in range(NSA_GROUPS):
        grp = (lane1 >= g * HEAD_DIM) & (lane1 < (g + 1) * HEAD_DIM)
        qs = []
        o_cmp = []
        p_sum = jnp.zeros((tq, n_cmp), F32)
        for r in range(NSA_REP):
            qc = q_ref[:, r * LANES:(r + 1) * LANES]
            qm = jnp.where(grp, qc, jnp.zeros_like(qc))
            qs.append(qm)
            s = jnp.where(cmp_ok, _dot_nt(qm, kcmp), NEG_INF)
            m = jnp.max(s, axis=-1, keepdims=True)
            e = jnp.where(cmp_ok, jnp.exp(s - m), 0.0)
            p = e / jnp.maximum(jnp.sum(e, axis=-1, keepdims=True), 1e-30)
            p_sum = p_sum + p
            o_cmp.append(_dot(p.astype(BF16), vcmp))

        imp = jnp.dot(p_sum, overlap, preferred_element_type=F32, precision=lax.Precision.HIGHEST)
        imp_t = jnp.where(future, NEG_INF, jnp.where(forced, FORCE_SCORE, imp.T))
        n_sel = seq // SEL_BLOCK
        val = imp_t[:n_sel, :]
        blk_s = blk[:n_sel, :]
        rank = jnp.zeros((n_sel, tq), jnp.int32)
        for jp in range(n_sel):
            other = imp_t[jp:jp + 1, :]
            beats = (other > val) | ((other == val) & (blk_s > jp))
            rank = rank + jnp.where(beats, 1, 0)
        sel_bias_t = jnp.where(rank < min(SEL_TOPK, n_sel), 0.0, NEG_INF).astype(F32)
        sel_bias_t = jnp.concatenate(
            [sel_bias_t, jnp.full((LANES - n_sel, tq), NEG_INF, F32)], axis=0)
        sel_bias = sel_bias_t.T.astype(BF16)

        for r in range(NSA_REP):
            qa = jnp.concatenate([qs[r], sel_bias], axis=1)

            def sel_tile(j, masked):
                ks0 = pl.multiple_of(j * tk, tk)
                k = ks_ref[pl.ds(ks0, tk), :]
                v = vs_ref[pl.ds(ks0, tk), :]
                onehot = jnp.where(key_lane == key_blk_iota + j * (tk // SEL_BLOCK), 1.0, 0.0)
                ka = jnp.concatenate([k, onehot.astype(BF16)], axis=1)
                s = _dot_nt(qa, ka)
                if masked:
                    s = jnp.where(ks0 + col_tk <= qpos_col, s, NEG_INF)
                return s, v

            s, v = sel_tile(jd, True)
            state = _online_update(s, v, jnp.full((tq, 1), NEG_INF, F32),
                                   jnp.zeros((tq, 1), F32), jnp.zeros((tq, LANES), F32))

            def body(j, st):
                s, v = sel_tile(j, False)
                return _online_update(s, v, *st)

            m, l, acc = lax.fori_loop(0, jd, body, state)
            o_sel = acc / l

            s = jnp.where(win_ok, _dot_nt(qs[r], kwin), NEG_INF)
            m = jnp.max(s, axis=-1, keepdims=True)
            e = jnp.exp(s - m)
            o_win = _dot(e.astype(BF16), vwin) / jnp.sum(e, axis=-1, keepdims=True)

            hidx = FOX_HEADS + 3 * (g * NSA_REP + r)
            o = (aux[:, hidx:hidx + 1] * o_cmp[r] + aux[:, hidx + 1:hidx + 2] * o_sel
                 + aux[:, hidx + 2:hidx + 3] * o_win)
            acc_out[r] = o if acc_out[r] is None else jnp.where(grp, o, acc_out[r])

    for r in range(NSA_REP):
        o_ref[:, r * LANES:(r + 1) * LANES] = acc_out[r].astype(o_ref.dtype)


def _nsa(nq, nkv, kcmp, vcmp, aux, *, batch, seq, tq, tk):
    n = nq.shape[0]
    nt = seq // tq
    n16 = kcmp.shape[1]
    kv = lambda c: pl.BlockSpec((seq, LANES), lambda b, i: (b, c))
    cmp_spec = pl.BlockSpec((None, n16, LANES), lambda b, i: (b, 0, 0))
    width = NSA_HEADS * HEAD_DIM
    return pl.pallas_call(
        functools.partial(_nsa_kernel, tq=tq, tk=tk, seq=seq),
        grid=(batch, nt),
        in_specs=[pl.BlockSpec((tq, width), lambda b, i: (b * nt + i, 0)),
                  cmp_spec, cmp_spec, kv(2), kv(3), kv(4), kv(5),
                  pl.BlockSpec((tq, LANES), lambda b, i: (b * nt + i, 0))],
        out_specs=pl.BlockSpec((tq, width), lambda b, i: (b * nt + i, 0)),
        out_shape=jax.ShapeDtypeStruct((n, width), BF16),
        compiler_params=_params("arbitrary", "arbitrary"),
        name="nsa",
    )(nq, kcmp, vcmp, nkv, nkv, nkv, nkv, aux)


def _band_kernel(q_ref, k_ref, v_ref, o_ref, lse_ref, *, tq, span):
    i = pl.program_id(1)
    length = k_ref.shape[0]
    width = DIL_HEADS * HEAD_DIM
    ks0 = pl.multiple_of(jnp.clip(i * tq - DIL_SPAN, 0, length - span), LANES)
    k = k_ref[pl.ds(ks0, span), :]
    v = v_ref[pl.ds(ks0, span), :]
    q = q_ref[...]
    upos_q = i * tq + lax.broadcasted_iota(jnp.int32, (tq, 1), 0)
    upos_k = ks0 + lax.broadcasted_iota(jnp.int32, (1, span), 1)
    delta = upos_q - upos_k
    ok = (delta >= 0) & (delta <= DIL_SPAN)
    lane = lax.broadcasted_iota(jnp.int32, (1, width), 1)
    lane_a = lax.broadcasted_iota(jnp.int32, (1, LANES), 1)
    out = jnp.zeros((tq, width), F32)
    lse = jnp.zeros((tq, LANES), F32)
    for h in range(DIL_HEADS):
        head = (lane >= h * HEAD_DIM) & (lane < (h + 1) * HEAD_DIM)
        qh = jnp.where(head, q, jnp.zeros_like(q))
        s = jnp.where(ok, _dot_nt(qh, k), NEG_INF)
        m = jnp.max(s, axis=-1, keepdims=True)
        e = jnp.exp(s - m)
        den = jnp.sum(e, axis=-1, keepdims=True)
        out = jnp.where(head, _dot(e.astype(BF16), v) / den, out)
        lse = jnp.where(lane_a == h, m + jnp.log(den), lse)
    o_ref[...] = out.astype(o_ref.dtype)
    lse_ref[...] = lse


def _band(qkv, *, tq):
    rows, length, _ = qkv.shape
    width = DIL_HEADS * HEAD_DIM
    nt = length // tq
    span = min(tq + DIL_SPAN, length)
    return pl.pallas_call(
        functools.partial(_band_kernel, tq=tq, span=span),
        grid=(rows, nt),
        in_specs=[pl.BlockSpec((None, tq, width), lambda c, i: (c, i, 0)),
                  pl.BlockSpec((None, length, width), lambda c, i: (c, 0, 1)),
                  pl.BlockSpec((None, length, width), lambda c, i: (c, 0, 2))],
        out_specs=[pl.BlockSpec((None, tq, width), lambda c, i: (c, i, 0)),
                   pl.BlockSpec((None, tq, LANES), lambda c, i: (c, i, 0))],
        out_shape=[jax.ShapeDtypeStruct((rows, length, width), BF16),
                   jax.ShapeDtypeStruct((rows, length, LANES), F32)],
        compiler_params=_params("arbitrary", "arbitrary"),
        name="dilated_band",
    )(qkv, qkv, qkv)


def _outproj_kernel(x_ref, fox_ref, nsa_ref, d0_ref, d1_ref, d2_ref, l0_ref, l1_ref, l2_ref,
                    w_ref, g_ref, o_ref):
    lses = [l0_ref[...], l1_ref[...], l2_ref[...]]
    top = jnp.maximum(jnp.maximum(lses[0], lses[1]), lses[2])
    es = [jnp.exp(l - top) for l in lses]
    tot = es[0] + es[1] + es[2]
    ws = [e / tot for e in es]
    lane = lax.broadcasted_iota(jnp.int32, (1, DIL_HEADS * HEAD_DIM), 1)
    dil = None
    for w, d_ref in zip(ws, (d0_ref, d1_ref, d2_ref)):
        wide = jnp.zeros((w.shape[0], DIL_HEADS * HEAD_DIM), F32)
        for h in range(DIL_HEADS):
            head = (lane >= h * HEAD_DIM) & (lane < (h + 1) * HEAD_DIM)
            wide = jnp.where(head, w[:, h:h + 1], wide)
        term = wide * d_ref[...].astype(F32)
        dil = term if dil is None else dil + term
    dil = dil.astype(BF16)
    fw = FOX_HEADS * HEAD_DIM
    nw = NSA_HEADS * HEAD_DIM
    y = (_dot(fox_ref[...], w_ref[:fw, :]) + _dot(nsa_ref[...], w_ref[fw:fw + nw, :])
         + _dot(dil, w_ref[fw + nw:, :]))
    o_ref[...] = x_ref[...] + _rms(y, g_ref[...])


def _outproj(x, fox_o, nsa_o, dils, lses, w, g, *, tm):
    n, d = x.shape
    tok = lambda width: pl.BlockSpec((tm, width), lambda i: (i, 0))
    const = lambda shape: pl.BlockSpec(shape, lambda i: (0, 0))
    return pl.pallas_call(
        _outproj_kernel,
        grid=(n // tm,),
        in_specs=[tok(d), tok(256), tok(512), tok(256), tok(256), tok(256),
                  tok(LANES), tok(LANES), tok(LANES), const(w.shape), const((1, d))],
        out_specs=tok(d),
        out_shape=jax.ShapeDtypeStruct((n, d), F32),
        compiler_params=_params("arbitrary"),
        name="outproj",
    )(x, fox_o, nsa_o, *dils, *lses, w, g)


def _ffn_kernel(x_ref, gpre_ref, wup_ref, cw_ref, cb_ref, wdn_ref, gpost_ref, o_ref, tail_ref,
                *, tm, d_ff, chunk):
    @pl.when(pl.program_id(1) == 0)
    def _():
        tail_ref[...] = jnp.zeros_like(tail_ref)

    x = x_ref[...]
    h = _rms(x, gpre_ref[...]).astype(BF16)
    row = lax.broadcasted_iota(jnp.int32, (tm, 1), 0)

    def conv(c0):
        u = _dot(h, wup_ref[:, c0:c0 + chunk])
        tail = tail_ref[:, c0:c0 + chunk]
        u1 = jnp.where(row == 0, tail[7:8, :], pltpu.roll(u, 1, 0))
        u2 = jnp.where(row == 0, tail[6:7, :], jnp.where(row == 1, tail[7:8, :], pltpu.roll(u, 2, 0)))
        tail_ref[:, c0:c0 + chunk] = u[tm - 8:, :]
        return (cb_ref[:, c0:c0 + chunk] + u2 * cw_ref[0:1, c0:c0 + chunk]
                + u1 * cw_ref[1:2, c0:c0 + chunk] + u * cw_ref[2:3, c0:c0 + chunk])

    y = jnp.zeros((tm, x.shape[1]), F32)
    for c0 in range(0, d_ff, chunk):
        a = conv(c0)
        b = conv(d_ff + c0)
        act = (a * jax.nn.sigmoid(a) * b).astype(BF16)
        y = y + _dot(act, wdn_ref[c0:c0 + chunk, :])
    o_ref[...] = x + _rms(y, gpost_ref[...])


def _ffn(x, gpre, wup, cw, cb, wdn, gpost, *, batch, seq, tm, chunk):
    n, d = x.shape
    d_ff = wdn.shape[0]
    nt = seq // tm
    tok = pl.BlockSpec((tm, d), lambda b, i: (b * nt + i, 0))
    const = lambda a: pl.BlockSpec(a.shape, lambda b, i: (0, 0), pipeline_mode=pl.Buffered(1))
    return pl.pallas_call(
        functools.partial(_ffn_kernel, tm=tm, d_ff=d_ff, chunk=chunk),
        grid=(batch, nt),
        in_specs=[tok, const(gpre), const(wup), const(cw), const(cb), const(wdn), const(gpost)],
        out_specs=tok,
        out_shape=jax.ShapeDtypeStruct((n, d), F32),
        scratch_shapes=[pltpu.VMEM((8, 2 * d_ff), F32)],
        compiler_params=_params("arbitrary", "arbitrary"),
        name="conv_ffn",
    )(x, gpre, wup, cw, cb, wdn, gpost)


def _rope_tables(positions):
    inv_freq = ROPE_THETA ** (-2.0 * jnp.arange(ROPE_HALF, dtype=F32) / ROPE_DIM)
    ang = positions.astype(F32).reshape(-1, 1) * inv_freq
    cos, sin = jnp.cos(ang), jnp.sin(ang)
    n = ang.shape[0]
    ones = jnp.ones((n, HEAD_DIM - ROPE_DIM), F32)
    zeros_h = jnp.zeros((n, ROPE_HALF), F32)
    zeros_r = jnp.zeros((n, HEAD_DIM - ROPE_DIM), F32)
    rc = jnp.concatenate([cos, cos, ones], axis=1)
    rs1 = jnp.concatenate([zeros_h, sin, zeros_r], axis=1)
    rs2 = jnp.concatenate([-sin, zeros_h, zeros_r], axis=1)
    tile2 = lambda t: jnp.concatenate([t, t], axis=1)
    return tile2(rc), tile2(rs1), tile2(rs2)


def _nsa_head_perm():
    cols = []
    for r in range(NSA_REP):
        for g in range(NSA_GROUPS):
            h = g * NSA_REP + r
            cols.extend(range(h * HEAD_DIM, (h + 1) * HEAD_DIM))
    return np.asarray(cols)


def _regroup_w_in(w):
    fw, nw, kvw, dw = FOX_HEADS * HEAD_DIM, NSA_HEADS * HEAD_DIM, NSA_GROUPS * HEAD_DIM, DIL_HEADS * HEAD_DIM
    sizes = [fw, fw, fw, FOX_HEADS, nw] + [kvw] * 6 + [3 * NSA_HEADS, dw, dw, dw]
    offs = np.concatenate([[0], np.cumsum(sizes)])
    part = lambda idx: w[:, offs[idx]:offs[idx + 1]]
    fq, fk, fv, ff, nq = (part(t) for t in range(5))
    kvs = [part(t) for t in range(5, 11)]
    ng, dq, dk, dv = (part(t) for t in range(11, 15))
    nq = nq[:, _nsa_head_perm()]
    pad = jnp.zeros((w.shape[0], 2 * LANES - FOX_HEADS - 3 * NSA_HEADS), w.dtype)
    cols = [fq * ATTN_SCALE, fk, fv, nq * ATTN_SCALE] + kvs + [dq * ATTN_SCALE, dk, dv, ff, ng, pad]
    return jnp.concatenate(cols, axis=1).astype(BF16)


def _regroup_w_out(w):
    fw, nw = FOX_HEADS * HEAD_DIM, NSA_HEADS * HEAD_DIM
    return jnp.concatenate([w[:fw], w[fw:fw + nw][_nsa_head_perm()], w[fw + nw:]], axis=0).astype(BF16)


def _blocks16(t, batch, seq):
    t = t.reshape(batch, seq // CMP_STRIDE, CMP_STRIDE, NSA_GROUPS, HEAD_DIM)
    return jnp.transpose(t, (0, 3, 1, 2, 4)).reshape(batch, NSA_GROUPS, seq // CMP_STRIDE, CMP_STRIDE * HEAD_DIM)


def _place_w2(w2):
    z = jnp.zeros_like(w2)
    return jnp.stack([jnp.concatenate([w2, z], axis=1), jnp.concatenate([z, w2], axis=1)]).astype(BF16)


def _pad_rows8(p):
    flat = p.reshape(1, -1)
    return jnp.concatenate([flat, jnp.zeros((7, flat.shape[1]), flat.dtype)], axis=0)


def kernel(x, positions, attn_pre_norm, attn_post_norm, ffn_pre_norm, ffn_post_norm, w_in, b_forget, b_nsa_gate, cmp_pos_k, cmp_w1_k, cmp_w2_k, cmp_pos_v, cmp_w1_v, cmp_w2_v, w_out, w_up, conv_w, conv_b, w_down):
    batch, seq, d = x.shape
    depth = w_in.shape[0]
    n = batch * seq
    tm = min(512, seq)
    assert seq % tm == 0 and seq % (16 * LANES) == 0

    rc, rs1, rs2 = _rope_tables(positions)
    xf = x.reshape(n, d)
    for l in range(depth):
        bias = jnp.concatenate([b_forget[l], b_nsa_gate[l],
                                jnp.zeros((LANES - FOX_HEADS - 3 * NSA_HEADS,), F32)]).reshape(1, LANES)
        fox, nq, nkv, dil, aux = _inproj(xf, attn_pre_norm[l].reshape(1, d), _regroup_w_in(w_in[l]),
                                         rc, rs1, rs2, bias, batch=batch, seq=seq, tm=tm)

        ck_t = jnp.transpose(aux.reshape(batch, seq, LANES)[:, :, :8], (0, 2, 1))
        fox_o = _fox(fox, aux, ck_t, batch=batch, seq=seq, tq=tm)

        kcmp, vcmp = _compress(
            _blocks16(nkv[:, 0:LANES], batch, seq), _blocks16(nkv[:, LANES:2 * LANES], batch, seq),
            cmp_w1_k[l].astype(BF16), _place_w2(cmp_w2_k[l]), _pad_rows8(cmp_pos_k[l]),
            cmp_w1_v[l].astype(BF16), _place_w2(cmp_w2_v[l]), _pad_rows8(cmp_pos_v[l]))
        nsa_o = _nsa(nq, nkv, kcmp, vcmp, aux, batch=batch, seq=seq, tq=LANES, tk=tm)

        dils, lses = [], []
        for dilation in DIL_DILATIONS:
            length = seq // dilation
            t = dil.reshape(batch, length, dilation, dil.shape[1])
            t = jnp.transpose(t, (0, 2, 1, 3)).reshape(batch * dilation, length, dil.shape[1])
            o, lse = _band(t, tq=LANES)
            back = lambda a: jnp.transpose(a.reshape(batch, dilation, length, a.shape[-1]),
                                           (0, 2, 1, 3)).reshape(n, a.shape[-1])
            dils.append(back(o))
            lses.append(back(lse))

        xf = _outproj(xf, fox_o, nsa_o, dils, lses, _regroup_w_out(w_out[l]),
                      attn_post_norm[l].reshape(1, d), tm=tm)
        xf = _ffn(xf, ffn_pre_norm[l].reshape(1, d), w_up[l].astype(BF16), conv_w[l],
                  conv_b[l].reshape(1, -1), w_down[l].astype(BF16), ffn_post_norm[l].reshape(1, d),
                  batch=batch, seq=seq, tm=tm, chunk=256)
    return xf.reshape(batch, seq, d)
```

```python
import functools
import math

import numpy as np
import jax
import jax.numpy as jnp
from jax import lax
from jax.experimental import pallas as pl
from jax.experimental.pallas import tpu as pltpu

HEAD_DIM = 64
FOX_HEADS = 4
NSA_HEADS = 8
NSA_GROUPS = 2
NSA_REP = NSA_HEADS // NSA_GROUPS
DIL_HEADS = 4
DIL_DILATIONS = (1, 4, 16)
DIL_SPAN = 128
ROPE_THETA = 500000.0
ROPE_DIM = HEAD_DIM // 4
ROPE_HALF = ROPE_DIM // 2
CMP_BLOCK = 32
CMP_STRIDE = 16
SEL_BLOCK = 64
SEL_TOPK = 16
NSA_WINDOW = 512
CONV_WIDTH = 3
RMS_EPS = 1e-6
NEG_INF = -1e30
FORCE_SCORE = 1e9
ATTN_SCALE = HEAD_DIM ** -0.5
LOG2E = math.log2(math.e)
Q_SCALE = ATTN_SCALE * LOG2E
LANES = 128
VMEM_LIMIT = 56 * 1024 * 1024
NSA_CHAINS = 4

F32 = jnp.float32
BF16 = jnp.bfloat16


def _dot_nt(a, b):
    return lax.dot_general(a, b, (((1,), (1,)), ((), ())), preferred_element_type=F32)


def _dot(a, b):
    return jnp.dot(a, b, preferred_element_type=F32)


def _rms(x, g):
    return x * lax.rsqrt(jnp.mean(x * x, axis=-1, keepdims=True) + RMS_EPS) * g


def _params(*sem):
    return pltpu.CompilerParams(dimension_semantics=sem, vmem_limit_bytes=VMEM_LIMIT)


def _with_ones(v):
    return jnp.concatenate([v, jnp.ones((v.shape[0], LANES), v.dtype)], axis=1)


def _softmax_pv(s, v1):
    m = jnp.max(s, axis=-1, keepdims=True)
    return m, _dot(jnp.exp2((s - m).astype(BF16)), v1)


def _online_update(s, v1, m, acc):
    m_new = jnp.maximum(m, jnp.max(s, axis=-1, keepdims=True))
    p = jnp.exp2((s - m_new).astype(BF16))
    acc = jnp.exp2(m - m_new) * acc + _dot(p, v1)
    return m_new, acc


_ROPE_CHUNKS = {"fox": (), "nq": (0, 1, 2, 3), "nkv": (0, 2, 4), "dil": (0, 1, 2, 3)}
_SEG_WIDTH = {"fox": 768, "nq": 512, "nkv": 768, "dil": 768}
_SEG_ORDER = ("fox", "nq", "nkv", "dil")
_AUX_OFFSET = sum(_SEG_WIDTH.values())
_W_IN_COLS = _AUX_OFFSET + 2 * LANES


def _inproj_kernel(x_ref, g_ref, w_ref, rc_ref, rs1_ref, rs2_ref, bias_ref,
                   fox_ref, nq_ref, nkv_ref, dil_ref, dil4_ref, dil16_ref, aux_ref,
                   carry_ref, stage_ref, *, tm):
    @pl.when(pl.program_id(1) == 0)
    def _():
        carry_ref[...] = jnp.zeros_like(carry_ref)

    h = _rms(x_ref[...], g_ref[...]).astype(BF16)
    rc, rs1, rs2 = rc_ref[...], rs1_ref[...], rs2_ref[...]
    outs = {"fox": fox_ref, "nq": nq_ref, "nkv": nkv_ref, "dil": dil_ref}
    col = 0
    for name in _SEG_ORDER:
        for c0 in range(0, _SEG_WIDTH[name], 2 * LANES):
            y = _dot(h, w_ref[:, col + c0:col + c0 + 2 * LANES])
            for half in range(2):
                chunk = c0 // LANES + half
                yc = y[:, half * LANES:(half + 1) * LANES]
                if chunk in _ROPE_CHUNKS[name]:
                    yc = (yc * rc + pltpu.roll(yc, ROPE_HALF, 1) * rs1
                          + pltpu.roll(yc, LANES - ROPE_HALF, 1) * rs2)
                outs[name][:, chunk * LANES:(chunk + 1) * LANES] = yc.astype(BF16)
                if name == "dil":
                    stage_ref[...] = yc
                    for dilation, ref in ((DIL_DILATIONS[1], dil4_ref), (DIL_DILATIONS[2], dil16_ref)):
                        for c in range(dilation):
                            part = stage_ref[pl.ds(c, tm // dilation, stride=dilation), :]
                            ref[c, :, chunk * LANES:(chunk + 1) * LANES] = part.astype(BF16)
        col += _SEG_WIDTH[name]

    ya = _dot(h, w_ref[:, _AUX_OFFSET:_AUX_OFFSET + 2 * LANES])[:, :LANES] + bias_ref[...]
    lane = lax.broadcasted_iota(jnp.int32, (tm, LANES), 1)
    logf = jnp.minimum(ya, 0.0) - jnp.log1p(jnp.exp(-jnp.abs(ya)))
    logf = jnp.where(lane < FOX_HEADS, logf, 0.0)
    row = lax.broadcasted_iota(jnp.int32, (tm, tm), 0)
    colm = lax.broadcasted_iota(jnp.int32, (tm, tm), 1)
    tril = jnp.where(colm <= row, 1.0, 0.0).astype(F32)
    csum = jnp.dot(tril, logf, preferred_element_type=F32,
                   precision=lax.Precision.HIGHEST) + carry_ref[...]
    carry_ref[...] = csum[tm - 1:tm, :]
    gate = jax.nn.sigmoid(ya)
    aux_ref[...] = jnp.where(lane < FOX_HEADS, csum * LOG2E,
                             jnp.where(lane < FOX_HEADS + 3 * NSA_HEADS, gate, 0.0))


def _inproj(x, g, w, rc, rs1, rs2, bias, *, batch, seq, tm):
    n, d = x.shape
    nt = seq // tm
    tok = lambda width: pl.BlockSpec((tm, width), lambda b, i: (b * nt + i, 0))
    const = lambda shape: pl.BlockSpec(shape, lambda b, i: (0, 0))
    cls = lambda dil: pl.BlockSpec((None, dil, tm // dil, 768), lambda b, i: (b, 0, i, 0))
    cls_shape = lambda dil: jax.ShapeDtypeStruct((batch, dil, seq // dil, 768), BF16)
    d4, d16 = DIL_DILATIONS[1], DIL_DILATIONS[2]
    return pl.pallas_call(
        functools.partial(_inproj_kernel, tm=tm),
        grid=(batch, nt),
        in_specs=[tok(d), const((1, d)), const(w.shape), tok(LANES), tok(LANES), tok(LANES),
                  const((1, LANES))],
        out_specs=[tok(768), tok(512), tok(768), tok(768), cls(d4), cls(d16), tok(LANES)],
        out_shape=[jax.ShapeDtypeStruct((n, 768), BF16), jax.ShapeDtypeStruct((n, 512), BF16),
                   jax.ShapeDtypeStruct((n, 768), BF16), jax.ShapeDtypeStruct((n, 768), BF16),
                   cls_shape(d4), cls_shape(d16), jax.ShapeDtypeStruct((n, LANES), F32)],
        scratch_shapes=[pltpu.VMEM((1, LANES), F32), pltpu.VMEM((tm, LANES), F32)],
        compiler_params=_params("arbitrary", "arbitrary"),
        name="inproj",
    )(x, g, w, rc, rs1, rs2, bias)


def _stack_heads(q, heads):
    lane = lax.broadcasted_iota(jnp.int32, (1, q.shape[1]), 1)
    blocks = []
    for h in range(heads):
        head = (lane >= h * HEAD_DIM) & (lane < (h + 1) * HEAD_DIM)
        blocks.append(jnp.where(head, q, jnp.zeros_like(q)))
    return jnp.concatenate(blocks, axis=0)


def _pair_values(v):
    return _with_ones(v[:, :LANES]), _with_ones(v[:, LANES:])


def _pair_pv(p, v_pair):
    half = p.shape[0] // 2
    return jnp.concatenate([_dot(p[:half], v_pair[0]), _dot(p[half:], v_pair[1])], axis=0)


def _unstack_pairs(x, tq):
    return jnp.concatenate(_unstack_halves(x, tq), axis=1)


def _unstack_halves(x, tq):
    lane = lax.broadcasted_iota(jnp.int32, (1, LANES), 1)
    return [jnp.where(lane < HEAD_DIM, x[(2 * c) * tq:(2 * c + 1) * tq],
                      x[(2 * c + 1) * tq:(2 * c + 2) * tq]) for c in range(2)]


def _fox_kernel(q_ref, k_ref, v_ref, ck_ref, o_ref, *, tq, tk):
    i = pl.program_id(1)
    start = i * tq
    rows = FOX_HEADS * tq
    qst = _stack_heads(q_ref[...], FOX_HEADS)
    qpos = start + (lax.broadcasted_iota(jnp.int32, (rows, 1), 0) & (tq - 1))
    col = lax.broadcasted_iota(jnp.int32, (1, tk), 1)
    jd = start // tk

    def tile(j, masked):
        ks0 = pl.multiple_of(j * tk, tk)
        k = k_ref[pl.ds(ks0, tk), :]
        scores = []
        for h in range(FOX_HEADS):
            s = _dot_nt(qst[h * tq:(h + 1) * tq], k) - ck_ref[h:h + 1, pl.ds(ks0, tk)]
            if masked:
                s = jnp.where(ks0 + col <= qpos[:tq], s, NEG_INF)
            scores.append(s)
        v_pair = _pair_values(v_ref[pl.ds(ks0, tk), :])
        return scores, [v_pair[h // 2] for h in range(FOX_HEADS)]

    def body(j, st):
        scores, values = tile(j, False)
        return tuple(_online_update(s, v1, *st_h) for s, v1, st_h in zip(scores, values, st))

    scores, values = tile(jd, True)
    state = lax.fori_loop(0, jd, body, tuple(_softmax_pv(s, v1) for s, v1 in zip(scores, values)))
    acc = jnp.concatenate([acc_h for _, acc_h in state], axis=0)
    o_ref[...] = _unstack_pairs(acc[:, :LANES] / acc[:, LANES:], tq).astype(o_ref.dtype)


def _fox(fox, ck_t, *, batch, seq, tq, tk):
    n = fox.shape[0]
    nt = seq // tq
    width = FOX_HEADS * HEAD_DIM
    return pl.pallas_call(
        functools.partial(_fox_kernel, tq=tq, tk=tk),
        grid=(batch, nt),
        in_specs=[pl.BlockSpec((tq, width), lambda b, i: (b * nt + i, 0)),
                  pl.BlockSpec((seq, width), lambda b, i: (b, 1)),
                  pl.BlockSpec((seq, width), lambda b, i: (b, 2)),
                  pl.BlockSpec((None, 8, seq), lambda b, i: (b, 0, 0))],
        out_specs=pl.BlockSpec((tq, width), lambda b, i: (b * nt + i, 0)),
        out_shape=jax.ShapeDtypeStruct((n, width), BF16),
        compiler_params=_params("arbitrary", "arbitrary"),
        name="fox",
    )(fox, fox, fox, ck_t)


def _gelu_tanh(x):
    return 0.5 * x * (1.0 + jnp.tanh(math.sqrt(2.0 / math.pi) * (x + 0.044715 * (x * x * x))))


def _compress_kernel(xk_ref, xv_ref, w1k_ref, w2k_ref, pk_ref, w1v_ref, w2v_ref, pv_ref,
                     kc_ref, vc_ref):
    half = CMP_STRIDE * HEAD_DIM

    def run(x_ref, w1_ref, w2_ref, p_ref, o_ref):
        w1 = w1_ref[...]
        posb = _dot(p_ref[...].astype(BF16), w1)[0:1, :]
        out = None
        for g in range(NSA_GROUPS):
            x = x_ref[g]
            n16 = x.shape[0]
            y1 = _dot(x, w1[:half, :])
            y2 = _dot(x, w1[half:, :])
            hid = y1 + pltpu.roll(y2, n16 - 1, 0) + posb
            a = _gelu_tanh(hid).astype(BF16)
            og = _dot(a, w2_ref[g])
            out = og if out is None else out + og
        o_ref[...] = out.astype(o_ref.dtype)

    run(xk_ref, w1k_ref, w2k_ref, pk_ref, kc_ref)
    run(xv_ref, w1v_ref, w2v_ref, pv_ref, vc_ref)


def _compress(xk, xv, w1k, w2k, pk, w1v, w2v, pv):
    batch, groups, n16, feat = xk.shape
    xspec = pl.BlockSpec((None, groups, n16, feat), lambda b: (b, 0, 0, 0))
    full = lambda a: pl.BlockSpec(a.shape, lambda b: (0,) * a.ndim)
    ospec = pl.BlockSpec((None, n16, LANES), lambda b: (b, 0, 0))
    return pl.pallas_call(
        _compress_kernel,
        grid=(batch,),
        in_specs=[xspec, xspec, full(w1k), full(w2k), full(pk), full(w1v), full(w2v), full(pv)],
        out_specs=[ospec, ospec],
        out_shape=[jax.ShapeDtypeStruct((batch, n16, LANES), BF16)] * 2,
        compiler_params=_params("arbitrary"),
        name="nsa_compress",
    )(xk, xv, w1k, w2k, pk, w1v, w2v, pv)


def _nsa_kernel(q_ref, kcmp_ref, vcmp_ref, ks_ref, vs_ref, kw_ref, vw_ref, aux_ref, o_ref,
                *, tq, tk, seq, chains):
    i = pl.program_id(1)
    start = i * tq
    n_cmp = kcmp_ref.shape[0]
    wspan = NSA_WINDOW + tq

    rows = NSA_HEADS * tq
    crow = rows // chains
    n_sel = seq // SEL_BLOCK

    lane1 = lax.broadcasted_iota(jnp.int32, (1, LANES), 1)
    qpos_col = start + (lax.broadcasted_iota(jnp.int32, (rows, 1), 0) & (tq - 1))
    qpos_row = start + (lax.broadcasted_iota(jnp.int32, (1, NSA_GROUPS * tq), 1) & (tq - 1))

    kcmp = kcmp_ref[...]
    vcmp = vcmp_ref[...]
    cmp_end = lax.broadcasted_iota(jnp.int32, (1, n_cmp), 1) * CMP_STRIDE + (CMP_BLOCK - 1)
    cmp_ok = cmp_end <= qpos_col
    cs = lax.broadcasted_iota(jnp.int32, (n_cmp, LANES), 0) * CMP_STRIDE
    ss = lax.broadcasted_iota(jnp.int32, (n_cmp, LANES), 1) * SEL_BLOCK
    overlap = jnp.where((cs < ss + SEL_BLOCK) & (cs + CMP_BLOCK > ss), 1.0, 0.0).astype(F32)

    blk = lax.broadcasted_iota(jnp.int32, (LANES, NSA_GROUPS * tq), 0)
    cur = qpos_row // SEL_BLOCK
    forced = (blk == 0) | (blk == cur) | (blk == cur - 1)
    future = blk * SEL_BLOCK > qpos_row

    wstart = pl.multiple_of(jnp.maximum(start - NSA_WINDOW, 0), tq)
    kwin = kw_ref[pl.ds(wstart, wspan), :]
    vwin = vw_ref[pl.ds(wstart, wspan), :]
    kpos_w = wstart + lax.broadcasted_iota(jnp.int32, (1, wspan), 1)
    win_ok = (kpos_w <= qpos_col) & (kpos_w > qpos_col - NSA_WINDOW)

    jd = start // tk
    col_tk = lax.broadcasted_iota(jnp.int32, (1, tk), 1)
    key_blk_iota = lax.broadcasted_iota(jnp.int32, (tk, LANES), 0) // SEL_BLOCK
    key_lane = lax.broadcasted_iota(jnp.int32, (tk, LANES), 1)

    q_blocks = []
    for g in range(NSA_GROUPS):
        grp = (lane1 >= g * HEAD_DIM) & (lane1 < (g + 1) * HEAD_DIM)
        for r in range(NSA_REP):
            qc = q_ref[:, r * LANES:(r + 1) * LANES]
            q_blocks.append(jnp.where(grp, qc, jnp.zeros_like(qc)))
    qst = jnp.concatenate(q_blocks, axis=0)

    s = jnp.where(cmp_ok, _dot_nt(qst, kcmp), NEG_INF)
    m = jnp.max(s, axis=-1, keepdims=True)
    e = jnp.where(cmp_ok, jnp.exp2(s - m), 0.0)
    p = e / jnp.maximum(jnp.sum(e, axis=-1, keepdims=True), 1e-30)
    o_cmp = _dot(p.astype(BF16), vcmp)

    p_sum = []
    for g in range(NSA_GROUPS):
        blocks = [p[(g * NSA_REP + r) * tq:(g * NSA_REP + r + 1) * tq] for r in range(NSA_REP)]
        p_sum.append((blocks[0] + blocks[1]) + (blocks[2] + blocks[3]))
    imp = jnp.dot(jnp.concatenate(p_sum, axis=0), overlap, preferred_element_type=F32,
                  precision=lax.Precision.HIGHEST)
    imp_t = jnp.concatenate([imp[g * tq:(g + 1) * tq].T for g in range(NSA_GROUPS)], axis=1)
    imp_t = jnp.where(future, NEG_INF, jnp.where(forced, FORCE_SCORE, imp_t))
    val = imp_t[:n_sel, :]
    blk_s = lax.broadcasted_iota(jnp.int32, val.shape, 0)
    rank = jnp.zeros(val.shape, jnp.int32)
    for jp in range(n_sel):
        other = imp_t[jp:jp + 1, :]
        beats = (other > val) | ((other == val) & (blk_s > jp))
        rank = rank + jnp.where(beats, 1, 0)
    sb_blocks = []
    for g in range(NSA_GROUPS):
        chosen = rank[:, g * tq:(g + 1) * tq] < min(SEL_TOPK, n_sel)
        sb_t = jnp.concatenate([jnp.where(chosen, 0.0, NEG_INF).astype(F32),
                                jnp.full((LANES - n_sel, tq), NEG_INF, F32)], axis=0)
        sb_blocks.extend([sb_t.T.astype(BF16)] * NSA_REP)
    qa = jnp.concatenate([qst, jnp.concatenate(sb_blocks, axis=0)], axis=1)

    def sel_tile(j, masked):
        ks0 = pl.multiple_of(j * tk, tk)
        k = ks_ref[pl.ds(ks0, tk), :]
        v = vs_ref[pl.ds(ks0, tk), :]
        onehot = jnp.where(key_lane == key_blk_iota + j * (tk // SEL_BLOCK), 1.0, 0.0)
        ka = jnp.concatenate([k, onehot.astype(BF16)], axis=1)
        scores = []
        for c in range(chains):
            s = _dot_nt(qa[c * crow:(c + 1) * crow], ka)
            if masked:
                s = jnp.where(ks0 + col_tk <= qpos_col[:crow], s, NEG_INF)
            scores.append(s)
        return scores, _with_ones(v)

    def body(j, st):
        scores, v1 = sel_tile(j, False)
        return tuple(_online_update(s, v1, *st_c) for s, st_c in zip(scores, st))

    scores, v1 = sel_tile(jd, True)
    state = lax.fori_loop(0, jd, body, tuple(_softmax_pv(s, v1) for s in scores))
    acc = jnp.concatenate([acc_c for _, acc_c in state], axis=0)
    o_sel = acc[:, :LANES] / acc[:, LANES:]

    vwin1 = _with_ones(vwin)
    accs = []
    for c in range(chains):
        s = jnp.where(win_ok[:crow], _dot_nt(qst[c * crow:(c + 1) * crow], kwin), NEG_INF)
        accs.append(_softmax_pv(s, vwin1)[1])
    acc = jnp.concatenate(accs, axis=0)
    o_win = acc[:, :LANES] / acc[:, LANES:]

    aux = aux_ref[...]
    for r in range(NSA_REP):
        per_group = []
        for g in range(NSA_GROUPS):
            head = g * NSA_REP + r
            rs = slice(head * tq, (head + 1) * tq)
            c = FOX_HEADS + 3 * head
            per_group.append(aux[:, c:c + 1] * o_cmp[rs] + aux[:, c + 1:c + 2] * o_sel[rs]
                             + aux[:, c + 2:c + 3] * o_win[rs])
        o = jnp.where(lane1 < HEAD_DIM, per_group[0], per_group[1])
        o_ref[:, r * LANES:(r + 1) * LANES] = o.astype(o_ref.dtype)


def _nsa(nq, nkv, kcmp, vcmp, aux, *, batch, seq, tq, tk):
    n = nq.shape[0]
    nt = seq // tq
    n16 = kcmp.shape[1]
    kv = lambda c: pl.BlockSpec((seq, LANES), lambda b, i: (b, c))
    cmp_spec = pl.BlockSpec((None, n16, LANES), lambda b, i: (b, 0, 0))
    width = NSA_HEADS * HEAD_DIM
    return pl.pallas_call(
        functools.partial(_nsa_kernel, tq=tq, tk=tk, seq=seq, chains=NSA_CHAINS),
        grid=(batch, nt),
        in_specs=[pl.BlockSpec((tq, width), lambda b, i: (b * nt + i, 0)),
                  cmp_spec, cmp_spec, kv(2), kv(3), kv(4), kv(5),
                  pl.BlockSpec((tq, LANES), lambda b, i: (b * nt + i, 0))],
        out_specs=pl.BlockSpec((tq, width), lambda b, i: (b * nt + i, 0)),
        out_shape=jax.ShapeDtypeStruct((n, width), BF16),
        compiler_params=_params("arbitrary", "arbitrary"),
        name="nsa",
    )(nq, kcmp, vcmp, nkv, nkv, nkv, nkv, aux)


_DIL_UNROLL = 4


def _dilated_kernel(q1_ref, k1_ref, v1_ref, q4_ref, k4_ref, v4_ref, q16_ref, k16_ref, v16_ref,
                    o_ref, osc_ref, lsc_ref, *, tile, seq):
    i = pl.program_id(1)
    tq = DIL_SPAN
    rows = DIL_HEADS * tq
    row_u = lax.broadcasted_iota(jnp.int32, (rows, 1), 0) & (tq - 1)
    refs = ((q1_ref, k1_ref, v1_ref), (q4_ref, k4_ref, v4_ref), (q16_ref, k16_ref, v16_ref))
    for pat, (dilation, (q_ref, k_ref, v_ref)) in enumerate(zip(DIL_DILATIONS, refs)):
        length = seq // dilation
        span = min(2 * tq, length)
        per_class = tile // (tq * dilation)
        n_sub = tile // tq
        base = i * (tile // dilation)
        col_u = lax.broadcasted_iota(jnp.int32, (1, span), 1)

        def sub_tile(sub):
            c = sub // per_class
            w = sub % per_class
            u0 = pl.multiple_of(w * tq, tq)
            ks0 = pl.multiple_of(jnp.clip(base + u0 - DIL_SPAN, 0, length - span), tq)
            qst = _stack_heads(q_ref[c, pl.ds(u0, tq), :], DIL_HEADS)
            delta = (base + u0 + row_u) - (ks0 + col_u)
            s = jnp.where((delta >= 0) & (delta <= DIL_SPAN),
                          _dot_nt(qst, k_ref[c, pl.ds(ks0, span), :]), NEG_INF)
            m = jnp.max(s, axis=-1, keepdims=True)
            acc = _pair_pv(jnp.exp2((s - m).astype(BF16)),
                           _pair_values(v_ref[c, pl.ds(ks0, span), :]))
            den = acc[:, LANES:]
            dst = pl.ds(u0 * dilation + c, tq, stride=dilation)
            for half, (o, lse) in enumerate(zip(_unstack_halves(acc[:, :LANES] / den, tq),
                                                _unstack_halves(m + jnp.log2(den), tq))):
                osc_ref[pat, half, dst, :] = o
                lsc_ref[pat, half, dst, :] = lse

        def trip(it, carry):
            for k in range(_DIL_UNROLL):
                sub_tile(it * _DIL_UNROLL + k)
            return carry

        lax.fori_loop(0, n_sub // _DIL_UNROLL, trip, 0)

    chunk = 2 * tq
    for r0 in range(0, tile, chunk):
        for half in range(2):
            ls = [lsc_ref[p, half, r0:r0 + chunk, :] for p in range(3)]
            top = jnp.maximum(jnp.maximum(ls[0], ls[1]), ls[2])
            es = [jnp.exp2(l - top) for l in ls]
            num = (es[0] * osc_ref[0, half, r0:r0 + chunk, :] + es[1] * osc_ref[1, half, r0:r0 + chunk, :]
                   + es[2] * osc_ref[2, half, r0:r0 + chunk, :])
            o_ref[r0:r0 + chunk, half * LANES:(half + 1) * LANES] = (
                num / (es[0] + es[1] + es[2])).astype(o_ref.dtype)


def _dilated(dil1, dil4, dil16, *, batch, seq, tile):
    width = DIL_HEADS * HEAD_DIM
    nt = seq // tile
    specs = []
    for arr in (dil1, dil4, dil16):
        dilation, length = arr.shape[1], arr.shape[2]
        specs.append(pl.BlockSpec((None, dilation, tile // dilation, width), lambda b, i: (b, 0, i, 0)))
        for part in (1, 2):
            specs.append(pl.BlockSpec((None, dilation, length, width),
                                      lambda b, i, part=part: (b, 0, 0, part),
                                      pipeline_mode=pl.Buffered(1)))
    return pl.pallas_call(
        functools.partial(_dilated_kernel, tile=tile, seq=seq),
        grid=(batch, nt),
        in_specs=specs,
        out_specs=pl.BlockSpec((tile, width), lambda b, i: (b * nt + i, 0)),
        out_shape=jax.ShapeDtypeStruct((batch * seq, width), BF16),
        scratch_shapes=[pltpu.VMEM((3, 2, tile, LANES), F32), pltpu.VMEM((3, 2, tile, LANES), F32)],
        compiler_params=_params("arbitrary", "arbitrary"),
        name="dilated",
    )(dil1, dil1, dil1, dil4, dil4, dil4, dil16, dil16, dil16)


def _outproj_kernel(x_ref, fox_ref, nsa_ref, dil_ref, w_ref, g_ref, o_ref):
    fw = FOX_HEADS * HEAD_DIM
    nw = NSA_HEADS * HEAD_DIM
    y = (_dot(fox_ref[...], w_ref[:fw, :]) + _dot(nsa_ref[...], w_ref[fw:fw + nw, :])
         + _dot(dil_ref[...], w_ref[fw + nw:, :]))
    o_ref[...] = x_ref[...] + _rms(y, g_ref[...])


def _outproj(x, fox_o, nsa_o, dil_o, w, g, *, tm):
    n, d = x.shape
    tok = lambda width: pl.BlockSpec((tm, width), lambda i: (i, 0))
    const = lambda shape: pl.BlockSpec(shape, lambda i: (0, 0))
    return pl.pallas_call(
        _outproj_kernel,
        grid=(n // tm,),
        in_specs=[tok(d), tok(fox_o.shape[1]), tok(nsa_o.shape[1]), tok(dil_o.shape[1]),
                  const(w.shape), const((1, d))],
        out_specs=tok(d),
        out_shape=jax.ShapeDtypeStruct((n, d), F32),
        compiler_params=_params("arbitrary"),
        name="outproj",
    )(x, fox_o, nsa_o, dil_o, w, g)


def _ffn_kernel(x_ref, gpre_ref, wup_ref, cw_ref, cb_ref, wdn_ref, gpost_ref, o_ref, tail_ref,
                *, tm, d_ff, chunk):
    @pl.when(pl.program_id(1) == 0)
    def _():
        tail_ref[...] = jnp.zeros_like(tail_ref)

    x = x_ref[...]
    h = _rms(x, gpre_ref[...]).astype(BF16)
    row = lax.broadcasted_iota(jnp.int32, (tm, 1), 0)

    def conv(c0):
        u = _dot(h, wup_ref[:, c0:c0 + chunk])
        tail = tail_ref[:, c0:c0 + chunk]
        u1 = jnp.where(row == 0, tail[7:8, :], pltpu.roll(u, 1, 0))
        u2 = jnp.where(row == 0, tail[6:7, :], jnp.where(row == 1, tail[7:8, :], pltpu.roll(u, 2, 0)))
        tail_ref[:, c0:c0 + chunk] = u[tm - 8:, :]
        return (cb_ref[:, c0:c0 + chunk] + u2 * cw_ref[0:1, c0:c0 + chunk]
                + u1 * cw_ref[1:2, c0:c0 + chunk] + u * cw_ref[2:3, c0:c0 + chunk])

    y = jnp.zeros((tm, x.shape[1]), F32)
    for c0 in range(0, d_ff, chunk):
        a = conv(c0)
        b = conv(d_ff + c0)
        act = (a * jax.nn.sigmoid(a) * b).astype(BF16)
        y = y + _dot(act, wdn_ref[c0:c0 + chunk, :])
    o_ref[...] = x + _rms(y, gpost_ref[...])


def _ffn(x, gpre, wup, cw, cb, wdn, gpost, *, batch, seq, tm, chunk):
    n, d = x.shape
    d_ff = wdn.shape[0]
    nt = seq // tm
    tok = pl.BlockSpec((tm, d), lambda b, i: (b * nt + i, 0))
    const = lambda a: pl.BlockSpec(a.shape, lambda b, i: (0, 0), pipeline_mode=pl.Buffered(1))
    return pl.pallas_call(
        functools.partial(_ffn_kernel, tm=tm, d_ff=d_ff, chunk=chunk),
        grid=(batch, nt),
        in_specs=[tok, const(gpre), const(wup), const(cw), const(cb), const(wdn), const(gpost)],
        out_specs=tok,
        out_shape=jax.ShapeDtypeStruct((n, d), F32),
        scratch_shapes=[pltpu.VMEM((8, 2 * d_ff), F32)],
        compiler_params=_params("arbitrary", "arbitrary"),
        name="conv_ffn",
    )(x, gpre, wup, cw, cb, wdn, gpost)


def _rope_tables(positions):
    inv_freq = ROPE_THETA ** (-2.0 * jnp.arange(ROPE_HALF, dtype=F32) / ROPE_DIM)
    ang = positions.astype(F32).reshape(-1, 1) * inv_freq
    cos, sin = jnp.cos(ang), jnp.sin(ang)
    n = ang.shape[0]
    ones = jnp.ones((n, HEAD_DIM - ROPE_DIM), F32)
    zeros_h = jnp.zeros((n, ROPE_HALF), F32)
    zeros_r = jnp.zeros((n, HEAD_DIM - ROPE_DIM), F32)
    rc = jnp.concatenate([cos, cos, ones], axis=1)
    rs1 = jnp.concatenate([zeros_h, sin, zeros_r], axis=1)
    rs2 = jnp.concatenate([-sin, zeros_h, zeros_r], axis=1)
    tile2 = lambda t: jnp.concatenate([t, t], axis=1)
    return tile2(rc), tile2(rs1), tile2(rs2)


def _nsa_head_perm():
    cols = []
    for r in range(NSA_REP):
        for g in range(NSA_GROUPS):
            h = g * NSA_REP + r
            cols.extend(range(h * HEAD_DIM, (h + 1) * HEAD_DIM))
    return np.asarray(cols)


def _regroup_w_in(w):
    fw, nw, kvw, dw = FOX_HEADS * HEAD_DIM, NSA_HEADS * HEAD_DIM, NSA_GROUPS * HEAD_DIM, DIL_HEADS * HEAD_DIM
    sizes = [fw, fw, fw, FOX_HEADS, nw] + [kvw] * 6 + [3 * NSA_HEADS, dw, dw, dw]
    offs = np.concatenate([[0], np.cumsum(sizes)])
    part = lambda idx: w[:, offs[idx]:offs[idx + 1]]
    fq, fk, fv, ff, nq = (part(t) for t in range(5))
    kvs = [part(t) for t in range(5, 11)]
    ng, dq, dk, dv = (part(t) for t in range(11, 15))
    nq = nq[:, _nsa_head_perm()]
    pad = jnp.zeros((w.shape[0], 2 * LANES - FOX_HEADS - 3 * NSA_HEADS), w.dtype)
    cols = [fq * Q_SCALE, fk, fv, nq * Q_SCALE] + kvs + [dq * Q_SCALE, dk, dv, ff, ng, pad]
    return jnp.concatenate(cols, axis=1).astype(BF16)


def _regroup_w_out(w):
    fw, nw = FOX_HEADS * HEAD_DIM, NSA_HEADS * HEAD_DIM
    return jnp.concatenate([w[:fw], w[fw:fw + nw][_nsa_head_perm()], w[fw + nw:]], axis=0).astype(BF16)


def _blocks16(t, batch, seq):
    t = t.reshape(batch, seq // CMP_STRIDE, CMP_STRIDE, NSA_GROUPS, HEAD_DIM)
    return jnp.transpose(t, (0, 3, 1, 2, 4)).reshape(batch, NSA_GROUPS, seq // CMP_STRIDE, CMP_STRIDE * HEAD_DIM)


def _place_w2(w2):
    z = jnp.zeros_like(w2)
    return jnp.stack([jnp.concatenate([w2, z], axis=1), jnp.concatenate([z, w2], axis=1)]).astype(BF16)


def _pad_rows8(p):
    flat = p.reshape(1, -1)
    return jnp.concatenate([flat, jnp.zeros((7, flat.shape[1]), flat.dtype)], axis=0)


def kernel(x, positions, attn_pre_norm, attn_post_norm, ffn_pre_norm, ffn_post_norm, w_in, b_forget, b_nsa_gate, cmp_pos_k, cmp_w1_k, cmp_w2_k, cmp_pos_v, cmp_w1_v, cmp_w2_v, w_out, w_up, conv_w, conv_b, w_down):
    batch, seq, d = x.shape
    depth = w_in.shape[0]
    n = batch * seq
    tm = min(512, seq)
    assert seq % tm == 0 and seq % (16 * LANES) == 0

    rc, rs1, rs2 = _rope_tables(positions)
    xf = x.reshape(n, d)
    for l in range(depth):
        bias = jnp.concatenate([b_forget[l], b_nsa_gate[l],
                                jnp.zeros((LANES - FOX_HEADS - 3 * NSA_HEADS,), F32)]).reshape(1, LANES)
        fox, nq, nkv, dil, dil4, dil16, aux = _inproj(
            xf, attn_pre_norm[l].reshape(1, d), _regroup_w_in(w_in[l]), rc, rs1, rs2, bias,
            batch=batch, seq=seq, tm=tm)

        ck_t = jnp.transpose(aux.reshape(batch, seq, LANES)[:, :, :8], (0, 2, 1))
        fox_o = _fox(fox, ck_t, batch=batch, seq=seq, tq=tm // 2, tk=tm)

        kcmp, vcmp = _compress(
            _blocks16(nkv[:, 0:LANES], batch, seq), _blocks16(nkv[:, LANES:2 * LANES], batch, seq),
            cmp_w1_k[l].astype(BF16), _place_w2(cmp_w2_k[l]), _pad_rows8(cmp_pos_k[l]),
            cmp_w1_v[l].astype(BF16), _place_w2(cmp_w2_v[l]), _pad_rows8(cmp_pos_v[l]))
        nsa_o = _nsa(nq, nkv, kcmp, vcmp, aux, batch=batch, seq=seq, tq=LANES, tk=tm)

        dil_o = _dilated(dil.reshape(batch, 1, seq, dil.shape[1]), dil4, dil16,
                         batch=batch, seq=seq, tile=16 * LANES)

        xf = _outproj(xf, fox_o, nsa_o, dil_o, _regroup_w_out(w_out[l]),
                      attn_post_norm[l].reshape(1, d), tm=tm)
        xf = _ffn(xf, ffn_pre_norm[l].reshape(1, d), w_up[l].astype(BF16), conv_w[l],
                  conv_b[l].reshape(1, -1), w_down[l].astype(BF16), ffn_post_norm[l].reshape(1, d),
                  batch=batch, seq=seq, tm=tm, chunk=256)
    return xf.reshape(batch, seq, d)
```

```python
import functools
import math

import numpy as np
import jax
import jax.numpy as jnp
from jax import lax
from jax.experimental import pallas as pl
from jax.experimental.pallas import tpu as pltpu

HEAD_DIM = 64
FOX_HEADS = 4
NSA_HEADS = 8
NSA_GROUPS = 2
NSA_REP = NSA_HEADS // NSA_GROUPS
DIL_HEADS = 4
DIL_DILATIONS = (1, 4, 16)
DIL_SPAN = 128
ROPE_THETA = 500000.0
ROPE_DIM = HEAD_DIM // 4
ROPE_HALF = ROPE_DIM // 2
CMP_BLOCK = 32
CMP_STRIDE = 16
SEL_BLOCK = 64
SEL_TOPK = 16
NSA_WINDOW = 512
CONV_WIDTH = 3
RMS_EPS = 1e-6
NEG_INF = -1e30
FORCE_SCORE = 1e9
ATTN_SCALE = HEAD_DIM ** -0.5
LOG2E = math.log2(math.e)
Q_SCALE = ATTN_SCALE * LOG2E
LANES = 128
VMEM_LIMIT = 56 * 1024 * 1024
FLASH_UNROLL = 2
NSA_CHAINS = 4

F32 = jnp.float32
BF16 = jnp.bfloat16


def _dot_nt(a, b):
    return lax.dot_general(a, b, (((1,), (1,)), ((), ())), preferred_element_type=F32)


def _dot(a, b):
    return jnp.dot(a, b, preferred_element_type=F32)


def _rms(x, g):
    return x * lax.rsqrt(jnp.mean(x * x, axis=-1, keepdims=True) + RMS_EPS) * g


def _params(*sem):
    return pltpu.CompilerParams(dimension_semantics=sem, vmem_limit_bytes=VMEM_LIMIT)


def _with_ones(v):
    return jnp.concatenate([v, jnp.ones((v.shape[0], LANES), v.dtype)], axis=1)


def _softmax_pv(s, v1):
    m = jnp.max(s, axis=-1, keepdims=True)
    return m, _dot(jnp.exp2((s - m).astype(BF16)), v1)


def _flash_tiles(n_before, first, score_fn, value_fn):
    state = []
    for s, v1 in zip(score_fn(first, True), value_fn(first)):
        m = jnp.max(s, axis=-1, keepdims=True)
        state.append((m, _dot(jnp.exp2((s - m).astype(BF16)), v1)))

    def step(j, state):
        out = []
        for s, v1, (m, acc) in zip(score_fn(j, False), value_fn(j), state):
            m_new = jnp.maximum(m, jnp.max(s, axis=-1, keepdims=True))
            p = jnp.exp2((s - m_new).astype(BF16))
            out.append((m_new, jnp.exp2(m - m_new) * acc + _dot(p, v1)))
        return tuple(out)

    def group(t, state):
        for u in range(FLASH_UNROLL):
            state = step(t * FLASH_UNROLL + u, state)
        return state

    n_groups = n_before // FLASH_UNROLL
    state = lax.fori_loop(0, n_groups, group, tuple(state))
    state = lax.fori_loop(n_groups * FLASH_UNROLL, n_before, step, state)
    return [acc for _, acc in state]


_ROPE_CHUNKS = {"fox": (), "nq": (0, 1, 2, 3), "nkv": (0, 2, 4), "dil": (0, 1, 2, 3)}
_SEG_WIDTH = {"fox": 768, "nq": 512, "nkv": 768, "dil": 768}
_SEG_ORDER = ("fox", "nq", "nkv", "dil")
_AUX_OFFSET = sum(_SEG_WIDTH.values())
_W_IN_COLS = _AUX_OFFSET + 2 * LANES


def _inproj_kernel(x_ref, g_ref, w_ref, rc_ref, rs1_ref, rs2_ref, bias_ref,
                   fox_ref, nq_ref, nkv_ref, dil_ref, dil4_ref, dil16_ref, aux_ref, ck_ref,
                   xk_ref, xv_ref, carry_ref, stage_ref, *, tm):
    @pl.when(pl.program_id(1) == 0)
    def _():
        carry_ref[...] = jnp.zeros_like(carry_ref)

    h = _rms(x_ref[...], g_ref[...]).astype(BF16)
    rc, rs1, rs2 = rc_ref[...], rs1_ref[...], rs2_ref[...]
    outs = {"fox": fox_ref, "nq": nq_ref, "nkv": nkv_ref, "dil": dil_ref}
    col = 0
    for name in _SEG_ORDER:
        for c0 in range(0, _SEG_WIDTH[name], 2 * LANES):
            y = _dot(h, w_ref[:, col + c0:col + c0 + 2 * LANES])
            for half in range(2):
                chunk = c0 // LANES + half
                yc = y[:, half * LANES:(half + 1) * LANES]
                if chunk in _ROPE_CHUNKS[name]:
                    yc = (yc * rc + pltpu.roll(yc, ROPE_HALF, 1) * rs1
                          + pltpu.roll(yc, LANES - ROPE_HALF, 1) * rs2)
                outs[name][:, chunk * LANES:(chunk + 1) * LANES] = yc.astype(BF16)
                if name == "nkv" and chunk < 2:
                    stage_ref[...] = yc
                    for t in range(CMP_STRIDE):
                        part = stage_ref[pl.ds(t, tm // CMP_STRIDE, stride=CMP_STRIDE), :]
                        (xk_ref, xv_ref)[chunk][:, t * LANES:(t + 1) * LANES] = part.astype(BF16)
                if name == "dil":
                    stage_ref[...] = yc
                    for dilation, ref in ((DIL_DILATIONS[1], dil4_ref), (DIL_DILATIONS[2], dil16_ref)):
                        for c in range(dilation):
                            part = stage_ref[pl.ds(c, tm // dilation, stride=dilation), :]
                            ref[c, :, chunk * LANES:(chunk + 1) * LANES] = part.astype(BF16)
        col += _SEG_WIDTH[name]

    ya = _dot(h, w_ref[:, _AUX_OFFSET:_AUX_OFFSET + 2 * LANES])[:, :LANES] + bias_ref[...]
    aux_ref[...] = jax.nn.sigmoid(ya)
    logf = jnp.minimum(ya, 0.0) - jnp.log1p(jnp.exp(-jnp.abs(ya)))
    csum = logf.T[:8, :]
    lane = lax.broadcasted_iota(jnp.int32, csum.shape, 1)
    shift = 1
    while shift < tm:
        csum = csum + jnp.where(lane >= shift, pltpu.roll(csum, shift, 1), 0.0)
        shift *= 2
    csum = csum + jnp.concatenate([carry_ref[...]] * (tm // LANES), axis=1)
    carry_ref[...] = jnp.broadcast_to(csum[:, tm - 1:tm], carry_ref.shape)
    ck_ref[...] = csum * LOG2E


def _inproj(x, g, w, rc, rs1, rs2, bias, *, batch, seq, tm):
    n, d = x.shape
    nt = seq // tm
    tok = lambda width: pl.BlockSpec((tm, width), lambda b, i: (b * nt + i, 0))
    const = lambda shape: pl.BlockSpec(shape, lambda b, i: (0, 0))
    cls = lambda dil: pl.BlockSpec((None, dil, tm // dil, 768), lambda b, i: (b, 0, i, 0))
    cls_shape = lambda dil: jax.ShapeDtypeStruct((batch, dil, seq // dil, 768), BF16)
    d4, d16 = DIL_DILATIONS[1], DIL_DILATIONS[2]
    blk16 = pl.BlockSpec((None, tm // CMP_STRIDE, CMP_STRIDE * LANES), lambda b, i: (b, i, 0))
    blk16_shape = jax.ShapeDtypeStruct((batch, seq // CMP_STRIDE, CMP_STRIDE * LANES), BF16)
    return pl.pallas_call(
        functools.partial(_inproj_kernel, tm=tm),
        grid=(batch, nt),
        in_specs=[tok(d), const((1, d)), const(w.shape), tok(LANES), tok(LANES), tok(LANES),
                  const((1, LANES))],
        out_specs=[tok(768), tok(512), tok(768), tok(768), cls(d4), cls(d16), tok(LANES),
                   pl.BlockSpec((None, 8, tm), lambda b, i: (b, 0, i)), blk16, blk16],
        out_shape=[jax.ShapeDtypeStruct((n, 768), BF16), jax.ShapeDtypeStruct((n, 512), BF16),
                   jax.ShapeDtypeStruct((n, 768), BF16), jax.ShapeDtypeStruct((n, 768), BF16),
                   cls_shape(d4), cls_shape(d16), jax.ShapeDtypeStruct((n, LANES), F32),
                   jax.ShapeDtypeStruct((batch, 8, seq), F32), blk16_shape, blk16_shape],
        scratch_shapes=[pltpu.VMEM((8, LANES), F32), pltpu.VMEM((tm, LANES), F32)],
        compiler_params=_params("arbitrary", "arbitrary"),
        name="inproj",
    )(x, g, w, rc, rs1, rs2, bias)


def _stack_heads(q, heads):
    lane = lax.broadcasted_iota(jnp.int32, (1, q.shape[1]), 1)
    blocks = []
    for h in range(heads):
        head = (lane >= h * HEAD_DIM) & (lane < (h + 1) * HEAD_DIM)
        blocks.append(jnp.where(head, q, jnp.zeros_like(q)))
    return jnp.concatenate(blocks, axis=0)


def _pair_values(v):
    return _with_ones(v[:, :LANES]), _with_ones(v[:, LANES:])


def _pair_pv(p, v_pair):
    half = p.shape[0] // 2
    return jnp.concatenate([_dot(p[:half], v_pair[0]), _dot(p[half:], v_pair[1])], axis=0)


def _unstack_pairs(x, tq):
    return jnp.concatenate(_unstack_halves(x, tq), axis=1)


def _unstack_halves(x, tq):
    lane = lax.broadcasted_iota(jnp.int32, (1, LANES), 1)
    return [jnp.where(lane < HEAD_DIM, x[(2 * c) * tq:(2 * c + 1) * tq],
                      x[(2 * c + 1) * tq:(2 * c + 2) * tq]) for c in range(2)]


def _fox_kernel(q_ref, k_ref, v_ref, ck_ref, o_ref, *, tq, tk):
    i = pl.program_id(1)
    start = i * tq
    rows = FOX_HEADS * tq
    qst = _stack_heads(q_ref[...], FOX_HEADS)
    qpos = start + (lax.broadcasted_iota(jnp.int32, (rows, 1), 0) & (tq - 1))
    col = lax.broadcasted_iota(jnp.int32, (1, tk), 1)
    jd = start // tk

    def scores(j, masked):
        ks0 = pl.multiple_of(j * tk, tk)
        k = k_ref[pl.ds(ks0, tk), :]
        out = []
        for h in range(FOX_HEADS):
            s = _dot_nt(qst[h * tq:(h + 1) * tq], k) - ck_ref[h:h + 1, pl.ds(ks0, tk)]
            if masked:
                s = jnp.where(ks0 + col <= qpos[:tq], s, NEG_INF)
            out.append(s)
        return out

    def values(j):
        v_pair = _pair_values(v_ref[pl.ds(pl.multiple_of(j * tk, tk), tk), :])
        return [v_pair[h // 2] for h in range(FOX_HEADS)]

    acc = jnp.concatenate(_flash_tiles(jd, jd, scores, values), axis=0)
    o_ref[...] = _unstack_pairs(acc[:, :LANES] / acc[:, LANES:], tq).astype(o_ref.dtype)


def _fox(fox, ck_t, *, batch, seq, tq, tk):
    n = fox.shape[0]
    nt = seq // tq
    width = FOX_HEADS * HEAD_DIM
    return pl.pallas_call(
        functools.partial(_fox_kernel, tq=tq, tk=tk),
        grid=(batch, nt),
        in_specs=[pl.BlockSpec((tq, width), lambda b, i: (b * nt + i, 0)),
                  pl.BlockSpec((seq, width), lambda b, i: (b, 1)),
                  pl.BlockSpec((seq, width), lambda b, i: (b, 2)),
                  pl.BlockSpec((None, 8, seq), lambda b, i: (b, 0, 0))],
        out_specs=pl.BlockSpec((tq, width), lambda b, i: (b * nt + i, 0)),
        out_shape=jax.ShapeDtypeStruct((n, width), BF16),
        compiler_params=_params("arbitrary", "arbitrary"),
        name="fox",
    )(fox, fox, fox, ck_t)


def _gelu_tanh(x):
    return 0.5 * x * (1.0 + jnp.tanh(math.sqrt(2.0 / math.pi) * (x + 0.044715 * (x * x * x))))


def _compress_kernel(xk_ref, xv_ref, w1k_ref, w2k_ref, pk_ref, w1v_ref, w2v_ref, pv_ref,
                     kc_ref, vc_ref):
    feat = CMP_STRIDE * LANES

    def run(x_ref, w1_ref, w2_ref, p_ref, o_ref):
        x = x_ref[...]
        n16 = x.shape[0]
        p = p_ref[...].astype(BF16)
        posb = (_dot(p[:, :feat], w1_ref[0, 0]) + _dot(p[:, feat:], w1_ref[0, 1]))[0:1, :]
        out = None
        for g in range(NSA_GROUPS):
            hid = _dot(x, w1_ref[g, 0]) + pltpu.roll(_dot(x, w1_ref[g, 1]), n16 - 1, 0) + posb
            a = _gelu_tanh(hid).astype(BF16)
            og = _dot(a, w2_ref[g])
            out = og if out is None else out + og
        o_ref[...] = out.astype(o_ref.dtype)

    run(xk_ref, w1k_ref, w2k_ref, pk_ref, kc_ref)
    run(xv_ref, w1v_ref, w2v_ref, pv_ref, vc_ref)


def _compress(xk, xv, w1k, w2k, pk, w1v, w2v, pv):
    batch, n16, feat = xk.shape
    xspec = pl.BlockSpec((None, n16, feat), lambda b: (b, 0, 0))
    full = lambda a: pl.BlockSpec(a.shape, lambda b: (0,) * a.ndim)
    ospec = pl.BlockSpec((None, n16, LANES), lambda b: (b, 0, 0))
    return pl.pallas_call(
        _compress_kernel,
        grid=(batch,),
        in_specs=[xspec, xspec, full(w1k), full(w2k), full(pk), full(w1v), full(w2v), full(pv)],
        out_specs=[ospec, ospec],
        out_shape=[jax.ShapeDtypeStruct((batch, n16, LANES), BF16)] * 2,
        compiler_params=_params("arbitrary"),
        name="nsa_compress",
    )(xk, xv, w1k, w2k, pk, w1v, w2v, pv)


def _nsa_kernel(q_ref, kcmp_ref, vcmp_ref, ks_ref, vs_ref, kw_ref, vw_ref, aux_ref, o_ref,
                *, tq, tk, seq, chains):
    i = pl.program_id(1)
    start = i * tq
    n_cmp = kcmp_ref.shape[0]
    wspan = NSA_WINDOW + tq

    rows = NSA_HEADS * tq
    crow = rows // chains
    n_sel = seq // SEL_BLOCK

    lane1 = lax.broadcasted_iota(jnp.int32, (1, LANES), 1)
    qpos_col = start + (lax.broadcasted_iota(jnp.int32, (rows, 1), 0) & (tq - 1))
    qpos_row = start + (lax.broadcasted_iota(jnp.int32, (1, NSA_GROUPS * tq), 1) & (tq - 1))

    kcmp = kcmp_ref[...]
    vcmp = vcmp_ref[...]
    cmp_end = lax.broadcasted_iota(jnp.int32, (1, n_cmp), 1) * CMP_STRIDE + (CMP_BLOCK - 1)
    cmp_ok = cmp_end <= qpos_col
    cs = lax.broadcasted_iota(jnp.int32, (n_cmp, LANES), 0) * CMP_STRIDE
    ss = lax.broadcasted_iota(jnp.int32, (n_cmp, LANES), 1) * SEL_BLOCK
    overlap = jnp.where((cs < ss + SEL_BLOCK) & (cs + CMP_BLOCK > ss), 1.0, 0.0).astype(F32)

    blk = lax.broadcasted_iota(jnp.int32, (LANES, NSA_GROUPS * tq), 0)
    cur = qpos_row // SEL_BLOCK
    forced = (blk == 0) | (blk == cur) | (blk == cur - 1)
    future = blk * SEL_BLOCK > qpos_row

    wstart = pl.multiple_of(jnp.maximum(start - NSA_WINDOW, 0), tq)
    kwin = kw_ref[pl.ds(wstart, wspan), :]
    vwin = vw_ref[pl.ds(wstart, wspan), :]
    kpos_w = wstart + lax.broadcasted_iota(jnp.int32, (1, wspan), 1)
    win_ok = (kpos_w <= qpos_col) & (kpos_w > qpos_col - NSA_WINDOW)

    jd = start // tk
    col_tk = lax.broadcasted_iota(jnp.int32, (1, tk), 1)
    key_blk_iota = lax.broadcasted_iota(jnp.int32, (tk, LANES), 0) // SEL_BLOCK
    key_lane = lax.broadcasted_iota(jnp.int32, (tk, LANES), 1)

    q_blocks = []
    for g in range(NSA_GROUPS):
        grp = (lane1 >= g * HEAD_DIM) & (lane1 < (g + 1) * HEAD_DIM)
        for r in range(NSA_REP):
            qc = q_ref[:, r * LANES:(r + 1) * LANES]
            q_blocks.append(jnp.where(grp, qc, jnp.zeros_like(qc)))
    qst = jnp.concatenate(q_blocks, axis=0)

    s = jnp.where(cmp_ok, _dot_nt(qst, kcmp), NEG_INF)
    m = jnp.max(s, axis=-1, keepdims=True)
    e = jnp.where(cmp_ok, jnp.exp2(s - m), 0.0)
    p = e / jnp.maximum(jnp.sum(e, axis=-1, keepdims=True), 1e-30)
    o_cmp = _dot(p.astype(BF16), vcmp)

    p_sum = []
    for g in range(NSA_GROUPS):
        blocks = [p[(g * NSA_REP + r) * tq:(g * NSA_REP + r + 1) * tq] for r in range(NSA_REP)]
        p_sum.append((blocks[0] + blocks[1]) + (blocks[2] + blocks[3]))
    imp = jnp.dot(jnp.concatenate(p_sum, axis=0), overlap, preferred_element_type=F32,
                  precision=lax.Precision.HIGHEST)
    imp_t = jnp.concatenate([imp[g * tq:(g + 1) * tq].T for g in range(NSA_GROUPS)], axis=1)
    imp_t = jnp.where(future, NEG_INF, jnp.where(forced, FORCE_SCORE, imp_t))
    val = imp_t[:n_sel, :]
    blk_s = lax.broadcasted_iota(jnp.int32, val.shape, 0)
    rank = jnp.zeros(val.shape, jnp.int32)
    vwin1 = _with_ones(vwin)
    accs = []
    for c in range(chains):
        s = jnp.where(win_ok[:crow], _dot_nt(qst[c * crow:(c + 1) * crow], kwin), NEG_INF)
        accs.append(_softmax_pv(s, vwin1)[1])
        for jp in range(c * n_sel // chains, (c + 1) * n_sel // chains):
            other = imp_t[jp:jp + 1, :]
            beats = (other > val) | ((other == val) & (blk_s > jp))
            rank = rank + jnp.where(beats, 1, 0)
    acc = jnp.concatenate(accs, axis=0)
    o_win = acc[:, :LANES] / acc[:, LANES:]

    aux = aux_ref[...]
    gate = lambda head, branch: aux[:, FOX_HEADS + 3 * head + branch:FOX_HEADS + 3 * head + branch + 1]
    partial = [gate(head, 0) * o_cmp[head * tq:(head + 1) * tq]
               + gate(head, 2) * o_win[head * tq:(head + 1) * tq] for head in range(NSA_HEADS)]

    sb_blocks = []
    for g in range(NSA_GROUPS):
        chosen = rank[:, g * tq:(g + 1) * tq] < min(SEL_TOPK, n_sel)
        sb_t = jnp.concatenate([jnp.where(chosen, 0.0, NEG_INF).astype(F32),
                                jnp.full((LANES - n_sel, tq), NEG_INF, F32)], axis=0)
        sb_blocks.extend([sb_t.T.astype(BF16)] * NSA_REP)
    qa = jnp.concatenate([qst, jnp.concatenate(sb_blocks, axis=0)], axis=1)

    def sel_scores(j, masked):
        ks0 = pl.multiple_of(j * tk, tk)
        onehot = jnp.where(key_lane == key_blk_iota + j * (tk // SEL_BLOCK), 1.0, 0.0)
        ka = jnp.concatenate([ks_ref[pl.ds(ks0, tk), :], onehot.astype(BF16)], axis=1)
        scores = []
        for c in range(chains):
            s = _dot_nt(qa[c * crow:(c + 1) * crow], ka)
            if masked:
                s = jnp.where(ks0 + col_tk <= qpos_col[:crow], s, NEG_INF)
            scores.append(s)
        return scores

    def sel_values(j):
        return [_with_ones(vs_ref[pl.ds(pl.multiple_of(j * tk, tk), tk), :])] * chains

    acc = jnp.concatenate(_flash_tiles(jd, jd, sel_scores, sel_values), axis=0)
    o_sel = acc[:, :LANES] / acc[:, LANES:]

    for r in range(NSA_REP):
        per_group = []
        for g in range(NSA_GROUPS):
            head = g * NSA_REP + r
            per_group.append(partial[head] + gate(head, 1) * o_sel[head * tq:(head + 1) * tq])
        o = jnp.where(lane1 < HEAD_DIM, per_group[0], per_group[1])
        o_ref[:, r * LANES:(r + 1) * LANES] = o.astype(o_ref.dtype)


def _nsa(nq, nkv, kcmp, vcmp, aux, *, batch, seq, tq, tk):
    n = nq.shape[0]
    nt = seq // tq
    n16 = kcmp.shape[1]
    kv = lambda c: pl.BlockSpec((seq, LANES), lambda b, i: (b, c))
    cmp_spec = pl.BlockSpec((None, n16, LANES), lambda b, i: (b, 0, 0))
    width = NSA_HEADS * HEAD_DIM
    return pl.pallas_call(
        functools.partial(_nsa_kernel, tq=tq, tk=tk, seq=seq, chains=NSA_CHAINS),
        grid=(batch, nt),
        in_specs=[pl.BlockSpec((tq, width), lambda b, i: (b * nt + i, 0)),
                  cmp_spec, cmp_spec, kv(2), kv(3), kv(4), kv(5),
                  pl.BlockSpec((tq, LANES), lambda b, i: (b * nt + i, 0))],
        out_specs=pl.BlockSpec((tq, width), lambda b, i: (b * nt + i, 0)),
        out_shape=jax.ShapeDtypeStruct((n, width), BF16),
        compiler_params=_params("arbitrary", "arbitrary"),
        name="nsa",
    )(nq, kcmp, vcmp, nkv, nkv, nkv, nkv, aux)


_DIL_UNROLL = 4


def _dilated_kernel(q1_ref, k1_ref, v1_ref, q4_ref, k4_ref, v4_ref, q16_ref, k16_ref, v16_ref,
                    o_ref, osc_ref, lsc_ref, *, tile, seq):
    i = pl.program_id(1)
    tq = DIL_SPAN
    rows = DIL_HEADS * tq
    row_u = lax.broadcasted_iota(jnp.int32, (rows, 1), 0) & (tq - 1)
    refs = ((q1_ref, k1_ref, v1_ref), (q4_ref, k4_ref, v4_ref), (q16_ref, k16_ref, v16_ref))
    for pat, (dilation, (q_ref, k_ref, v_ref)) in enumerate(zip(DIL_DILATIONS, refs)):
        length = seq // dilation
        span = min(2 * tq, length)
        per_class = tile // (tq * dilation)
        n_sub = tile // tq
        base = i * (tile // dilation)
        col_u = lax.broadcasted_iota(jnp.int32, (1, span), 1)

        def sub_tile(sub):
            c = sub // per_class
            w = sub % per_class
            u0 = pl.multiple_of(w * tq, tq)
            ks0 = pl.multiple_of(jnp.clip(base + u0 - DIL_SPAN, 0, length - span), tq)
            qst = _stack_heads(q_ref[c, pl.ds(u0, tq), :], DIL_HEADS)
            delta = (base + u0 + row_u) - (ks0 + col_u)
            s = jnp.where((delta >= 0) & (delta <= DIL_SPAN),
                          _dot_nt(qst, k_ref[c, pl.ds(ks0, span), :]), NEG_INF)
            m = jnp.max(s, axis=-1, keepdims=True)
            acc = _pair_pv(jnp.exp2((s - m).astype(BF16)),
                           _pair_values(v_ref[c, pl.ds(ks0, span), :]))
            den = acc[:, LANES:]
            dst = pl.ds(u0 * dilation + c, tq, stride=dilation)
            for half, (o, lse) in enumerate(zip(_unstack_halves(acc[:, :LANES] / den, tq),
                                                _unstack_halves(m + jnp.log2(den), tq))):
                osc_ref[pat, half, dst, :] = o
                lsc_ref[pat, half, dst, :] = lse

        def trip(it, carry):
            for k in range(_DIL_UNROLL):
                sub_tile(it * _DIL_UNROLL + k)
            return carry

        lax.fori_loop(0, n_sub // _DIL_UNROLL, trip, 0)

    chunk = 2 * tq
    for r0 in range(0, tile, chunk):
        for half in range(2):
            ls = [lsc_ref[p, half, r0:r0 + chunk, :] for p in range(3)]
            top = jnp.maximum(jnp.maximum(ls[0], ls[1]), ls[2])
            es = [jnp.exp2(l - top) for l in ls]
            num = (es[0] * osc_ref[0, half, r0:r0 + chunk, :] + es[1] * osc_ref[1, half, r0:r0 + chunk, :]
                   + es[2] * osc_ref[2, half, r0:r0 + chunk, :])
            o_ref[r0:r0 + chunk, half * LANES:(half + 1) * LANES] = (
                num / (es[0] + es[1] + es[2])).astype(o_ref.dtype)


def _dilated(dil1, dil4, dil16, *, batch, seq, tile):
    width = DIL_HEADS * HEAD_DIM
    nt = seq // tile
    specs = []
    for arr in (dil1, dil4, dil16):
        dilation, length = arr.shape[1], arr.shape[2]
        specs.append(pl.BlockSpec((None, dilation, tile // dilation, width), lambda b, i: (b, 0, i, 0)))
        for part in (1, 2):
            specs.append(pl.BlockSpec((None, dilation, length, width),
                                      lambda b, i, part=part: (b, 0, 0, part),
                                      pipeline_mode=pl.Buffered(1)))
    return pl.pallas_call(
        functools.partial(_dilated_kernel, tile=tile, seq=seq),
        grid=(batch, nt),
        in_specs=specs,
        out_specs=pl.BlockSpec((tile, width), lambda b, i: (b * nt + i, 0)),
        out_shape=jax.ShapeDtypeStruct((batch * seq, width), BF16),
        scratch_shapes=[pltpu.VMEM((3, 2, tile, LANES), F32), pltpu.VMEM((3, 2, tile, LANES), F32)],
        compiler_params=_params("arbitrary", "arbitrary"),
        name="dilated",
    )(dil1, dil1, dil1, dil4, dil4, dil4, dil16, dil16, dil16)


def _outproj_kernel(x_ref, fox_ref, nsa_ref, dil_ref, w_ref, g_ref, o_ref):
    fw = FOX_HEADS * HEAD_DIM
    nw = NSA_HEADS * HEAD_DIM
    y = (_dot(fox_ref[...], w_ref[:fw, :]) + _dot(nsa_ref[...], w_ref[fw:fw + nw, :])
         + _dot(dil_ref[...], w_ref[fw + nw:, :]))
    o_ref[...] = x_ref[...] + _rms(y, g_ref[...])


def _outproj(x, fox_o, nsa_o, dil_o, w, g, *, tm):
    n, d = x.shape
    tok = lambda width: pl.BlockSpec((tm, width), lambda i: (i, 0))
    const = lambda shape: pl.BlockSpec(shape, lambda i: (0, 0))
    return pl.pallas_call(
        _outproj_kernel,
        grid=(n // tm,),
        in_specs=[tok(d), tok(fox_o.shape[1]), tok(nsa_o.shape[1]), tok(dil_o.shape[1]),
                  const(w.shape), const((1, d))],
        out_specs=tok(d),
        out_shape=jax.ShapeDtypeStruct((n, d), F32),
        compiler_params=_params("arbitrary"),
        name="outproj",
    )(x, fox_o, nsa_o, dil_o, w, g)


def _ffn_kernel(x_ref, gpre_ref, wup_ref, cw_ref, cb_ref, wdn_ref, gpost_ref, o_ref,
                tail_ref, ubuf_ref, act_ref, *, tm, d_ff, chunk):
    @pl.when(pl.program_id(1) == 0)
    def _():
        tail_ref[...] = jnp.zeros_like(tail_ref)

    x = x_ref[...]
    h = _rms(x, gpre_ref[...]).astype(BF16)
    halo = tail_ref.shape[0]

    def conv(slot, c0):
        u = _dot(h, wup_ref[:, c0:c0 + chunk])
        ubuf_ref[slot, 0:halo, :] = tail_ref[:, c0:c0 + chunk]
        ubuf_ref[slot, halo:halo + tm, :] = u
        tail_ref[:, c0:c0 + chunk] = u[tm - halo:, :]
        u1 = ubuf_ref[slot, halo - 1:halo - 1 + tm, :]
        u2 = ubuf_ref[slot, halo - 2:halo - 2 + tm, :]
        return (cb_ref[:, c0:c0 + chunk] + u2 * cw_ref[0:1, c0:c0 + chunk]
                + u1 * cw_ref[1:2, c0:c0 + chunk] + u * cw_ref[2:3, c0:c0 + chunk])

    for c0 in range(0, d_ff, chunk):
        a = conv(0, c0)
        b = conv(1, d_ff + c0)
        act_ref[:, c0:c0 + chunk] = (a * jax.nn.sigmoid(a) * b).astype(BF16)
    y = _dot(act_ref[...], wdn_ref[...])
    o_ref[...] = x + _rms(y, gpost_ref[...])


def _ffn(x, gpre, wup, cw, cb, wdn, gpost, *, batch, seq, tm, chunk):
    n, d = x.shape
    d_ff = wdn.shape[0]
    nt = seq // tm
    tok = pl.BlockSpec((tm, d), lambda b, i: (b * nt + i, 0))
    const = lambda a: pl.BlockSpec(a.shape, lambda b, i: (0, 0), pipeline_mode=pl.Buffered(1))
    return pl.pallas_call(
        functools.partial(_ffn_kernel, tm=tm, d_ff=d_ff, chunk=chunk),
        grid=(batch, nt),
        in_specs=[tok, const(gpre), const(wup), const(cw), const(cb), const(wdn), const(gpost)],
        out_specs=tok,
        out_shape=jax.ShapeDtypeStruct((n, d), F32),
        scratch_shapes=[pltpu.VMEM((8, 2 * d_ff), F32), pltpu.VMEM((2, 8 + tm, chunk), F32),
                        pltpu.VMEM((tm, d_ff), BF16)],
        compiler_params=_params("arbitrary", "arbitrary"),
        name="conv_ffn",
    )(x, gpre, wup, cw, cb, wdn, gpost)


def _rope_tables(positions):
    inv_freq = ROPE_THETA ** (-2.0 * jnp.arange(ROPE_HALF, dtype=F32) / ROPE_DIM)
    ang = positions.astype(F32).reshape(-1, 1) * inv_freq
    cos, sin = jnp.cos(ang), jnp.sin(ang)
    n = ang.shape[0]
    ones = jnp.ones((n, HEAD_DIM - ROPE_DIM), F32)
    zeros_h = jnp.zeros((n, ROPE_HALF), F32)
    zeros_r = jnp.zeros((n, HEAD_DIM - ROPE_DIM), F32)
    rc = jnp.concatenate([cos, cos, ones], axis=1)
    rs1 = jnp.concatenate([zeros_h, sin, zeros_r], axis=1)
    rs2 = jnp.concatenate([-sin, zeros_h, zeros_r], axis=1)
    tile2 = lambda t: jnp.concatenate([t, t], axis=1)
    return tile2(rc), tile2(rs1), tile2(rs2)


def _nsa_head_perm():
    cols = []
    for r in range(NSA_REP):
        for g in range(NSA_GROUPS):
            h = g * NSA_REP + r
            cols.extend(range(h * HEAD_DIM, (h + 1) * HEAD_DIM))
    return np.asarray(cols)


def _regroup_w_in(w):
    fw, nw, kvw, dw = FOX_HEADS * HEAD_DIM, NSA_HEADS * HEAD_DIM, NSA_GROUPS * HEAD_DIM, DIL_HEADS * HEAD_DIM
    sizes = [fw, fw, fw, FOX_HEADS, nw] + [kvw] * 6 + [3 * NSA_HEADS, dw, dw, dw]
    offs = np.concatenate([[0], np.cumsum(sizes)])
    part = lambda idx: w[:, offs[idx]:offs[idx + 1]]
    fq, fk, fv, ff, nq = (part(t) for t in range(5))
    kvs = [part(t) for t in range(5, 11)]
    ng, dq, dk, dv = (part(t) for t in range(11, 15))
    nq = nq[:, _nsa_head_perm()]
    pad = jnp.zeros((w.shape[0], 2 * LANES - FOX_HEADS - 3 * NSA_HEADS), w.dtype)
    cols = [fq * Q_SCALE, fk, fv, nq * Q_SCALE] + kvs + [dq * Q_SCALE, dk, dv, ff, ng, pad]
    return jnp.concatenate(cols, axis=1).astype(BF16)


def _regroup_w_out(w):
    fw, nw = FOX_HEADS * HEAD_DIM, NSA_HEADS * HEAD_DIM
    return jnp.concatenate([w[:fw], w[fw:fw + nw][_nsa_head_perm()], w[fw + nw:]], axis=0).astype(BF16)


def _place_w1(w1):
    hidden = w1.shape[1]
    w = w1.reshape(CMP_BLOCK // CMP_STRIDE, CMP_STRIDE, HEAD_DIM, hidden)
    z = jnp.zeros_like(w)
    per_group = [jnp.concatenate([w, z], axis=2), jnp.concatenate([z, w], axis=2)]
    return jnp.stack(per_group).reshape(NSA_GROUPS, CMP_BLOCK // CMP_STRIDE, CMP_STRIDE * LANES,
                                        hidden).astype(BF16)


def _place_pos(p):
    z = jnp.zeros_like(p)
    row = jnp.concatenate([p, z], axis=1).reshape(1, -1)
    return jnp.concatenate([row, jnp.zeros((7, row.shape[1]), row.dtype)], axis=0)


def _place_w2(w2):
    z = jnp.zeros_like(w2)
    return jnp.stack([jnp.concatenate([w2, z], axis=1), jnp.concatenate([z, w2], axis=1)]).astype(BF16)


def kernel(x, positions, attn_pre_norm, attn_post_norm, ffn_pre_norm, ffn_post_norm, w_in, b_forget, b_nsa_gate, cmp_pos_k, cmp_w1_k, cmp_w2_k, cmp_pos_v, cmp_w1_v, cmp_w2_v, w_out, w_up, conv_w, conv_b, w_down):
    batch, seq, d = x.shape
    depth = w_in.shape[0]
    n = batch * seq
    tm = min(512, seq)
    assert seq % tm == 0 and seq % (16 * LANES) == 0

    rc, rs1, rs2 = _rope_tables(positions)
    xf = x.reshape(n, d)
    for l in range(depth):
        bias = jnp.concatenate([b_forget[l], b_nsa_gate[l],
                                jnp.zeros((LANES - FOX_HEADS - 3 * NSA_HEADS,), F32)]).reshape(1, LANES)
        fox, nq, nkv, dil, dil4, dil16, aux, ck_t, xk, xv = _inproj(
            xf, attn_pre_norm[l].reshape(1, d), _regroup_w_in(w_in[l]), rc, rs1, rs2, bias,
            batch=batch, seq=seq, tm=tm)
        fox_o = _fox(fox, ck_t, batch=batch, seq=seq, tq=tm // 2, tk=tm)

        kcmp, vcmp = _compress(
            xk, xv, _place_w1(cmp_w1_k[l]), _place_w2(cmp_w2_k[l]), _place_pos(cmp_pos_k[l]),
            _place_w1(cmp_w1_v[l]), _place_w2(cmp_w2_v[l]), _place_pos(cmp_pos_v[l]))
        nsa_o = _nsa(nq, nkv, kcmp, vcmp, aux, batch=batch, seq=seq, tq=LANES, tk=tm)

        dil_o = _dilated(dil.reshape(batch, 1, seq, dil.shape[1]), dil4, dil16,
                         batch=batch, seq=seq, tile=16 * LANES)

        xf = _outproj(xf, fox_o, nsa_o, dil_o, _regroup_w_out(w_out[l]),
                      attn_post_norm[l].reshape(1, d), tm=tm)
        xf = _ffn(xf, ffn_pre_norm[l].reshape(1, d), w_up[l].astype(BF16), conv_w[l],
                  conv_b[l].reshape(1, -1), w_down[l].astype(BF16), ffn_post_norm[l].reshape(1, d),
                  batch=batch, seq=seq, tm=tm, chunk=256)
    return xf.reshape(batch, seq, d)
```

```python
import functools
import math

import numpy as np
import jax
import jax.numpy as jnp
from jax import lax
from jax.experimental import pallas as pl
from jax.experimental.pallas import tpu as pltpu

HEAD_DIM = 64
FOX_HEADS = 4
NSA_HEADS = 8
NSA_GROUPS = 2
NSA_REP = NSA_HEADS // NSA_GROUPS
DIL_HEADS = 4
DIL_DILATIONS = (1, 4, 16)
DIL_SPAN = 128
ROPE_THETA = 500000.0
ROPE_DIM = HEAD_DIM // 4
ROPE_HALF = ROPE_DIM // 2
CMP_BLOCK = 32
CMP_STRIDE = 16
SEL_BLOCK = 64
SEL_TOPK = 16
NSA_WINDOW = 512
CONV_WIDTH = 3
RMS_EPS = 1e-6
NEG_INF = -1e30
FORCE_SCORE = 1e9
ATTN_SCALE = HEAD_DIM ** -0.5
LOG2E = math.log2(math.e)
Q_SCALE = ATTN_SCALE * LOG2E
LANES = 128
VMEM_LIMIT = 56 * 1024 * 1024
FLASH_UNROLL = 2
NSA_CHAINS = 4
FOX_SPLIT = 1

F32 = jnp.float32
BF16 = jnp.bfloat16


def _dot_nt(a, b):
    return lax.dot_general(a, b, (((1,), (1,)), ((), ())), preferred_element_type=F32)


def _dot(a, b):
    return jnp.dot(a, b, preferred_element_type=F32)


def _rms(x, g):
    return x * lax.rsqrt(jnp.mean(x * x, axis=-1, keepdims=True) + RMS_EPS) * g


def _params(*sem):
    return pltpu.CompilerParams(dimension_semantics=sem, vmem_limit_bytes=VMEM_LIMIT)


def _with_ones(v):
    return jnp.concatenate([v, jnp.ones((v.shape[0], LANES), v.dtype)], axis=1)


def _softmax_pv(s, v1):
    m = jnp.max(s, axis=-1, keepdims=True)
    return m, _dot(jnp.exp2((s - m).astype(BF16)), v1)


def _flash_tiles(n_before, first, score_fn, value_fn):
    state = []
    for s, v1 in zip(score_fn(first, True), value_fn(first)):
        m = jnp.max(s, axis=-1, keepdims=True)
        state.append((m, _dot(jnp.exp2((s - m).astype(BF16)), v1)))

    def step(j, state):
        out = []
        for s, v1, (m, acc) in zip(score_fn(j, False), value_fn(j), state):
            m_new = jnp.maximum(m, jnp.max(s, axis=-1, keepdims=True))
            p = jnp.exp2((s - m_new).astype(BF16))
            out.append((m_new, jnp.exp2(m - m_new) * acc + _dot(p, v1)))
        return tuple(out)

    def group(t, state):
        for u in range(FLASH_UNROLL):
            state = step(t * FLASH_UNROLL + u, state)
        return state

    n_groups = n_before // FLASH_UNROLL
    state = lax.fori_loop(0, n_groups, group, tuple(state))
    state = lax.fori_loop(n_groups * FLASH_UNROLL, n_before, step, state)
    return [acc for _, acc in state]


_ROPE_CHUNKS = {"fox": (), "nq": (0, 1, 2, 3), "nkv": (0, 2, 4), "dil": (0, 1, 2, 3)}
_SEG_WIDTH = {"fox": 768, "nq": 512, "nkv": 768, "dil": 768}
_SEG_ORDER = ("fox", "nq", "nkv", "dil")
_AUX_OFFSET = sum(_SEG_WIDTH.values())
_W_IN_COLS = _AUX_OFFSET + 2 * LANES


def _inproj_kernel(x_ref, g_ref, w_ref, rc_ref, rs1_ref, rs2_ref, bias_ref,
                   fox_ref, nq_ref, nkv_ref, dil_ref, dil4_ref, dil16_ref, aux_ref, ck_ref,
                   xk_ref, xv_ref, carry_ref, stage_ref, *, tm):
    @pl.when(pl.program_id(1) == 0)
    def _():
        carry_ref[...] = jnp.zeros_like(carry_ref)

    h = _rms(x_ref[...], g_ref[...]).astype(BF16)
    rc, rs1, rs2 = rc_ref[...], rs1_ref[...], rs2_ref[...]
    outs = {"fox": fox_ref, "nq": nq_ref, "nkv": nkv_ref, "dil": dil_ref}
    col = 0
    for name in _SEG_ORDER:
        for c0 in range(0, _SEG_WIDTH[name], 2 * LANES):
            y = _dot(h, w_ref[:, col + c0:col + c0 + 2 * LANES])
            for half in range(2):
                chunk = c0 // LANES + half
                yc = y[:, half * LANES:(half + 1) * LANES]
                if chunk in _ROPE_CHUNKS[name]:
                    yc = (yc * rc + pltpu.roll(yc, ROPE_HALF, 1) * rs1
                          + pltpu.roll(yc, LANES - ROPE_HALF, 1) * rs2)
                outs[name][:, chunk * LANES:(chunk + 1) * LANES] = yc.astype(BF16)
                if name == "nkv" and chunk < 2:
                    stage_ref[...] = yc
                    for t in range(CMP_STRIDE):
                        part = stage_ref[pl.ds(t, tm // CMP_STRIDE, stride=CMP_STRIDE), :]
                        (xk_ref, xv_ref)[chunk][:, t * LANES:(t + 1) * LANES] = part.astype(BF16)
                if name == "dil":
                    stage_ref[...] = yc
                    for dilation, ref in ((DIL_DILATIONS[1], dil4_ref), (DIL_DILATIONS[2], dil16_ref)):
                        for c in range(dilation):
                            part = stage_ref[pl.ds(c, tm // dilation, stride=dilation), :]
                            ref[c, :, chunk * LANES:(chunk + 1) * LANES] = part.astype(BF16)
        col += _SEG_WIDTH[name]

    ya = _dot(h, w_ref[:, _AUX_OFFSET:_AUX_OFFSET + 2 * LANES])[:, :LANES] + bias_ref[...]
    aux_ref[...] = jax.nn.sigmoid(ya)
    logf = jnp.minimum(ya, 0.0) - jnp.log1p(jnp.exp(-jnp.abs(ya)))
    csum = logf.T[:8, :]
    lane = lax.broadcasted_iota(jnp.int32, csum.shape, 1)
    shift = 1
    while shift < tm:
        csum = csum + jnp.where(lane >= shift, pltpu.roll(csum, shift, 1), 0.0)
        shift *= 2
    csum = csum + jnp.concatenate([carry_ref[...]] * (tm // LANES), axis=1)
    carry_ref[...] = jnp.broadcast_to(csum[:, tm - 1:tm], carry_ref.shape)
    ck_ref[...] = csum * LOG2E


def _inproj(x, g, w, rc, rs1, rs2, bias, *, batch, seq, tm):
    n, d = x.shape
    nt = seq // tm
    tok = lambda width: pl.BlockSpec((tm, width), lambda b, i: (b * nt + i, 0))
    const = lambda shape: pl.BlockSpec(shape, lambda b, i: (0, 0))
    cls = lambda dil: pl.BlockSpec((None, dil, tm // dil, 768), lambda b, i: (b, 0, i, 0))
    cls_shape = lambda dil: jax.ShapeDtypeStruct((batch, dil, seq // dil, 768), BF16)
    d4, d16 = DIL_DILATIONS[1], DIL_DILATIONS[2]
    blk16 = pl.BlockSpec((None, tm // CMP_STRIDE, CMP_STRIDE * LANES), lambda b, i: (b, i, 0))
    blk16_shape = jax.ShapeDtypeStruct((batch, seq // CMP_STRIDE, CMP_STRIDE * LANES), BF16)
    return pl.pallas_call(
        functools.partial(_inproj_kernel, tm=tm),
        grid=(batch, nt),
        in_specs=[tok(d), const((1, d)), const(w.shape), tok(LANES), tok(LANES), tok(LANES),
                  const((1, LANES))],
        out_specs=[tok(768), tok(512), tok(768), tok(768), cls(d4), cls(d16), tok(LANES),
                   pl.BlockSpec((None, 8, tm), lambda b, i: (b, 0, i)), blk16, blk16],
        out_shape=[jax.ShapeDtypeStruct((n, 768), BF16), jax.ShapeDtypeStruct((n, 512), BF16),
                   jax.ShapeDtypeStruct((n, 768), BF16), jax.ShapeDtypeStruct((n, 768), BF16),
                   cls_shape(d4), cls_shape(d16), jax.ShapeDtypeStruct((n, LANES), F32),
                   jax.ShapeDtypeStruct((batch, 8, seq), F32), blk16_shape, blk16_shape],
        scratch_shapes=[pltpu.VMEM((8, LANES), F32), pltpu.VMEM((tm, LANES), F32)],
        compiler_params=_params("arbitrary", "arbitrary"),
        name="inproj",
    )(x, g, w, rc, rs1, rs2, bias)


def _stack_heads(q, heads):
    lane = lax.broadcasted_iota(jnp.int32, (1, q.shape[1]), 1)
    blocks = []
    for h in range(heads):
        head = (lane >= h * HEAD_DIM) & (lane < (h + 1) * HEAD_DIM)
        blocks.append(jnp.where(head, q, jnp.zeros_like(q)))
    return jnp.concatenate(blocks, axis=0)


def _pair_values(v):
    return _with_ones(v[:, :LANES]), _with_ones(v[:, LANES:])


def _pair_pv(p, v_pair):
    half = p.shape[0] // 2
    return jnp.concatenate([_dot(p[:half], v_pair[0]), _dot(p[half:], v_pair[1])], axis=0)


def _unstack_pairs(x, tq):
    return jnp.concatenate(_unstack_halves(x, tq), axis=1)


def _unstack_halves(x, tq):
    lane = lax.broadcasted_iota(jnp.int32, (1, LANES), 1)
    return [jnp.where(lane < HEAD_DIM, x[(2 * c) * tq:(2 * c + 1) * tq],
                      x[(2 * c + 1) * tq:(2 * c + 2) * tq]) for c in range(2)]


def _fox_kernel(q_ref, k_ref, v_ref, ck_ref, o_ref, *, tq, tk):
    i = pl.program_id(1)
    start = i * tq
    rows = FOX_HEADS * tq
    qst = _stack_heads(q_ref[...], FOX_HEADS)
    qpos = start + (lax.broadcasted_iota(jnp.int32, (rows, 1), 0) & (tq - 1))
    col = lax.broadcasted_iota(jnp.int32, (1, tk), 1)
    jd = start // tk

    crow = tq // FOX_SPLIT

    def scores(j, masked):
        ks0 = pl.multiple_of(j * tk, tk)
        k = k_ref[pl.ds(ks0, tk), :]
        out = []
        for h in range(FOX_HEADS):
            for r0 in range(h * tq, (h + 1) * tq, crow):
                s = _dot_nt(qst[r0:r0 + crow], k) - ck_ref[h:h + 1, pl.ds(ks0, tk)]
                if masked:
                    s = jnp.where(ks0 + col <= qpos[r0:r0 + crow], s, NEG_INF)
                out.append(s)
        return out

    def values(j):
        v_pair = _pair_values(v_ref[pl.ds(pl.multiple_of(j * tk, tk), tk), :])
        return [v_pair[h // 2] for h in range(FOX_HEADS) for _ in range(FOX_SPLIT)]

    acc = jnp.concatenate(_flash_tiles(jd, jd, scores, values), axis=0)
    o_ref[...] = _unstack_pairs(acc[:, :LANES] / acc[:, LANES:], tq).astype(o_ref.dtype)


def _fox(fox, ck_t, *, batch, seq, tq, tk):
    n = fox.shape[0]
    nt = seq // tq
    width = FOX_HEADS * HEAD_DIM
    return pl.pallas_call(
        functools.partial(_fox_kernel, tq=tq, tk=tk),
        grid=(batch, nt),
        in_specs=[pl.BlockSpec((tq, width), lambda b, i: (b * nt + i, 0)),
                  pl.BlockSpec((seq, width), lambda b, i: (b, 1)),
                  pl.BlockSpec((seq, width), lambda b, i: (b, 2)),
                  pl.BlockSpec((None, 8, seq), lambda b, i: (b, 0, 0))],
        out_specs=pl.BlockSpec((tq, width), lambda b, i: (b * nt + i, 0)),
        out_shape=jax.ShapeDtypeStruct((n, width), BF16),
        compiler_params=_params("arbitrary", "arbitrary"),
        name="fox",
    )(fox, fox, fox, ck_t)


def _gelu_tanh(x):
    return 0.5 * x * (1.0 + jnp.tanh(math.sqrt(2.0 / math.pi) * (x + 0.044715 * (x * x * x))))


def _compress_kernel(xk_ref, xv_ref, w1k_ref, w2k_ref, pk_ref, w1v_ref, w2v_ref, pv_ref,
                     kc_ref, vc_ref):
    feat = CMP_STRIDE * LANES

    def run(x_ref, w1_ref, w2_ref, p_ref, o_ref):
        x = x_ref[...]
        n16 = x.shape[0]
        p = p_ref[...].astype(BF16)
        posb = (_dot(p[:, :feat], w1_ref[0, 0]) + _dot(p[:, feat:], w1_ref[0, 1]))[0:1, :]
        out = None
        for g in range(NSA_GROUPS):
            hid = _dot(x, w1_ref[g, 0]) + pltpu.roll(_dot(x, w1_ref[g, 1]), n16 - 1, 0) + posb
            a = _gelu_tanh(hid).astype(BF16)
            og = _dot(a, w2_ref[g])
            out = og if out is None else out + og
        o_ref[...] = out.astype(o_ref.dtype)

    run(xk_ref, w1k_ref, w2k_ref, pk_ref, kc_ref)
    run(xv_ref, w1v_ref, w2v_ref, pv_ref, vc_ref)


def _compress(xk, xv, w1k, w2k, pk, w1v, w2v, pv):
    batch, n16, feat = xk.shape
    xspec = pl.BlockSpec((None, n16, feat), lambda b: (b, 0, 0))
    full = lambda a: pl.BlockSpec(a.shape, lambda b: (0,) * a.ndim)
    ospec = pl.BlockSpec((None, n16, LANES), lambda b: (b, 0, 0))
    return pl.pallas_call(
        _compress_kernel,
        grid=(batch,),
        in_specs=[xspec, xspec, full(w1k), full(w2k), full(pk), full(w1v), full(w2v), full(pv)],
        out_specs=[ospec, ospec],
        out_shape=[jax.ShapeDtypeStruct((batch, n16, LANES), BF16)] * 2,
        compiler_params=_params("arbitrary"),
        name="nsa_compress",
    )(xk, xv, w1k, w2k, pk, w1v, w2v, pv)


def _nsa_kernel(q_ref, kcmp_ref, vcmp_ref, ks_ref, vs_ref, kw_ref, vw_ref, aux_ref, o_ref,
                *, tq, tk, seq, chains):
    i = pl.program_id(1)
    start = i * tq
    n_cmp = kcmp_ref.shape[0]
    wspan = NSA_WINDOW + tq

    rows = NSA_HEADS * tq
    crow = rows // chains
    n_sel = seq // SEL_BLOCK

    lane1 = lax.broadcasted_iota(jnp.int32, (1, LANES), 1)
    qpos_col = start + (lax.broadcasted_iota(jnp.int32, (rows, 1), 0) & (tq - 1))
    qpos_row = start + (lax.broadcasted_iota(jnp.int32, (1, NSA_GROUPS * tq), 1) & (tq - 1))

    kcmp = kcmp_ref[...]
    vcmp = vcmp_ref[...]
    cmp_end = lax.broadcasted_iota(jnp.int32, (1, n_cmp), 1) * CMP_STRIDE + (CMP_BLOCK - 1)
    cmp_ok = cmp_end <= qpos_col
    cs = lax.broadcasted_iota(jnp.int32, (n_cmp, LANES), 0) * CMP_STRIDE
    ss = lax.broadcasted_iota(jnp.int32, (n_cmp, LANES), 1) * SEL_BLOCK
    overlap = jnp.where((cs < ss + SEL_BLOCK) & (cs + CMP_BLOCK > ss), 1.0, 0.0).astype(F32)

    blk = lax.broadcasted_iota(jnp.int32, (LANES, NSA_GROUPS * tq), 0)
    cur = qpos_row // SEL_BLOCK
    forced = (blk == 0) | (blk == cur) | (blk == cur - 1)
    future = blk * SEL_BLOCK > qpos_row

    wstart = pl.multiple_of(jnp.maximum(start - NSA_WINDOW, 0), tq)
    kwin = kw_ref[pl.ds(wstart, wspan), :]
    vwin = vw_ref[pl.ds(wstart, wspan), :]
    kpos_w = wstart + lax.broadcasted_iota(jnp.int32, (1, wspan), 1)
    win_ok = (kpos_w <= qpos_col) & (kpos_w > qpos_col - NSA_WINDOW)

    jd = start // tk
    col_tk = lax.broadcasted_iota(jnp.int32, (1, tk), 1)
    key_blk_iota = lax.broadcasted_iota(jnp.int32, (tk, LANES), 0) // SEL_BLOCK
    key_lane = lax.broadcasted_iota(jnp.int32, (tk, LANES), 1)

    q_blocks = []
    for g in range(NSA_GROUPS):
        grp = (lane1 >= g * HEAD_DIM) & (lane1 < (g + 1) * HEAD_DIM)
        for r in range(NSA_REP):
            qc = q_ref[:, r * LANES:(r + 1) * LANES]
            q_blocks.append(jnp.where(grp, qc, jnp.zeros_like(qc)))
    qst = jnp.concatenate(q_blocks, axis=0)

    s = jnp.where(cmp_ok, _dot_nt(qst, kcmp), NEG_INF)
    m = jnp.max(s, axis=-1, keepdims=True)
    e = jnp.where(cmp_ok, jnp.exp2(s - m), 0.0)
    p = e / jnp.maximum(jnp.sum(e, axis=-1, keepdims=True), 1e-30)
    o_cmp = _dot(p.astype(BF16), vcmp)

    p_sum = []
    for g in range(NSA_GROUPS):
        blocks = [p[(g * NSA_REP + r) * tq:(g * NSA_REP + r + 1) * tq] for r in range(NSA_REP)]
        p_sum.append((blocks[0] + blocks[1]) + (blocks[2] + blocks[3]))
    imp = jnp.dot(jnp.concatenate(p_sum, axis=0), overlap, preferred_element_type=F32,
                  precision=lax.Precision.HIGHEST)
    imp_t = jnp.concatenate([imp[g * tq:(g + 1) * tq].T for g in range(NSA_GROUPS)], axis=1)
    imp_t = jnp.where(future, NEG_INF, jnp.where(forced, FORCE_SCORE, imp_t))
    val = imp_t[:n_sel, :]
    blk_s = lax.broadcasted_iota(jnp.int32, val.shape, 0)
    rank = jnp.zeros(val.shape, jnp.int32)
    vwin1 = _with_ones(vwin)
    accs = []
    for c in range(chains):
        s = jnp.where(win_ok[:crow], _dot_nt(qst[c * crow:(c + 1) * crow], kwin), NEG_INF)
        accs.append(_softmax_pv(s, vwin1)[1])
        for jp in range(c * n_sel // chains, (c + 1) * n_sel // chains):
            other = imp_t[jp:jp + 1, :]
            beats = (other > val) | ((other == val) & (blk_s > jp))
            rank = rank + jnp.where(beats, 1, 0)
    acc = jnp.concatenate(accs, axis=0)
    o_win = acc[:, :LANES] / acc[:, LANES:]

    aux = aux_ref[...]
    gate = lambda head, branch: aux[:, FOX_HEADS + 3 * head + branch:FOX_HEADS + 3 * head + branch + 1]
    partial = [gate(head, 0) * o_cmp[head * tq:(head + 1) * tq]
               + gate(head, 2) * o_win[head * tq:(head + 1) * tq] for head in range(NSA_HEADS)]

    sb_blocks = []
    for g in range(NSA_GROUPS):
        chosen = rank[:, g * tq:(g + 1) * tq] < min(SEL_TOPK, n_sel)
        sb_t = jnp.concatenate([jnp.where(chosen, 0.0, NEG_INF).astype(F32),
                                jnp.full((LANES - n_sel, tq), NEG_INF, F32)], axis=0)
        sb_blocks.extend([sb_t.T.astype(BF16)] * NSA_REP)
    qa = jnp.concatenate([qst, jnp.concatenate(sb_blocks, axis=0)], axis=1)

    def sel_scores(j, masked):
        ks0 = pl.multiple_of(j * tk, tk)
        onehot = jnp.where(key_lane == key_blk_iota + j * (tk // SEL_BLOCK), 1.0, 0.0)
        ka = jnp.concatenate([ks_ref[pl.ds(ks0, tk), :], onehot.astype(BF16)], axis=1)
        scores = []
        for c in range(chains):
            s = _dot_nt(qa[c * crow:(c + 1) * crow], ka)
            if masked:
                s = jnp.where(ks0 + col_tk <= qpos_col[:crow], s, NEG_INF)
            scores.append(s)
        return scores

    def sel_values(j):
        return [_with_ones(vs_ref[pl.ds(pl.multiple_of(j * tk, tk), tk), :])] * chains

    acc = jnp.concatenate(_flash_tiles(jd, jd, sel_scores, sel_values), axis=0)
    o_sel = acc[:, :LANES] / acc[:, LANES:]

    for r in range(NSA_REP):
        per_group = []
        for g in range(NSA_GROUPS):
            head = g * NSA_REP + r
            per_group.append(partial[head] + gate(head, 1) * o_sel[head * tq:(head + 1) * tq])
        o = jnp.where(lane1 < HEAD_DIM, per_group[0], per_group[1])
        o_ref[:, r * LANES:(r + 1) * LANES] = o.astype(o_ref.dtype)


def _nsa(nq, nkv, kcmp, vcmp, aux, *, batch, seq, tq, tk):
    n = nq.shape[0]
    nt = seq // tq
    n16 = kcmp.shape[1]
    kv = lambda c: pl.BlockSpec((seq, LANES), lambda b, i: (b, c))
    cmp_spec = pl.BlockSpec((None, n16, LANES), lambda b, i: (b, 0, 0))
    width = NSA_HEADS * HEAD_DIM
    return pl.pallas_call(
        functools.partial(_nsa_kernel, tq=tq, tk=tk, seq=seq, chains=NSA_CHAINS),
        grid=(batch, nt),
        in_specs=[pl.BlockSpec((tq, width), lambda b, i: (b * nt + i, 0)),
                  cmp_spec, cmp_spec, kv(2), kv(3), kv(4), kv(5),
                  pl.BlockSpec((tq, LANES), lambda b, i: (b * nt + i, 0))],
        out_specs=pl.BlockSpec((tq, width), lambda b, i: (b * nt + i, 0)),
        out_shape=jax.ShapeDtypeStruct((n, width), BF16),
        compiler_params=_params("arbitrary", "arbitrary"),
        name="nsa",
    )(nq, kcmp, vcmp, nkv, nkv, nkv, nkv, aux)


_DIL_UNROLL = 16


def _dilated_kernel(q1_ref, k1_ref, v1_ref, q4_ref, k4_ref, v4_ref, q16_ref, k16_ref, v16_ref,
                    o_ref, osc_ref, lsc_ref, *, tile, seq):
    i = pl.program_id(1)
    tq = DIL_SPAN
    rows = DIL_HEADS * tq
    row_u = lax.broadcasted_iota(jnp.int32, (rows, 1), 0) & (tq - 1)
    refs = ((q1_ref, k1_ref, v1_ref), (q4_ref, k4_ref, v4_ref), (q16_ref, k16_ref, v16_ref))
    for pat, (dilation, (q_ref, k_ref, v_ref)) in enumerate(zip(DIL_DILATIONS, refs)):
        length = seq // dilation
        span = min(2 * tq, length)
        per_class = tile // (tq * dilation)
        n_sub = tile // tq
        base = i * (tile // dilation)
        rel = row_u - lax.broadcasted_iota(jnp.int32, (1, span), 1)

        def band(offset):
            return jnp.where((rel + offset >= 0) & (rel + offset <= DIL_SPAN), 0.0, NEG_INF).astype(F32)

        bias_shifted, bias_aligned = band(min(DIL_SPAN, length - span + DIL_SPAN)), band(0)

        def sub_tile(sub):
            c = sub // per_class
            w = sub % per_class
            u0 = pl.multiple_of(w * tq, tq)
            ks0 = pl.multiple_of(jnp.maximum(base + u0 - DIL_SPAN, 0), tq)
            qst = _stack_heads(q_ref[c, pl.ds(u0, tq), :], DIL_HEADS)
            bias = jnp.where(base + u0 == 0, bias_aligned, bias_shifted)
            s = _dot_nt(qst, k_ref[c, pl.ds(ks0, span), :]) + bias
            m = jnp.max(s, axis=-1, keepdims=True)
            acc = _pair_pv(jnp.exp2((s - m).astype(BF16)),
                           _pair_values(v_ref[c, pl.ds(ks0, span), :]))
            den = acc[:, LANES:]
            dst = pl.ds(u0 * dilation + c, tq, stride=dilation)
            for half, (o, lse) in enumerate(zip(_unstack_halves(acc[:, :LANES] / den, tq),
                                                _unstack_halves(m + jnp.log2(den), tq))):
                osc_ref[pat, half, dst, :] = o
                lsc_ref[pat, half, dst, :] = lse

        def trip(it, carry):
            for k in range(_DIL_UNROLL):
                sub_tile(it * _DIL_UNROLL + k)
            return carry

        lax.fori_loop(0, n_sub // _DIL_UNROLL, trip, 0)

    chunk = 2 * tq
    for r0 in range(0, tile, chunk):
        for half in range(2):
            ls = [lsc_ref[p, half, r0:r0 + chunk, :] for p in range(3)]
            top = jnp.maximum(jnp.maximum(ls[0], ls[1]), ls[2])
            es = [jnp.exp2(l - top) for l in ls]
            num = (es[0] * osc_ref[0, half, r0:r0 + chunk, :] + es[1] * osc_ref[1, half, r0:r0 + chunk, :]
                   + es[2] * osc_ref[2, half, r0:r0 + chunk, :])
            o_ref[r0:r0 + chunk, half * LANES:(half + 1) * LANES] = (
                num / (es[0] + es[1] + es[2])).astype(o_ref.dtype)


def _dilated(dil1, dil4, dil16, *, batch, seq, tile):
    width = DIL_HEADS * HEAD_DIM
    nt = seq // tile
    specs = []
    for arr in (dil1, dil4, dil16):
        dilation, length = arr.shape[1], arr.shape[2]
        specs.append(pl.BlockSpec((None, dilation, tile // dilation, width), lambda b, i: (b, 0, i, 0)))
        for part in (1, 2):
            specs.append(pl.BlockSpec((None, dilation, length, width),
                                      lambda b, i, part=part: (b, 0, 0, part),
                                      pipeline_mode=pl.Buffered(1)))
    return pl.pallas_call(
        functools.partial(_dilated_kernel, tile=tile, seq=seq),
        grid=(batch, nt),
        in_specs=specs,
        out_specs=pl.BlockSpec((tile, width), lambda b, i: (b * nt + i, 0)),
        out_shape=jax.ShapeDtypeStruct((batch * seq, width), BF16),
        scratch_shapes=[pltpu.VMEM((3, 2, tile, LANES), F32), pltpu.VMEM((3, 2, tile, LANES), F32)],
        compiler_params=_params("arbitrary", "arbitrary"),
        name="dilated",
    )(dil1, dil1, dil1, dil4, dil4, dil4, dil16, dil16, dil16)


def _outproj_kernel(x_ref, fox_ref, nsa_ref, dil_ref, w_ref, g_ref, o_ref):
    fw = FOX_HEADS * HEAD_DIM
    nw = NSA_HEADS * HEAD_DIM
    y = (_dot(fox_ref[...], w_ref[:fw, :]) + _dot(nsa_ref[...], w_ref[fw:fw + nw, :])
         + _dot(dil_ref[...], w_ref[fw + nw:, :]))
    o_ref[...] = x_ref[...] + _rms(y, g_ref[...])


def _outproj(x, fox_o, nsa_o, dil_o, w, g, *, tm):
    n, d = x.shape
    tok = lambda width: pl.BlockSpec((tm, width), lambda i: (i, 0))
    const = lambda shape: pl.BlockSpec(shape, lambda i: (0, 0))
    return pl.pallas_call(
        _outproj_kernel,
        grid=(n // tm,),
        in_specs=[tok(d), tok(fox_o.shape[1]), tok(nsa_o.shape[1]), tok(dil_o.shape[1]),
                  const(w.shape), const((1, d))],
        out_specs=tok(d),
        out_shape=jax.ShapeDtypeStruct((n, d), F32),
        compiler_params=_params("arbitrary"),
        name="outproj",
    )(x, fox_o, nsa_o, dil_o, w, g)


def _ffn_kernel(x_ref, gpre_ref, wup_ref, cw_ref, cb_ref, wdn_ref, gpost_ref, o_ref,
                tail_ref, ubuf_ref, act_ref, *, tm, d_ff, chunk):
    @pl.when(pl.program_id(1) == 0)
    def _():
        tail_ref[...] = jnp.zeros_like(tail_ref)

    x = x_ref[...]
    h = _rms(x, gpre_ref[...]).astype(BF16)
    halo = tail_ref.shape[0]

    def conv(slot, c0):
        u = _dot(h, wup_ref[:, c0:c0 + chunk])
        ubuf_ref[slot, 0:halo, :] = tail_ref[:, c0:c0 + chunk]
        ubuf_ref[slot, halo:halo + tm, :] = u
        tail_ref[:, c0:c0 + chunk] = u[tm - halo:, :]
        u1 = ubuf_ref[slot, halo - 1:halo - 1 + tm, :]
        u2 = ubuf_ref[slot, halo - 2:halo - 2 + tm, :]
        return (cb_ref[:, c0:c0 + chunk] + u2 * cw_ref[0:1, c0:c0 + chunk]
                + u1 * cw_ref[1:2, c0:c0 + chunk] + u * cw_ref[2:3, c0:c0 + chunk])

    for c0 in range(0, d_ff, chunk):
        a = conv(0, c0)
        b = conv(1, d_ff + c0)
        act_ref[:, c0:c0 + chunk] = (a * jax.nn.sigmoid(a) * b).astype(BF16)
    y = _dot(act_ref[...], wdn_ref[...])
    o_ref[...] = x + _rms(y, gpost_ref[...])


def _ffn(x, gpre, wup, cw, cb, wdn, gpost, *, batch, seq, tm, chunk):
    n, d = x.shape
    d_ff = wdn.shape[0]
    nt = seq // tm
    tok = pl.BlockSpec((tm, d), lambda b, i: (b * nt + i, 0))
    const = lambda a: pl.BlockSpec(a.shape, lambda b, i: (0, 0), pipeline_mode=pl.Buffered(1))
    return pl.pallas_call(
        functools.partial(_ffn_kernel, tm=tm, d_ff=d_ff, chunk=chunk),
        grid=(batch, nt),
        in_specs=[tok, const(gpre), const(wup), const(cw), const(cb), const(wdn), const(gpost)],
        out_specs=tok,
        out_shape=jax.ShapeDtypeStruct((n, d), F32),
        scratch_shapes=[pltpu.VMEM((8, 2 * d_ff), F32), pltpu.VMEM((2, 8 + tm, chunk), F32),
                        pltpu.VMEM((tm, d_ff), BF16)],
        compiler_params=_params("arbitrary", "arbitrary"),
        name="conv_ffn",
    )(x, gpre, wup, cw, cb, wdn, gpost)


def _rope_tables(positions):
    inv_freq = ROPE_THETA ** (-2.0 * jnp.arange(ROPE_HALF, dtype=F32) / ROPE_DIM)
    ang = positions.astype(F32).reshape(-1, 1) * inv_freq
    cos, sin = jnp.cos(ang), jnp.sin(ang)
    n = ang.shape[0]
    ones = jnp.ones((n, HEAD_DIM - ROPE_DIM), F32)
    zeros_h = jnp.zeros((n, ROPE_HALF), F32)
    zeros_r = jnp.zeros((n, HEAD_DIM - ROPE_DIM), F32)
    rc = jnp.concatenate([cos, cos, ones], axis=1)
    rs1 = jnp.concatenate([zeros_h, sin, zeros_r], axis=1)
    rs2 = jnp.concatenate([-sin, zeros_h, zeros_r], axis=1)
    tile2 = lambda t: jnp.concatenate([t, t], axis=1)
    return tile2(rc), tile2(rs1), tile2(rs2)


def _nsa_head_perm():
    cols = []
    for r in range(NSA_REP):
        for g in range(NSA_GROUPS):
            h = g * NSA_REP + r
            cols.extend(range(h * HEAD_DIM, (h + 1) * HEAD_DIM))
    return np.asarray(cols)


def _regroup_w_in(w):
    fw, nw, kvw, dw = FOX_HEADS * HEAD_DIM, NSA_HEADS * HEAD_DIM, NSA_GROUPS * HEAD_DIM, DIL_HEADS * HEAD_DIM
    sizes = [fw, fw, fw, FOX_HEADS, nw] + [kvw] * 6 + [3 * NSA_HEADS, dw, dw, dw]
    offs = np.concatenate([[0], np.cumsum(sizes)])
    part = lambda idx: w[:, offs[idx]:offs[idx + 1]]
    fq, fk, fv, ff, nq = (part(t) for t in range(5))
    kvs = [part(t) for t in range(5, 11)]
    ng, dq, dk, dv = (part(t) for t in range(11, 15))
    nq = nq[:, _nsa_head_perm()]
    pad = jnp.zeros((w.shape[0], 2 * LANES - FOX_HEADS - 3 * NSA_HEADS), w.dtype)
    cols = [fq * Q_SCALE, fk, fv, nq * Q_SCALE] + kvs + [dq * Q_SCALE, dk, dv, ff, ng, pad]
    return jnp.concatenate(cols, axis=1).astype(BF16)


def _regroup_w_out(w):
    fw, nw = FOX_HEADS * HEAD_DIM, NSA_HEADS * HEAD_DIM
    return jnp.concatenate([w[:fw], w[fw:fw + nw][_nsa_head_perm()], w[fw + nw:]], axis=0).astype(BF16)


def _place_w1(w1):
    hidden = w1.shape[1]
    w = w1.reshape(CMP_BLOCK // CMP_STRIDE, CMP_STRIDE, HEAD_DIM, hidden)
    z = jnp.zeros_like(w)
    per_group = [jnp.concatenate([w, z], axis=2), jnp.concatenate([z, w], axis=2)]
    return jnp.stack(per_group).reshape(NSA_GROUPS, CMP_BLOCK // CMP_STRIDE, CMP_STRIDE * LANES,
                                        hidden).astype(BF16)


def _place_pos(p):
    z = jnp.zeros_like(p)
    row = jnp.concatenate([p, z], axis=1).reshape(1, -1)
    return jnp.concatenate([row, jnp.zeros((7, row.shape[1]), row.dtype)], axis=0)


def _place_w2(w2):
    z = jnp.zeros_like(w2)
    return jnp.stack([jnp.concatenate([w2, z], axis=1), jnp.concatenate([z, w2], axis=1)]).astype(BF16)


def kernel(x, positions, attn_pre_norm, attn_post_norm, ffn_pre_norm, ffn_post_norm, w_in, b_forget, b_nsa_gate, cmp_pos_k, cmp_w1_k, cmp_w2_k, cmp_pos_v, cmp_w1_v, cmp_w2_v, w_out, w_up, conv_w, conv_b, w_down):
    batch, seq, d = x.shape
    depth = w_in.shape[0]
    n = batch * seq
    tm = min(512, seq)
    assert seq % tm == 0 and seq % (16 * LANES) == 0

    rc, rs1, rs2 = _rope_tables(positions)
    xf = x.reshape(n, d)
    for l in range(depth):
        bias = jnp.concatenate([b_forget[l], b_nsa_gate[l],
                                jnp.zeros((LANES - FOX_HEADS - 3 * NSA_HEADS,), F32)]).reshape(1, LANES)
        fox, nq, nkv, dil, dil4, dil16, aux, ck_t, xk, xv = _inproj(
            xf, attn_pre_norm[l].reshape(1, d), _regroup_w_in(w_in[l]), rc, rs1, rs2, bias,
            batch=batch, seq=seq, tm=tm)
        fox_o = _fox(fox, ck_t, batch=batch, seq=seq, tq=tm, tk=tm)

        kcmp, vcmp = _compress(
            xk, xv, _place_w1(cmp_w1_k[l]), _place_w2(cmp_w2_k[l]), _place_pos(cmp_pos_k[l]),
            _place_w1(cmp_w1_v[l]), _place_w2(cmp_w2_v[l]), _place_pos(cmp_pos_v[l]))
        nsa_o = _nsa(nq, nkv, kcmp, vcmp, aux, batch=batch, seq=seq, tq=2 * LANES, tk=tm)

        dil_o = _dilated(dil.reshape(batch, 1, seq, dil.shape[1]), dil4, dil16,
                         batch=batch, seq=seq, tile=16 * LANES)

        xf = _outproj(xf, fox_o, nsa_o, dil_o, _regroup_w_out(w_out[l]),
                      attn_post_norm[l].reshape(1, d), tm=tm)
        xf = _ffn(xf, ffn_pre_norm[l].reshape(1, d), w_up[l].astype(BF16), conv_w[l],
                  conv_b[l].reshape(1, -1), w_down[l].astype(BF16), ffn_post_norm[l].reshape(1, d),
                  batch=batch, seq=seq, tm=tm, chunk=256)
    return xf.reshape(batch, seq, d)
```

```python
import functools
import math

import numpy as np
import jax
import jax.numpy as jnp
from jax import lax
from jax.experimental import pallas as pl
from jax.experimental.pallas import tpu as pltpu

HEAD_DIM = 64
FOX_HEADS = 4
NSA_HEADS = 8
NSA_GROUPS = 2
NSA_REP = NSA_HEADS // NSA_GROUPS
DIL_HEADS = 4
DIL_DILATIONS = (1, 4, 16)
DIL_SPAN = 128
ROPE_THETA = 500000.0
ROPE_DIM = HEAD_DIM // 4
ROPE_HALF = ROPE_DIM // 2
CMP_BLOCK = 32
CMP_STRIDE = 16
SEL_BLOCK = 64
SEL_TOPK = 16
NSA_WINDOW = 512
CONV_WIDTH = 3
RMS_EPS = 1e-6
NEG_INF = -1e30
FORCE_SCORE = 1e9
ATTN_SCALE = HEAD_DIM ** -0.5
LOG2E = math.log2(math.e)
Q_SCALE = ATTN_SCALE * LOG2E
LANES = 128
VMEM_LIMIT = 56 * 1024 * 1024
FLASH_UNROLL = 2
NSA_CHAINS = 4
FOX_SPLIT = 1

F32 = jnp.float32
BF16 = jnp.bfloat16


def _dot_nt(a, b):
    return lax.dot_general(a, b, (((1,), (1,)), ((), ())), preferred_element_type=F32)


def _dot(a, b):
    return jnp.dot(a, b, preferred_element_type=F32)


def _rms(x, g):
    return x * lax.rsqrt(jnp.mean(x * x, axis=-1, keepdims=True) + RMS_EPS) * g


def _params(*sem):
    return pltpu.CompilerParams(dimension_semantics=sem, vmem_limit_bytes=VMEM_LIMIT)


def _with_ones(v):
    return jnp.concatenate([v, jnp.ones((v.shape[0], LANES), v.dtype)], axis=1)


def _softmax_pv(s, v1):
    m = jnp.max(s, axis=-1, keepdims=True)
    return m, _dot(jnp.exp2((s - m).astype(BF16)), v1)


def _flash_tiles(n_before, first, score_fn, value_fn):
    state = []
    for s, v1 in zip(score_fn(first, True), value_fn(first)):
        m = jnp.max(s, axis=-1, keepdims=True)
        state.append((m, _dot(jnp.exp2((s - m).astype(BF16)), v1)))

    def step(j, state):
        out = []
        for s, v1, (m, acc) in zip(score_fn(j, False), value_fn(j), state):
            m_new = jnp.maximum(m, jnp.max(s, axis=-1, keepdims=True))
            p = jnp.exp2((s - m_new).astype(BF16))
            out.append((m_new, jnp.exp2(m - m_new) * acc + _dot(p, v1)))
        return tuple(out)

    def group(t, state):
        for u in range(FLASH_UNROLL):
            state = step(t * FLASH_UNROLL + u, state)
        return state

    n_groups = n_before // FLASH_UNROLL
    state = lax.fori_loop(0, n_groups, group, tuple(state))
    state = lax.fori_loop(n_groups * FLASH_UNROLL, n_before, step, state)
    return [acc for _, acc in state]


_ROPE_CHUNKS = {"fox": (), "nq": (0, 1, 2, 3), "nkv": (0, 2, 4), "dil": (0, 1, 2, 3)}
_SEG_WIDTH = {"fox": 768, "nq": 512, "nkv": 768, "dil": 768}
_SEG_ORDER = ("fox", "nq", "nkv", "dil")
_AUX_OFFSET = sum(_SEG_WIDTH.values())
_W_IN_COLS = _AUX_OFFSET + 2 * LANES


def _inproj_kernel(x_ref, g_ref, w_ref, rc_ref, rs1_ref, rs2_ref, bias_ref,
                   fox_ref, nq_ref, nkv_ref, dil_ref, dil4_ref, dil16_ref, aux_ref, ck_ref,
                   xk_ref, xv_ref, carry_ref, stage_ref, *, tm):
    @pl.when(pl.program_id(1) == 0)
    def _():
        carry_ref[...] = jnp.zeros_like(carry_ref)

    h = _rms(x_ref[...], g_ref[...]).astype(BF16)
    rc, rs1, rs2 = rc_ref[...], rs1_ref[...], rs2_ref[...]
    outs = {"fox": fox_ref, "nq": nq_ref, "nkv": nkv_ref, "dil": dil_ref}
    col = 0
    for name in _SEG_ORDER:
        for c0 in range(0, _SEG_WIDTH[name], 2 * LANES):
            y = _dot(h, w_ref[:, col + c0:col + c0 + 2 * LANES])
            for half in range(2):
                chunk = c0 // LANES + half
                yc = y[:, half * LANES:(half + 1) * LANES]
                if chunk in _ROPE_CHUNKS[name]:
                    yc = (yc * rc + pltpu.roll(yc, ROPE_HALF, 1) * rs1
                          + pltpu.roll(yc, LANES - ROPE_HALF, 1) * rs2)
                outs[name][:, chunk * LANES:(chunk + 1) * LANES] = yc.astype(BF16)
                if name == "nkv" and chunk < 2:
                    stage_ref[...] = yc
                    for t in range(CMP_STRIDE):
                        part = stage_ref[pl.ds(t, tm // CMP_STRIDE, stride=CMP_STRIDE), :]
                        (xk_ref, xv_ref)[chunk][:, t * LANES:(t + 1) * LANES] = part.astype(BF16)
                if name == "dil":
                    stage_ref[...] = yc
                    for dilation, ref in ((DIL_DILATIONS[1], dil4_ref), (DIL_DILATIONS[2], dil16_ref)):
                        for c in range(dilation):
                            part = stage_ref[pl.ds(c, tm // dilation, stride=dilation), :]
                            ref[c, :, chunk * LANES:(chunk + 1) * LANES] = part.astype(BF16)
        col += _SEG_WIDTH[name]

    ya = _dot(h, w_ref[:, _AUX_OFFSET:_AUX_OFFSET + 2 * LANES])[:, :LANES] + bias_ref[...]
    aux_ref[...] = jax.nn.sigmoid(ya)
    logf = jnp.minimum(ya, 0.0) - jnp.log1p(jnp.exp(-jnp.abs(ya)))
    csum = logf.T[:8, :]
    lane = lax.broadcasted_iota(jnp.int32, csum.shape, 1)
    shift = 1
    while shift < tm:
        csum = csum + jnp.where(lane >= shift, pltpu.roll(csum, shift, 1), 0.0)
        shift *= 2
    csum = csum + jnp.concatenate([carry_ref[...]] * (tm // LANES), axis=1)
    carry_ref[...] = jnp.broadcast_to(csum[:, tm - 1:tm], carry_ref.shape)
    ck_ref[...] = csum * LOG2E


def _inproj(x, g, w, rc, rs1, rs2, bias, *, batch, seq, tm):
    n, d = x.shape
    nt = seq // tm
    tok = lambda width: pl.BlockSpec((tm, width), lambda b, i: (b * nt + i, 0))
    const = lambda shape: pl.BlockSpec(shape, lambda b, i: (0, 0))
    cls = lambda dil: pl.BlockSpec((None, dil, tm // dil, 768), lambda b, i: (b, 0, i, 0))
    cls_shape = lambda dil: jax.ShapeDtypeStruct((batch, dil, seq // dil, 768), BF16)
    d4, d16 = DIL_DILATIONS[1], DIL_DILATIONS[2]
    blk16 = pl.BlockSpec((None, tm // CMP_STRIDE, CMP_STRIDE * LANES), lambda b, i: (b, i, 0))
    blk16_shape = jax.ShapeDtypeStruct((batch, seq // CMP_STRIDE, CMP_STRIDE * LANES), BF16)
    return pl.pallas_call(
        functools.partial(_inproj_kernel, tm=tm),
        grid=(batch, nt),
        in_specs=[tok(d), const((1, d)), const(w.shape), tok(LANES), tok(LANES), tok(LANES),
                  const((1, LANES))],
        out_specs=[tok(768), tok(512), tok(768), tok(768), cls(d4), cls(d16), tok(LANES),
                   pl.BlockSpec((None, 8, tm), lambda b, i: (b, 0, i)), blk16, blk16],
        out_shape=[jax.ShapeDtypeStruct((n, 768), BF16), jax.ShapeDtypeStruct((n, 512), BF16),
                   jax.ShapeDtypeStruct((n, 768), BF16), jax.ShapeDtypeStruct((n, 768), BF16),
                   cls_shape(d4), cls_shape(d16), jax.ShapeDtypeStruct((n, LANES), F32),
                   jax.ShapeDtypeStruct((batch, 8, seq), F32), blk16_shape, blk16_shape],
        scratch_shapes=[pltpu.VMEM((8, LANES), F32), pltpu.VMEM((tm, LANES), F32)],
        compiler_params=_params("arbitrary", "arbitrary"),
        name="inproj",
    )(x, g, w, rc, rs1, rs2, bias)


def _stack_heads(q, heads):
    lane = lax.broadcasted_iota(jnp.int32, (1, q.shape[1]), 1)
    blocks = []
    for h in range(heads):
        head = (lane >= h * HEAD_DIM) & (lane < (h + 1) * HEAD_DIM)
        blocks.append(jnp.where(head, q, jnp.zeros_like(q)))
    return jnp.concatenate(blocks, axis=0)


def _pair_values(v):
    return _with_ones(v[:, :LANES]), _with_ones(v[:, LANES:])


def _pair_pv(p, v_pair):
    half = p.shape[0] // 2
    return jnp.concatenate([_dot(p[:half], v_pair[0]), _dot(p[half:], v_pair[1])], axis=0)


def _unstack_pairs(x, tq):
    return jnp.concatenate(_unstack_halves(x, tq), axis=1)


def _unstack_halves(x, tq):
    lane = lax.broadcasted_iota(jnp.int32, (1, LANES), 1)
    return [jnp.where(lane < HEAD_DIM, x[(2 * c) * tq:(2 * c + 1) * tq],
                      x[(2 * c + 1) * tq:(2 * c + 2) * tq]) for c in range(2)]


def _fox_kernel(q_ref, k_ref, v_ref, ck_ref, o_ref, *, tq, tk):
    i = pl.program_id(1)
    start = i * tq
    rows = FOX_HEADS * tq
    qst = _stack_heads(q_ref[...], FOX_HEADS)
    qpos = start + (lax.broadcasted_iota(jnp.int32, (rows, 1), 0) & (tq - 1))
    col = lax.broadcasted_iota(jnp.int32, (1, tk), 1)
    jd = start // tk

    crow = tq // FOX_SPLIT

    def scores(j, masked):
        ks0 = pl.multiple_of(j * tk, tk)
        k = k_ref[pl.ds(ks0, tk), :]
        out = []
        for h in range(FOX_HEADS):
            for r0 in range(h * tq, (h + 1) * tq, crow):
                s = _dot_nt(qst[r0:r0 + crow], k) - ck_ref[h:h + 1, pl.ds(ks0, tk)]
                if masked:
                    s = jnp.where(ks0 + col <= qpos[r0:r0 + crow], s, NEG_INF)
                out.append(s)
        return out

    def values(j):
        v_pair = _pair_values(v_ref[pl.ds(pl.multiple_of(j * tk, tk), tk), :])
        return [v_pair[h // 2] for h in range(FOX_HEADS) for _ in range(FOX_SPLIT)]

    acc = jnp.concatenate(_flash_tiles(jd, jd, scores, values), axis=0)
    o_ref[...] = _unstack_pairs(acc[:, :LANES] / acc[:, LANES:], tq).astype(o_ref.dtype)


def _fox(fox, ck_t, *, batch, seq, tq, tk):
    n = fox.shape[0]
    nt = seq // tq
    width = FOX_HEADS * HEAD_DIM
    return pl.pallas_call(
        functools.partial(_fox_kernel, tq=tq, tk=tk),
        grid=(batch, nt),
        in_specs=[pl.BlockSpec((tq, width), lambda b, i: (b * nt + i, 0)),
                  pl.BlockSpec((seq, width), lambda b, i: (b, 1)),
                  pl.BlockSpec((seq, width), lambda b, i: (b, 2)),
                  pl.BlockSpec((None, 8, seq), lambda b, i: (b, 0, 0))],
        out_specs=pl.BlockSpec((tq, width), lambda b, i: (b * nt + i, 0)),
        out_shape=jax.ShapeDtypeStruct((n, width), BF16),
        compiler_params=_params("arbitrary", "arbitrary"),
        name="fox",
    )(fox, fox, fox, ck_t)


def _gelu_tanh(x):
    return 0.5 * x * (1.0 + jnp.tanh(math.sqrt(2.0 / math.pi) * (x + 0.044715 * (x * x * x))))


def _compress_kernel(xk_ref, xv_ref, w1k_ref, w2k_ref, pk_ref, w1v_ref, w2v_ref, pv_ref,
                     kc_ref, vc_ref):
    feat = CMP_STRIDE * LANES

    def run(x_ref, w1_ref, w2_ref, p_ref, o_ref):
        x = x_ref[...]
        n16 = x.shape[0]
        p = p_ref[...].astype(BF16)
        posb = (_dot(p[:, :feat], w1_ref[0, 0]) + _dot(p[:, feat:], w1_ref[0, 1]))[0:1, :]
        out = None
        for g in range(NSA_GROUPS):
            hid = _dot(x, w1_ref[g, 0]) + pltpu.roll(_dot(x, w1_ref[g, 1]), n16 - 1, 0) + posb
            a = _gelu_tanh(hid).astype(BF16)
            og = _dot(a, w2_ref[g])
            out = og if out is None else out + og
        o_ref[...] = out.astype(o_ref.dtype)

    run(xk_ref, w1k_ref, w2k_ref, pk_ref, kc_ref)
    run(xv_ref, w1v_ref, w2v_ref, pv_ref, vc_ref)


def _compress(xk, xv, w1k, w2k, pk, w1v, w2v, pv):
    batch, n16, feat = xk.shape
    xspec = pl.BlockSpec((None, n16, feat), lambda b: (b, 0, 0))
    full = lambda a: pl.BlockSpec(a.shape, lambda b: (0,) * a.ndim)
    ospec = pl.BlockSpec((None, n16, LANES), lambda b: (b, 0, 0))
    return pl.pallas_call(
        _compress_kernel,
        grid=(batch,),
        in_specs=[xspec, xspec, full(w1k), full(w2k), full(pk), full(w1v), full(w2v), full(pv)],
        out_specs=[ospec, ospec],
        out_shape=[jax.ShapeDtypeStruct((batch, n16, LANES), BF16)] * 2,
        compiler_params=_params("arbitrary"),
        name="nsa_compress",
    )(xk, xv, w1k, w2k, pk, w1v, w2v, pv)


def _nsa_kernel(q_ref, kcmp_ref, vcmp_ref, ks_ref, vs_ref, kw_ref, vw_ref, aux_ref, onehot_ref,
                o_ref, *, tq, tk, seq, chains):
    i = pl.program_id(1)
    start = i * tq
    n_cmp = kcmp_ref.shape[0]
    wspan = NSA_WINDOW + tq

    rows = NSA_HEADS * tq
    crow = rows // chains
    n_sel = seq // SEL_BLOCK

    lane1 = lax.broadcasted_iota(jnp.int32, (1, LANES), 1)
    qpos_col = start + (lax.broadcasted_iota(jnp.int32, (rows, 1), 0) & (tq - 1))
    qpos_row = start + (lax.broadcasted_iota(jnp.int32, (1, NSA_GROUPS * tq), 1) & (tq - 1))

    kcmp = kcmp_ref[...]
    vcmp = vcmp_ref[...]
    cmp_end = lax.broadcasted_iota(jnp.int32, (1, n_cmp), 1) * CMP_STRIDE + (CMP_BLOCK - 1)
    cmp_ok = cmp_end <= qpos_col
    cs = lax.broadcasted_iota(jnp.int32, (n_cmp, LANES), 0) * CMP_STRIDE
    ss = lax.broadcasted_iota(jnp.int32, (n_cmp, LANES), 1) * SEL_BLOCK
    overlap = jnp.where((cs < ss + SEL_BLOCK) & (cs + CMP_BLOCK > ss), 1.0, 0.0).astype(F32)

    blk = lax.broadcasted_iota(jnp.int32, (LANES, NSA_GROUPS * tq), 0)
    cur = qpos_row // SEL_BLOCK
    forced = (blk == 0) | (blk == cur) | (blk == cur - 1)
    future = blk * SEL_BLOCK > qpos_row

    wstart = pl.multiple_of(jnp.maximum(start - NSA_WINDOW, 0), tq)
    kwin = kw_ref[pl.ds(wstart, wspan), :]
    vwin = vw_ref[pl.ds(wstart, wspan), :]
    kpos_w = wstart + lax.broadcasted_iota(jnp.int32, (1, wspan), 1)
    win_ok = (kpos_w <= qpos_col) & (kpos_w > qpos_col - NSA_WINDOW)

    jd = start // tk
    col_tk = lax.broadcasted_iota(jnp.int32, (1, tk), 1)

    q_blocks = []
    for g in range(NSA_GROUPS):
        grp = (lane1 >= g * HEAD_DIM) & (lane1 < (g + 1) * HEAD_DIM)
        for r in range(NSA_REP):
            qc = q_ref[:, r * LANES:(r + 1) * LANES]
            q_blocks.append(jnp.where(grp, qc, jnp.zeros_like(qc)))
    qst = jnp.concatenate(q_blocks, axis=0)

    s = jnp.where(cmp_ok, _dot_nt(qst, kcmp), NEG_INF)
    m = jnp.max(s, axis=-1, keepdims=True)
    e = jnp.where(cmp_ok, jnp.exp2(s - m), 0.0)
    p = e / jnp.maximum(jnp.sum(e, axis=-1, keepdims=True), 1e-30)
    o_cmp = _dot(p.astype(BF16), vcmp)

    p_sum = []
    for g in range(NSA_GROUPS):
        blocks = [p[(g * NSA_REP + r) * tq:(g * NSA_REP + r + 1) * tq] for r in range(NSA_REP)]
        p_sum.append((blocks[0] + blocks[1]) + (blocks[2] + blocks[3]))
    imp = jnp.dot(jnp.concatenate(p_sum, axis=0), overlap, preferred_element_type=F32,
                  precision=lax.Precision.HIGHEST)
    imp_t = jnp.concatenate([imp[g * tq:(g + 1) * tq].T for g in range(NSA_GROUPS)], axis=1)
    imp_t = jnp.where(future, NEG_INF, jnp.where(forced, FORCE_SCORE, imp_t))
    sub = 8
    vals = [imp_t[r0:r0 + sub, :] for r0 in range(0, n_sel, sub)]
    row8 = lax.broadcasted_iota(jnp.int32, vals[0].shape, 0)
    ranks = [jnp.zeros(v.shape, jnp.int32) for v in vals]
    vwin1 = _with_ones(vwin)
    accs = []
    for c in range(chains):
        s = jnp.where(win_ok[:crow], _dot_nt(qst[c * crow:(c + 1) * crow], kwin), NEG_INF)
        accs.append(_softmax_pv(s, vwin1)[1])
        for jp in range(c * n_sel // chains, (c + 1) * n_sel // chains):
            other = imp_t[jp:jp + 1, :]
            for t, v in enumerate(vals):
                if t > jp // sub:
                    beats = other >= v
                elif t < jp // sub:
                    beats = other > v
                else:
                    beats = (other > v) | ((other == v) & (row8 > jp % sub))
                ranks[t] = ranks[t] + jnp.where(beats, 1, 0)
    rank = jnp.concatenate(ranks, axis=0)
    acc = jnp.concatenate(accs, axis=0)
    o_win = acc[:, :LANES] / acc[:, LANES:]

    aux = aux_ref[...]
    gate = lambda head, branch: aux[:, FOX_HEADS + 3 * head + branch:FOX_HEADS + 3 * head + branch + 1]
    partial = [gate(head, 0) * o_cmp[head * tq:(head + 1) * tq]
               + gate(head, 2) * o_win[head * tq:(head + 1) * tq] for head in range(NSA_HEADS)]

    sb_blocks = []
    for g in range(NSA_GROUPS):
        chosen = rank[:, g * tq:(g + 1) * tq] < min(SEL_TOPK, n_sel)
        sb_t = jnp.concatenate([jnp.where(chosen, 0.0, NEG_INF).astype(F32),
                                jnp.full((LANES - n_sel, tq), NEG_INF, F32)], axis=0)
        sb_blocks.extend([sb_t.T.astype(BF16)] * NSA_REP)
    qa = jnp.concatenate([qst, jnp.concatenate(sb_blocks, axis=0)], axis=1)

    def sel_scores(j, masked):
        ks0 = pl.multiple_of(j * tk, tk)
        ka = jnp.concatenate([ks_ref[pl.ds(ks0, tk), :], onehot_ref[pl.ds(ks0, tk), :]], axis=1)
        scores = []
        for c in range(chains):
            s = _dot_nt(qa[c * crow:(c + 1) * crow], ka)
            if masked:
                s = jnp.where(ks0 + col_tk <= qpos_col[:crow], s, NEG_INF)
            scores.append(s)
        return scores

    def sel_values(j):
        return [_with_ones(vs_ref[pl.ds(pl.multiple_of(j * tk, tk), tk), :])] * chains

    acc = jnp.concatenate(_flash_tiles(jd, jd, sel_scores, sel_values), axis=0)
    o_sel = acc[:, :LANES] / acc[:, LANES:]

    for r in range(NSA_REP):
        per_group = []
        for g in range(NSA_GROUPS):
            head = g * NSA_REP + r
            per_group.append(partial[head] + gate(head, 1) * o_sel[head * tq:(head + 1) * tq])
        o = jnp.where(lane1 < HEAD_DIM, per_group[0], per_group[1])
        o_ref[:, r * LANES:(r + 1) * LANES] = o.astype(o_ref.dtype)


def _nsa(nq, nkv, kcmp, vcmp, aux, *, batch, seq, tq, tk):
    n = nq.shape[0]
    onehot = (jnp.arange(seq)[:, None] // SEL_BLOCK == jnp.arange(LANES)[None, :]).astype(BF16)
    nt = seq // tq
    n16 = kcmp.shape[1]
    kv = lambda c: pl.BlockSpec((seq, LANES), lambda b, i: (b, c))
    cmp_spec = pl.BlockSpec((None, n16, LANES), lambda b, i: (b, 0, 0))
    width = NSA_HEADS * HEAD_DIM
    return pl.pallas_call(
        functools.partial(_nsa_kernel, tq=tq, tk=tk, seq=seq, chains=NSA_CHAINS),
        grid=(batch, nt),
        in_specs=[pl.BlockSpec((tq, width), lambda b, i: (b * nt + i, 0)),
                  cmp_spec, cmp_spec, kv(2), kv(3), kv(4), kv(5),
                  pl.BlockSpec((tq, LANES), lambda b, i: (b * nt + i, 0)),
                  pl.BlockSpec((seq, LANES), lambda b, i: (0, 0))],
        out_specs=pl.BlockSpec((tq, width), lambda b, i: (b * nt + i, 0)),
        out_shape=jax.ShapeDtypeStruct((n, width), BF16),
        compiler_params=_params("arbitrary", "arbitrary"),
        name="nsa",
    )(nq, kcmp, vcmp, nkv, nkv, nkv, nkv, aux, onehot)


_DIL_UNROLL = 16


def _dilated_kernel(q1_ref, k1_ref, v1_ref, q4_ref, k4_ref, v4_ref, q16_ref, k16_ref, v16_ref,
                    o_ref, osc_ref, lsc_ref, *, tile, seq):
    i = pl.program_id(1)
    tq = DIL_SPAN
    rows = DIL_HEADS * tq
    row_u = lax.broadcasted_iota(jnp.int32, (rows, 1), 0) & (tq - 1)
    refs = ((q1_ref, k1_ref, v1_ref), (q4_ref, k4_ref, v4_ref), (q16_ref, k16_ref, v16_ref))
    for pat, (dilation, (q_ref, k_ref, v_ref)) in enumerate(zip(DIL_DILATIONS, refs)):
        length = seq // dilation
        span = min(2 * tq, length)
        per_class = tile // (tq * dilation)
        n_sub = tile // tq
        base = i * (tile // dilation)
        rel = row_u - lax.broadcasted_iota(jnp.int32, (1, span), 1)

        def band(offset):
            return jnp.where((rel + offset >= 0) & (rel + offset <= DIL_SPAN), 0.0, NEG_INF).astype(F32)

        bias_shifted, bias_aligned = band(min(DIL_SPAN, length - span + DIL_SPAN)), band(0)

        def sub_tile(sub):
            c = sub // per_class
            w = sub % per_class
            u0 = pl.multiple_of(w * tq, tq)
            ks0 = pl.multiple_of(jnp.maximum(base + u0 - DIL_SPAN, 0), tq)
            qst = _stack_heads(q_ref[c, pl.ds(u0, tq), :], DIL_HEADS)
            bias = jnp.where(base + u0 == 0, bias_aligned, bias_shifted)
            s = _dot_nt(qst, k_ref[c, pl.ds(ks0, span), :]) + bias
            m = jnp.max(s, axis=-1, keepdims=True)
            acc = _pair_pv(jnp.exp2((s - m).astype(BF16)),
                           _pair_values(v_ref[c, pl.ds(ks0, span), :]))
            den = acc[:, LANES:]
            dst = pl.ds(u0 * dilation + c, tq, stride=dilation)
            for half, (o, lse) in enumerate(zip(_unstack_halves(acc[:, :LANES] / den, tq),
                                                _unstack_halves(m + jnp.log2(den), tq))):
                osc_ref[pat, half, dst, :] = o
                lsc_ref[pat, half, dst, :] = lse

        def trip(it, carry):
            for k in range(_DIL_UNROLL):
                sub_tile(it * _DIL_UNROLL + k)
            return carry

        lax.fori_loop(0, n_sub // _DIL_UNROLL, trip, 0)

    chunk = 2 * tq
    for r0 in range(0, tile, chunk):
        for half in range(2):
            ls = [lsc_ref[p, half, r0:r0 + chunk, :] for p in range(3)]
            top = jnp.maximum(jnp.maximum(ls[0], ls[1]), ls[2])
            es = [jnp.exp2(l - top) for l in ls]
            num = (es[0] * osc_ref[0, half, r0:r0 + chunk, :] + es[1] * osc_ref[1, half, r0:r0 + chunk, :]
                   + es[2] * osc_ref[2, half, r0:r0 + chunk, :])
            o_ref[r0:r0 + chunk, half * LANES:(half + 1) * LANES] = (
                num / (es[0] + es[1] + es[2])).astype(o_ref.dtype)


def _dilated(dil1, dil4, dil16, *, batch, seq, tile):
    width = DIL_HEADS * HEAD_DIM
    nt = seq // tile
    specs = []
    for arr in (dil1, dil4, dil16):
        dilation, length = arr.shape[1], arr.shape[2]
        specs.append(pl.BlockSpec((None, dilation, tile // dilation, width), lambda b, i: (b, 0, i, 0)))
        for part in (1, 2):
            specs.append(pl.BlockSpec((None, dilation, length, width),
                                      lambda b, i, part=part: (b, 0, 0, part),
                                      pipeline_mode=pl.Buffered(1)))
    return pl.pallas_call(
        functools.partial(_dilated_kernel, tile=tile, seq=seq),
        grid=(batch, nt),
        in_specs=specs,
        out_specs=pl.BlockSpec((tile, width), lambda b, i: (b * nt + i, 0)),
        out_shape=jax.ShapeDtypeStruct((batch * seq, width), BF16),
        scratch_shapes=[pltpu.VMEM((3, 2, tile, LANES), F32), pltpu.VMEM((3, 2, tile, LANES), F32)],
        compiler_params=_params("arbitrary", "arbitrary"),
        name="dilated",
    )(dil1, dil1, dil1, dil4, dil4, dil4, dil16, dil16, dil16)


def _mix_ffn_kernel(x_ref, fox_ref, nsa_ref, dil_ref, wout_ref, gmix_ref,
                    gpre_ref, wup_ref, cw_ref, cb_ref, wdn_ref, gpost_ref, o_ref,
                    tail_ref, ubuf_ref, act_ref, *, tm, d_ff, chunk):
    @pl.when(pl.program_id(1) == 0)
    def _():
        tail_ref[...] = jnp.zeros_like(tail_ref)

    fw = FOX_HEADS * HEAD_DIM
    nw = NSA_HEADS * HEAD_DIM
    mix = (_dot(fox_ref[...], wout_ref[:fw, :]) + _dot(nsa_ref[...], wout_ref[fw:fw + nw, :])
           + _dot(dil_ref[...], wout_ref[fw + nw:, :]))
    x = x_ref[...] + _rms(mix, gmix_ref[...])
    h = _rms(x, gpre_ref[...]).astype(BF16)
    halo = tail_ref.shape[0]

    def conv(slot, c0):
        u = _dot(h, wup_ref[:, c0:c0 + chunk])
        ubuf_ref[slot, 0:halo, :] = tail_ref[:, c0:c0 + chunk]
        ubuf_ref[slot, halo:halo + tm, :] = u
        tail_ref[:, c0:c0 + chunk] = u[tm - halo:, :]
        u1 = ubuf_ref[slot, halo - 1:halo - 1 + tm, :]
        u2 = ubuf_ref[slot, halo - 2:halo - 2 + tm, :]
        return (cb_ref[:, c0:c0 + chunk] + u2 * cw_ref[0:1, c0:c0 + chunk]
                + u1 * cw_ref[1:2, c0:c0 + chunk] + u * cw_ref[2:3, c0:c0 + chunk])

    for c0 in range(0, d_ff, chunk):
        a = conv(0, c0)
        b = conv(1, d_ff + c0)
        act_ref[:, c0:c0 + chunk] = (a * jax.nn.sigmoid(a) * b).astype(BF16)
    y = _dot(act_ref[...], wdn_ref[...])
    o_ref[...] = x + _rms(y, gpost_ref[...])


def _mix_ffn(x, fox_o, nsa_o, dil_o, wout, gmix, gpre, wup, cw, cb, wdn, gpost, *, batch, seq, tm, chunk):
    n, d = x.shape
    d_ff = wdn.shape[0]
    nt = seq // tm
    tok = lambda width: pl.BlockSpec((tm, width), lambda b, i: (b * nt + i, 0))
    const = lambda a: pl.BlockSpec(a.shape, lambda b, i: (0, 0), pipeline_mode=pl.Buffered(1))
    return pl.pallas_call(
        functools.partial(_mix_ffn_kernel, tm=tm, d_ff=d_ff, chunk=chunk),
        grid=(batch, nt),
        in_specs=[tok(d), tok(fox_o.shape[1]), tok(nsa_o.shape[1]), tok(dil_o.shape[1]),
                  const(wout), const(gmix), const(gpre), const(wup), const(cw), const(cb),
                  const(wdn), const(gpost)],
        out_specs=tok(d),
        out_shape=jax.ShapeDtypeStruct((n, d), F32),
        scratch_shapes=[pltpu.VMEM((8, 2 * d_ff), F32), pltpu.VMEM((2, 8 + tm, chunk), F32),
                        pltpu.VMEM((tm, d_ff), BF16)],
        compiler_params=_params("arbitrary", "arbitrary"),
        name="mix_ffn",
    )(x, fox_o, nsa_o, dil_o, wout, gmix, gpre, wup, cw, cb, wdn, gpost)


def _rope_tables(positions):
    inv_freq = ROPE_THETA ** (-2.0 * jnp.arange(ROPE_HALF, dtype=F32) / ROPE_DIM)
    ang = positions.astype(F32).reshape(-1, 1) * inv_freq
    cos, sin = jnp.cos(ang), jnp.sin(ang)
    n = ang.shape[0]
    ones = jnp.ones((n, HEAD_DIM - ROPE_DIM), F32)
    zeros_h = jnp.zeros((n, ROPE_HALF), F32)
    zeros_r = jnp.zeros((n, HEAD_DIM - ROPE_DIM), F32)
    rc = jnp.concatenate([cos, cos, ones], axis=1)
    rs1 = jnp.concatenate([zeros_h, sin, zeros_r], axis=1)
    rs2 = jnp.concatenate([-sin, zeros_h, zeros_r], axis=1)
    tile2 = lambda t: jnp.concatenate([t, t], axis=1)
    return tile2(rc), tile2(rs1), tile2(rs2)


def _nsa_head_perm():
    cols = []
    for r in range(NSA_REP):
        for g in range(NSA_GROUPS):
            h = g * NSA_REP + r
            cols.extend(range(h * HEAD_DIM, (h + 1) * HEAD_DIM))
    return np.asarray(cols)


def _regroup_w_in(w):
    fw, nw, kvw, dw = FOX_HEADS * HEAD_DIM, NSA_HEADS * HEAD_DIM, NSA_GROUPS * HEAD_DIM, DIL_HEADS * HEAD_DIM
    sizes = [fw, fw, fw, FOX_HEADS, nw] + [kvw] * 6 + [3 * NSA_HEADS, dw, dw, dw]
    offs = np.concatenate([[0], np.cumsum(sizes)])
    part = lambda idx: w[:, offs[idx]:offs[idx + 1]]
    fq, fk, fv, ff, nq = (part(t) for t in range(5))
    kvs = [part(t) for t in range(5, 11)]
    ng, dq, dk, dv = (part(t) for t in range(11, 15))
    nq = nq[:, _nsa_head_perm()]
    pad = jnp.zeros((w.shape[0], 2 * LANES - FOX_HEADS - 3 * NSA_HEADS), w.dtype)
    cols = [fq * Q_SCALE, fk, fv, nq * Q_SCALE] + kvs + [dq * Q_SCALE, dk, dv, ff, ng, pad]
    return jnp.concatenate(cols, axis=1).astype(BF16)


def _regroup_w_out(w):
    fw, nw = FOX_HEADS * HEAD_DIM, NSA_HEADS * HEAD_DIM
    return jnp.concatenate([w[:fw], w[fw:fw + nw][_nsa_head_perm()], w[fw + nw:]], axis=0).astype(BF16)


def _place_w1(w1):
    hidden = w1.shape[1]
    w = w1.reshape(CMP_BLOCK // CMP_STRIDE, CMP_STRIDE, HEAD_DIM, hidden)
    z = jnp.zeros_like(w)
    per_group = [jnp.concatenate([w, z], axis=2), jnp.concatenate([z, w], axis=2)]
    return jnp.stack(per_group).reshape(NSA_GROUPS, CMP_BLOCK // CMP_STRIDE, CMP_STRIDE * LANES,
                                        hidden).astype(BF16)


def _place_pos(p):
    z = jnp.zeros_like(p)
    row = jnp.concatenate([p, z], axis=1).reshape(1, -1)
    return jnp.concatenate([row, jnp.zeros((7, row.shape[1]), row.dtype)], axis=0)


def _place_w2(w2):
    z = jnp.zeros_like(w2)
    return jnp.stack([jnp.concatenate([w2, z], axis=1), jnp.concatenate([z, w2], axis=1)]).astype(BF16)


def kernel(x, positions, attn_pre_norm, attn_post_norm, ffn_pre_norm, ffn_post_norm, w_in, b_forget, b_nsa_gate, cmp_pos_k, cmp_w1_k, cmp_w2_k, cmp_pos_v, cmp_w1_v, cmp_w2_v, w_out, w_up, conv_w, conv_b, w_down):
    batch, seq, d = x.shape
    depth = w_in.shape[0]
    n = batch * seq
    tm = min(512, seq)
    assert seq % tm == 0 and seq % (16 * LANES) == 0

    rc, rs1, rs2 = _rope_tables(positions)
    xf = x.reshape(n, d)
    for l in range(depth):
        bias = jnp.concatenate([b_forget[l], b_nsa_gate[l],
                                jnp.zeros((LANES - FOX_HEADS - 3 * NSA_HEADS,), F32)]).reshape(1, LANES)
        fox, nq, nkv, dil, dil4, dil16, aux, ck_t, xk, xv = _inproj(
            xf, attn_pre_norm[l].reshape(1, d), _regroup_w_in(w_in[l]), rc, rs1, rs2, bias,
            batch=batch, seq=seq, tm=tm)
        fox_o = _fox(fox, ck_t, batch=batch, seq=seq, tq=tm, tk=tm)

        kcmp, vcmp = _compress(
            xk, xv, _place_w1(cmp_w1_k[l]), _place_w2(cmp_w2_k[l]), _place_pos(cmp_pos_k[l]),
            _place_w1(cmp_w1_v[l]), _place_w2(cmp_w2_v[l]), _place_pos(cmp_pos_v[l]))
        nsa_o = _nsa(nq, nkv, kcmp, vcmp, aux, batch=batch, seq=seq, tq=2 * LANES, tk=tm)

        dil_o = _dilated(dil.reshape(batch, 1, seq, dil.shape[1]), dil4, dil16,
                         batch=batch, seq=seq, tile=16 * LANES)

        xf = _mix_ffn(xf, fox_o, nsa_o, dil_o, _regroup_w_out(w_out[l]), attn_post_norm[l].reshape(1, d),
                      ffn_pre_norm[l].reshape(1, d), w_up[l].astype(BF16), conv_w[l],
                      conv_b[l].reshape(1, -1), w_down[l].astype(BF16), ffn_post_norm[l].reshape(1, d),
                      batch=batch, seq=seq, tm=tm, chunk=256)
    return xf.reshape(batch, seq, d)
```

```python
import functools
import math

import numpy as np
import jax
import jax.numpy as jnp
from jax import lax
from jax.experimental import pallas as pl
from jax.experimental.pallas import tpu as pltpu

HEAD_DIM = 64
FOX_HEADS = 4
NSA_HEADS = 8
NSA_GROUPS = 2
NSA_REP = NSA_HEADS // NSA_GROUPS
DIL_HEADS = 4
DIL_DILATIONS = (1, 4, 16)
DIL_SPAN = 128
ROPE_THETA = 500000.0
ROPE_DIM = HEAD_DIM // 4
ROPE_HALF = ROPE_DIM // 2
CMP_BLOCK = 32
CMP_STRIDE = 16
SEL_BLOCK = 64
SEL_TOPK = 16
NSA_WINDOW = 512
CONV_WIDTH = 3
RMS_EPS = 1e-6
NEG_INF = -1e30
FORCE_SCORE = 1e9
ATTN_SCALE = HEAD_DIM ** -0.5
LOG2E = math.log2(math.e)
Q_SCALE = ATTN_SCALE * LOG2E
LANES = 128
VMEM_LIMIT = 56 * 1024 * 1024
FLASH_GROUPS = (2, 1)
NSA_CHAINS = 4
FOX_SPLIT = 1

F32 = jnp.float32
BF16 = jnp.bfloat16


def _dot_nt(a, b):
    return lax.dot_general(a, b, (((1,), (1,)), ((), ())), preferred_element_type=F32)


def _dot(a, b):
    return jnp.dot(a, b, preferred_element_type=F32)


def _rms(x, g):
    return x * lax.rsqrt(jnp.mean(x * x, axis=-1, keepdims=True) + RMS_EPS) * g


def _params(*sem):
    return pltpu.CompilerParams(dimension_semantics=sem, vmem_limit_bytes=VMEM_LIMIT)


def _with_ones(v):
    return jnp.concatenate([v, jnp.ones((v.shape[0], LANES), v.dtype)], axis=1)


def _softmax_pv(s, v1):
    m = jnp.max(s, axis=-1, keepdims=True)
    return m, _dot(jnp.exp2((s - m).astype(BF16)), v1)


def _flash_tiles(n_before, first, score_fn, value_fn):
    state = []
    for s, v1 in zip(score_fn(first, True), value_fn(first)):
        m = jnp.max(s, axis=-1, keepdims=True)
        state.append((m, _dot(jnp.exp2((s - m).astype(BF16)), v1)))

    def step(j, state):
        out = []
        for s, v1, (m, acc) in zip(score_fn(j, False), value_fn(j), state):
            m_new = jnp.maximum(m, jnp.max(s, axis=-1, keepdims=True))
            p = jnp.exp2((s - m_new).astype(BF16))
            out.append((m_new, jnp.exp2(m - m_new) * acc + _dot(p, v1)))
        return tuple(out)

    state = tuple(state)
    done = 0
    for size in FLASH_GROUPS:
        def group(t, state, size=size, done=done):
            for u in range(size):
                state = step(done + t * size + u, state)
            return state

        trips = (n_before - done) // size
        state = lax.fori_loop(0, trips, group, state)
        done = done + trips * size
    return [acc for _, acc in state]


_ROPE_CHUNKS = {"fox": (), "nq": (0, 1, 2, 3), "nkv": (0, 2, 4), "dil": (0, 1, 2, 3)}
_SEG_WIDTH = {"fox": 768, "nq": 512, "nkv": 768, "dil": 768}
_SEG_ORDER = ("fox", "nq", "nkv", "dil")
_AUX_OFFSET = sum(_SEG_WIDTH.values())
_W_IN_COLS = _AUX_OFFSET + 2 * LANES


def _inproj_kernel(x_ref, g_ref, w_ref, rc_ref, rs1_ref, rs2_ref, bias_ref,
                   fox_ref, nq_ref, nkv_ref, dil_ref, dil4_ref, dil16_ref, aux_ref, ck_ref,
                   xk_ref, xv_ref, carry_ref, stage_ref, *, tm):
    @pl.when(pl.program_id(1) == 0)
    def _():
        carry_ref[...] = jnp.zeros_like(carry_ref)

    h = _rms(x_ref[...], g_ref[...]).astype(BF16)
    rc, rs1, rs2 = rc_ref[...], rs1_ref[...], rs2_ref[...]
    outs = {"fox": fox_ref, "nq": nq_ref, "nkv": nkv_ref, "dil": dil_ref}
    col = 0
    for name in _SEG_ORDER:
        for c0 in range(0, _SEG_WIDTH[name], 2 * LANES):
            y = _dot(h, w_ref[:, col + c0:col + c0 + 2 * LANES])
            for half in range(2):
                chunk = c0 // LANES + half
                yc = y[:, half * LANES:(half + 1) * LANES]
                if chunk in _ROPE_CHUNKS[name]:
                    yc = (yc * rc + pltpu.roll(yc, ROPE_HALF, 1) * rs1
                          + pltpu.roll(yc, LANES - ROPE_HALF, 1) * rs2)
                outs[name][:, chunk * LANES:(chunk + 1) * LANES] = yc.astype(BF16)
                if name == "nkv" and chunk < 2:
                    stage_ref[...] = yc
                    for t in range(CMP_STRIDE):
                        part = stage_ref[pl.ds(t, tm // CMP_STRIDE, stride=CMP_STRIDE), :]
                        (xk_ref, xv_ref)[chunk][:, t * LANES:(t + 1) * LANES] = part.astype(BF16)
                if name == "dil":
                    stage_ref[...] = yc
                    for dilation, ref in ((DIL_DILATIONS[1], dil4_ref), (DIL_DILATIONS[2], dil16_ref)):
                        for c in range(dilation):
                            part = stage_ref[pl.ds(c, tm // dilation, stride=dilation), :]
                            ref[c, :, chunk * LANES:(chunk + 1) * LANES] = part.astype(BF16)
        col += _SEG_WIDTH[name]

    ya = _dot(h, w_ref[:, _AUX_OFFSET:_AUX_OFFSET + 2 * LANES])[:, :LANES] + bias_ref[...]
    aux_ref[...] = jax.nn.sigmoid(ya)
    logf = jnp.minimum(ya, 0.0) - jnp.log1p(jnp.exp(-jnp.abs(ya)))
    csum = logf.T[:8, :]
    lane = lax.broadcasted_iota(jnp.int32, csum.shape, 1)
    shift = 1
    while shift < tm:
        csum = csum + jnp.where(lane >= shift, pltpu.roll(csum, shift, 1), 0.0)
        shift *= 2
    csum = csum + jnp.concatenate([carry_ref[...]] * (tm // LANES), axis=1)
    carry_ref[...] = jnp.broadcast_to(csum[:, tm - 1:tm], carry_ref.shape)
    ck_ref[...] = csum * LOG2E


def _inproj(x, g, w, rc, rs1, rs2, bias, *, batch, seq, tm):
    n, d = x.shape
    nt = seq // tm
    tok = lambda width: pl.BlockSpec((tm, width), lambda b, i: (b * nt + i, 0))
    const = lambda shape: pl.BlockSpec(shape, lambda b, i: (0, 0))
    cls = lambda dil: pl.BlockSpec((None, dil, tm // dil, 768), lambda b, i: (b, 0, i, 0))
    cls_shape = lambda dil: jax.ShapeDtypeStruct((batch, dil, seq // dil, 768), BF16)
    d4, d16 = DIL_DILATIONS[1], DIL_DILATIONS[2]
    blk16 = pl.BlockSpec((None, tm // CMP_STRIDE, CMP_STRIDE * LANES), lambda b, i: (b, i, 0))
    blk16_shape = jax.ShapeDtypeStruct((batch, seq // CMP_STRIDE, CMP_STRIDE * LANES), BF16)
    return pl.pallas_call(
        functools.partial(_inproj_kernel, tm=tm),
        grid=(batch, nt),
        in_specs=[tok(d), const((1, d)), const(w.shape), tok(LANES), tok(LANES), tok(LANES),
                  const((1, LANES))],
        out_specs=[tok(768), tok(512), tok(768), tok(768), cls(d4), cls(d16), tok(LANES),
                   pl.BlockSpec((None, 8, tm), lambda b, i: (b, 0, i)), blk16, blk16],
        out_shape=[jax.ShapeDtypeStruct((n, 768), BF16), jax.ShapeDtypeStruct((n, 512), BF16),
                   jax.ShapeDtypeStruct((n, 768), BF16), jax.ShapeDtypeStruct((n, 768), BF16),
                   cls_shape(d4), cls_shape(d16), jax.ShapeDtypeStruct((n, LANES), F32),
                   jax.ShapeDtypeStruct((batch, 8, seq), F32), blk16_shape, blk16_shape],
        scratch_shapes=[pltpu.VMEM((8, LANES), F32), pltpu.VMEM((tm, LANES), F32)],
        compiler_params=_params("arbitrary", "arbitrary"),
        name="inproj",
    )(x, g, w, rc, rs1, rs2, bias)


def _stack_heads(q, heads):
    lane = lax.broadcasted_iota(jnp.int32, (1, q.shape[1]), 1)
    blocks = []
    for h in range(heads):
        head = (lane >= h * HEAD_DIM) & (lane < (h + 1) * HEAD_DIM)
        blocks.append(jnp.where(head, q, jnp.zeros_like(q)))
    return jnp.concatenate(blocks, axis=0)


def _pair_values(v):
    return _with_ones(v[:, :LANES]), _with_ones(v[:, LANES:])


def _pair_pv(p, v_pair):
    half = p.shape[0] // 2
    return jnp.concatenate([_dot(p[:half], v_pair[0]), _dot(p[half:], v_pair[1])], axis=0)


def _unstack_pairs(x, tq):
    return jnp.concatenate(_unstack_halves(x, tq), axis=1)


def _unstack_halves(x, tq):
    lane = lax.broadcasted_iota(jnp.int32, (1, LANES), 1)
    return [jnp.where(lane < HEAD_DIM, x[(2 * c) * tq:(2 * c + 1) * tq],
                      x[(2 * c + 1) * tq:(2 * c + 2) * tq]) for c in range(2)]


def _fox_kernel(q_ref, k_ref, v_ref, ck_ref, o_ref, *, tq, tk):
    i = pl.program_id(1)
    start = i * tq
    rows = FOX_HEADS * tq
    qst = _stack_heads(q_ref[...], FOX_HEADS)
    qpos = start + (lax.broadcasted_iota(jnp.int32, (rows, 1), 0) & (tq - 1))
    col = lax.broadcasted_iota(jnp.int32, (1, tk), 1)
    jd = start // tk

    crow = tq // FOX_SPLIT

    def scores(j, masked):
        ks0 = pl.multiple_of(j * tk, tk)
        k = k_ref[pl.ds(ks0, tk), :]
        out = []
        for h in range(FOX_HEADS):
            for r0 in range(h * tq, (h + 1) * tq, crow):
                s = _dot_nt(qst[r0:r0 + crow], k) - ck_ref[h:h + 1, pl.ds(ks0, tk)]
                if masked:
                    s = jnp.where(ks0 + col <= qpos[r0:r0 + crow], s, NEG_INF)
                out.append(s)
        return out

    def values(j):
        v_pair = _pair_values(v_ref[pl.ds(pl.multiple_of(j * tk, tk), tk), :])
        return [v_pair[h // 2] for h in range(FOX_HEADS) for _ in range(FOX_SPLIT)]

    acc = jnp.concatenate(_flash_tiles(jd, jd, scores, values), axis=0)
    o_ref[...] = _unstack_pairs(acc[:, :LANES] / acc[:, LANES:], tq).astype(o_ref.dtype)


def _fox(fox, ck_t, *, batch, seq, tq, tk):
    n = fox.shape[0]
    nt = seq // tq
    width = FOX_HEADS * HEAD_DIM
    return pl.pallas_call(
        functools.partial(_fox_kernel, tq=tq, tk=tk),
        grid=(batch, nt),
        in_specs=[pl.BlockSpec((tq, width), lambda b, i: (b * nt + i, 0)),
                  pl.BlockSpec((seq, width), lambda b, i: (b, 1)),
                  pl.BlockSpec((seq, width), lambda b, i: (b, 2)),
                  pl.BlockSpec((None, 8, seq), lambda b, i: (b, 0, 0))],
        out_specs=pl.BlockSpec((tq, width), lambda b, i: (b * nt + i, 0)),
        out_shape=jax.ShapeDtypeStruct((n, width), BF16),
        compiler_params=_params("arbitrary", "arbitrary"),
        name="fox",
    )(fox, fox, fox, ck_t)


def _gelu_tanh(x):
    return 0.5 * x * (1.0 + jnp.tanh(math.sqrt(2.0 / math.pi) * (x + 0.044715 * (x * x * x))))


def _compress_kernel(xk_ref, xv_ref, w1k_ref, w2k_ref, pk_ref, w1v_ref, w2v_ref, pv_ref,
                     kc_ref, vc_ref):
    feat = CMP_STRIDE * LANES

    def run(x_ref, w1_ref, w2_ref, p_ref, o_ref):
        x = x_ref[...]
        n16 = x.shape[0]
        p = p_ref[...].astype(BF16)
        posb = (_dot(p[:, :feat], w1_ref[0, 0]) + _dot(p[:, feat:], w1_ref[0, 1]))[0:1, :]
        out = None
        for g in range(NSA_GROUPS):
            hid = _dot(x, w1_ref[g, 0]) + pltpu.roll(_dot(x, w1_ref[g, 1]), n16 - 1, 0) + posb
            a = _gelu_tanh(hid).astype(BF16)
            og = _dot(a, w2_ref[g])
            out = og if out is None else out + og
        o_ref[...] = out.astype(o_ref.dtype)

    run(xk_ref, w1k_ref, w2k_ref, pk_ref, kc_ref)
    run(xv_ref, w1v_ref, w2v_ref, pv_ref, vc_ref)


def _compress(xk, xv, w1k, w2k, pk, w1v, w2v, pv):
    batch, n16, feat = xk.shape
    xspec = pl.BlockSpec((None, n16, feat), lambda b: (b, 0, 0))
    full = lambda a: pl.BlockSpec(a.shape, lambda b: (0,) * a.ndim)
    ospec = pl.BlockSpec((None, n16, LANES), lambda b: (b, 0, 0))
    return pl.pallas_call(
        _compress_kernel,
        grid=(batch,),
        in_specs=[xspec, xspec, full(w1k), full(w2k), full(pk), full(w1v), full(w2v), full(pv)],
        out_specs=[ospec, ospec],
        out_shape=[jax.ShapeDtypeStruct((batch, n16, LANES), BF16)] * 2,
        compiler_params=_params("arbitrary"),
        name="nsa_compress",
    )(xk, xv, w1k, w2k, pk, w1v, w2v, pv)


def _nsa_kernel(q_ref, kcmp_ref, vcmp_ref, ks_ref, vs_ref, kw_ref, vw_ref, aux_ref, onehot_ref,
                o_ref, *, tq, tk, seq, chains):
    i = pl.program_id(1)
    start = i * tq
    n_cmp = kcmp_ref.shape[0]
    wspan = NSA_WINDOW + tq

    rows = NSA_HEADS * tq
    crow = rows // chains
    n_sel = seq // SEL_BLOCK

    lane1 = lax.broadcasted_iota(jnp.int32, (1, LANES), 1)
    qpos_col = start + (lax.broadcasted_iota(jnp.int32, (rows, 1), 0) & (tq - 1))
    qpos_row = start + (lax.broadcasted_iota(jnp.int32, (1, NSA_GROUPS * tq), 1) & (tq - 1))

    kcmp = kcmp_ref[...]
    vcmp = vcmp_ref[...]
    cmp_end = lax.broadcasted_iota(jnp.int32, (1, n_cmp), 1) * CMP_STRIDE + (CMP_BLOCK - 1)
    cmp_ok = cmp_end <= qpos_col
    cs = lax.broadcasted_iota(jnp.int32, (n_cmp, LANES), 0) * CMP_STRIDE
    ss = lax.broadcasted_iota(jnp.int32, (n_cmp, LANES), 1) * SEL_BLOCK
    overlap = jnp.where((cs < ss + SEL_BLOCK) & (cs + CMP_BLOCK > ss), 1.0, 0.0).astype(F32)

    blk = lax.broadcasted_iota(jnp.int32, (LANES, NSA_GROUPS * tq), 0)
    cur = qpos_row // SEL_BLOCK
    forced = (blk == 0) | (blk == cur) | (blk == cur - 1)
    future = blk * SEL_BLOCK > qpos_row

    wstart = pl.multiple_of(jnp.maximum(start - NSA_WINDOW, 0), tq)
    kwin = kw_ref[pl.ds(wstart, wspan), :]
    vwin = vw_ref[pl.ds(wstart, wspan), :]
    kpos_w = wstart + lax.broadcasted_iota(jnp.int32, (1, wspan), 1)
    win_ok = (kpos_w <= qpos_col) & (kpos_w > qpos_col - NSA_WINDOW)

    jd = start // tk
    col_tk = lax.broadcasted_iota(jnp.int32, (1, tk), 1)

    q_blocks = []
    for g in range(NSA_GROUPS):
        grp = (lane1 >= g * HEAD_DIM) & (lane1 < (g + 1) * HEAD_DIM)
        for r in range(NSA_REP):
            qc = q_ref[:, r * LANES:(r + 1) * LANES]
            q_blocks.append(jnp.where(grp, qc, jnp.zeros_like(qc)))
    qst = jnp.concatenate(q_blocks, axis=0)

    s = jnp.where(cmp_ok, _dot_nt(qst, kcmp), NEG_INF)
    m = jnp.max(s, axis=-1, keepdims=True)
    e = jnp.where(cmp_ok, jnp.exp2(s - m), 0.0)
    p = e / jnp.maximum(jnp.sum(e, axis=-1, keepdims=True), 1e-30)
    o_cmp = _dot(p.astype(BF16), vcmp)

    p_sum = []
    for g in range(NSA_GROUPS):
        blocks = [p[(g * NSA_REP + r) * tq:(g * NSA_REP + r + 1) * tq] for r in range(NSA_REP)]
        p_sum.append((blocks[0] + blocks[1]) + (blocks[2] + blocks[3]))
    imp = jnp.dot(jnp.concatenate(p_sum, axis=0), overlap, preferred_element_type=F32,
                  precision=lax.Precision.HIGHEST)
    imp_t = jnp.concatenate([imp[g * tq:(g + 1) * tq].T for g in range(NSA_GROUPS)], axis=1)
    imp_t = jnp.where(future, NEG_INF, jnp.where(forced, FORCE_SCORE, imp_t))
    sub = 8
    vals = [imp_t[r0:r0 + sub, :] for r0 in range(0, n_sel, sub)]
    row8 = lax.broadcasted_iota(jnp.int32, vals[0].shape, 0)
    ranks = [jnp.zeros(v.shape, jnp.int32) for v in vals]
    vwin1 = _with_ones(vwin)
    accs = []
    for c in range(chains):
        s = jnp.where(win_ok[:crow], _dot_nt(qst[c * crow:(c + 1) * crow], kwin), NEG_INF)
        accs.append(_softmax_pv(s, vwin1)[1])
        for jp in range(c * n_sel // chains, (c + 1) * n_sel // chains):
            other = imp_t[jp:jp + 1, :]
            for t, v in enumerate(vals):
                if t > jp // sub:
                    beats = other >= v
                elif t < jp // sub:
                    beats = other > v
                else:
                    beats = (other > v) | ((other == v) & (row8 > jp % sub))
                ranks[t] = ranks[t] + jnp.where(beats, 1, 0)
    rank = jnp.concatenate(ranks, axis=0)
    acc = jnp.concatenate(accs, axis=0)
    o_win = acc[:, :LANES] / acc[:, LANES:]

    aux = aux_ref[...]
    gate = lambda head, branch: aux[:, FOX_HEADS + 3 * head + branch:FOX_HEADS + 3 * head + branch + 1]
    partial = [gate(head, 0) * o_cmp[head * tq:(head + 1) * tq]
               + gate(head, 2) * o_win[head * tq:(head + 1) * tq] for head in range(NSA_HEADS)]

    sb_blocks = []
    for g in range(NSA_GROUPS):
        chosen = rank[:, g * tq:(g + 1) * tq] < min(SEL_TOPK, n_sel)
        sb_t = jnp.concatenate([jnp.where(chosen, 0.0, NEG_INF).astype(F32),
                                jnp.full((LANES - n_sel, tq), NEG_INF, F32)], axis=0)
        sb_blocks.extend([sb_t.T.astype(BF16)] * NSA_REP)
    qa = jnp.concatenate([qst, jnp.concatenate(sb_blocks, axis=0)], axis=1)

    def sel_scores(j, masked):
        ks0 = pl.multiple_of(j * tk, tk)
        ka = jnp.concatenate([ks_ref[pl.ds(ks0, tk), :], onehot_ref[pl.ds(ks0, tk), :]], axis=1)
        scores = []
        for c in range(chains):
            s = _dot_nt(qa[c * crow:(c + 1) * crow], ka)
            if masked:
                s = jnp.where(ks0 + col_tk <= qpos_col[:crow], s, NEG_INF)
            scores.append(s)
        return scores

    def sel_values(j):
        return [_with_ones(vs_ref[pl.ds(pl.multiple_of(j * tk, tk), tk), :])] * chains

    acc = jnp.concatenate(_flash_tiles(jd, jd, sel_scores, sel_values), axis=0)
    o_sel = acc[:, :LANES] / acc[:, LANES:]

    for r in range(NSA_REP):
        per_group = []
        for g in range(NSA_GROUPS):
            head = g * NSA_REP + r
            per_group.append(partial[head] + gate(head, 1) * o_sel[head * tq:(head + 1) * tq])
        o = jnp.where(lane1 < HEAD_DIM, per_group[0], per_group[1])
        o_ref[:, r * LANES:(r + 1) * LANES] = o.astype(o_ref.dtype)


def _nsa(nq, nkv, kcmp, vcmp, aux, *, batch, seq, tq, tk):
    n = nq.shape[0]
    onehot = (jnp.arange(seq)[:, None] // SEL_BLOCK == jnp.arange(LANES)[None, :]).astype(BF16)
    nt = seq // tq
    n16 = kcmp.shape[1]
    kv = lambda c: pl.BlockSpec((seq, LANES), lambda b, i: (b, c))
    cmp_spec = pl.BlockSpec((None, n16, LANES), lambda b, i: (b, 0, 0))
    width = NSA_HEADS * HEAD_DIM
    return pl.pallas_call(
        functools.partial(_nsa_kernel, tq=tq, tk=tk, seq=seq, chains=NSA_CHAINS),
        grid=(batch, nt),
        in_specs=[pl.BlockSpec((tq, width), lambda b, i: (b * nt + i, 0)),
                  cmp_spec, cmp_spec, kv(2), kv(3), kv(4), kv(5),
                  pl.BlockSpec((tq, LANES), lambda b, i: (b * nt + i, 0)),
                  pl.BlockSpec((seq, LANES), lambda b, i: (0, 0))],
        out_specs=pl.BlockSpec((tq, width), lambda b, i: (b * nt + i, 0)),
        out_shape=jax.ShapeDtypeStruct((n, width), BF16),
        compiler_params=_params("arbitrary", "arbitrary"),
        name="nsa",
    )(nq, kcmp, vcmp, nkv, nkv, nkv, nkv, aux, onehot)


def _dilated_kernel(q1_ref, k1_ref, v1_ref, q4_ref, k4_ref, v4_ref, q16_ref, k16_ref, v16_ref,
                    o_ref, osc_ref, lsc_ref, *, tile, seq):
    i = pl.program_id(1)
    tq = DIL_SPAN
    rows = DIL_HEADS * tq
    row_u = lax.broadcasted_iota(jnp.int32, (rows, 1), 0) & (tq - 1)
    refs = ((q1_ref, k1_ref, v1_ref), (q4_ref, k4_ref, v4_ref), (q16_ref, k16_ref, v16_ref))
    for pat, (dilation, (q_ref, k_ref, v_ref)) in enumerate(zip(DIL_DILATIONS, refs)):
        length = seq // dilation
        span = min(2 * tq, length)
        per_class = tile // (tq * dilation)
        n_sub = tile // tq
        base = i * (tile // dilation)
        rel = row_u - lax.broadcasted_iota(jnp.int32, (1, span), 1)

        def band(offset):
            return jnp.where((rel + offset >= 0) & (rel + offset <= DIL_SPAN), 0.0, NEG_INF).astype(F32)

        bias_shifted, bias_aligned = band(min(DIL_SPAN, length - span + DIL_SPAN)), band(0)

        def sub_tile(sub):
            c = sub // per_class
            w = sub % per_class
            u0 = w * tq
            ks0 = pl.multiple_of(jnp.maximum(base + u0 - DIL_SPAN, 0), tq)
            qst = _stack_heads(q_ref[c, pl.ds(u0, tq), :], DIL_HEADS)
            bias = jnp.where(base == 0, bias_aligned, bias_shifted) if u0 == 0 else bias_shifted
            s = _dot_nt(qst, k_ref[c, pl.ds(ks0, span), :]) + bias
            m = jnp.max(s, axis=-1, keepdims=True)
            acc = _pair_pv(jnp.exp2((s - m).astype(BF16)),
                           _pair_values(v_ref[c, pl.ds(ks0, span), :]))
            den = acc[:, LANES:]
            dst = pl.ds(u0 * dilation + c, tq, stride=dilation)
            for half, (o, lse) in enumerate(zip(_unstack_halves(acc[:, :LANES] / den, tq),
                                                _unstack_halves(m + jnp.log2(den), tq))):
                osc_ref[pat, half, dst, :] = o
                lsc_ref[pat, half, dst, :] = lse

        for sub in range(n_sub):
            sub_tile(sub)

    chunk = 2 * tq
    for r0 in range(0, tile, chunk):
        for half in range(2):
            ls = [lsc_ref[p, half, r0:r0 + chunk, :] for p in range(3)]
            top = jnp.maximum(jnp.maximum(ls[0], ls[1]), ls[2])
            es = [jnp.exp2(l - top) for l in ls]
            num = (es[0] * osc_ref[0, half, r0:r0 + chunk, :] + es[1] * osc_ref[1, half, r0:r0 + chunk, :]
                   + es[2] * osc_ref[2, half, r0:r0 + chunk, :])
            o_ref[r0:r0 + chunk, half * LANES:(half + 1) * LANES] = (
                num / (es[0] + es[1] + es[2])).astype(o_ref.dtype)


def _dilated(dil1, dil4, dil16, *, batch, seq, tile):
    width = DIL_HEADS * HEAD_DIM
    nt = seq // tile
    specs = []
    for arr in (dil1, dil4, dil16):
        dilation, length = arr.shape[1], arr.shape[2]
        specs.append(pl.BlockSpec((None, dilation, tile // dilation, width), lambda b, i: (b, 0, i, 0)))
        for part in (1, 2):
            specs.append(pl.BlockSpec((None, dilation, length, width),
                                      lambda b, i, part=part: (b, 0, 0, part)))
    return pl.pallas_call(
        functools.partial(_dilated_kernel, tile=tile, seq=seq),
        grid=(batch, nt),
        in_specs=specs,
        out_specs=pl.BlockSpec((tile, width), lambda b, i: (b * nt + i, 0)),
        out_shape=jax.ShapeDtypeStruct((batch * seq, width), BF16),
        scratch_shapes=[pltpu.VMEM((3, 2, tile, LANES), F32), pltpu.VMEM((3, 2, tile, LANES), F32)],
        compiler_params=_params("arbitrary", "arbitrary"),
        name="dilated",
    )(dil1, dil1, dil1, dil4, dil4, dil4, dil16, dil16, dil16)


def _mix_ffn_kernel(x_ref, fox_ref, nsa_ref, dil_ref, wout_ref, gmix_ref,
                    gpre_ref, wup_ref, cw_ref, cb_ref, wdn_ref, gpost_ref, o_ref,
                    tail_ref, ubuf_ref, act_ref, *, tm, d_ff, chunk):
    @pl.when(pl.program_id(1) == 0)
    def _():
        tail_ref[...] = jnp.zeros_like(tail_ref)

    fw = FOX_HEADS * HEAD_DIM
    nw = NSA_HEADS * HEAD_DIM
    mix = (_dot(fox_ref[...], wout_ref[:fw, :]) + _dot(nsa_ref[...], wout_ref[fw:fw + nw, :])
           + _dot(dil_ref[...], wout_ref[fw + nw:, :]))
    x = x_ref[...] + _rms(mix, gmix_ref[...])
    h = _rms(x, gpre_ref[...]).astype(BF16)
    halo = tail_ref.shape[0]

    def conv(slot, c0):
        u = _dot(h, wup_ref[:, c0:c0 + chunk])
        ubuf_ref[slot, 0:halo, :] = tail_ref[:, c0:c0 + chunk]
        ubuf_ref[slot, halo:halo + tm, :] = u
        tail_ref[:, c0:c0 + chunk] = u[tm - halo:, :]
        u1 = ubuf_ref[slot, halo - 1:halo - 1 + tm, :]
        u2 = ubuf_ref[slot, halo - 2:halo - 2 + tm, :]
        return (cb_ref[:, c0:c0 + chunk] + u2 * cw_ref[0:1, c0:c0 + chunk]
                + u1 * cw_ref[1:2, c0:c0 + chunk] + u * cw_ref[2:3, c0:c0 + chunk])

    for c0 in range(0, d_ff, chunk):
        a = conv(0, c0)
        b = conv(1, d_ff + c0)
        act_ref[:, c0:c0 + chunk] = (a * jax.nn.sigmoid(a) * b).astype(BF16)
    y = _dot(act_ref[...], wdn_ref[...])
    o_ref[...] = x + _rms(y, gpost_ref[...])


def _mix_ffn(x, fox_o, nsa_o, dil_o, wout, gmix, gpre, wup, cw, cb, wdn, gpost, *, batch, seq, tm, chunk):
    n, d = x.shape
    d_ff = wdn.shape[0]
    nt = seq // tm
    tok = lambda width: pl.BlockSpec((tm, width), lambda b, i: (b * nt + i, 0))
    const = lambda a: pl.BlockSpec(a.shape, lambda b, i: (0, 0), pipeline_mode=pl.Buffered(1))
    return pl.pallas_call(
        functools.partial(_mix_ffn_kernel, tm=tm, d_ff=d_ff, chunk=chunk),
        grid=(batch, nt),
        in_specs=[tok(d), tok(fox_o.shape[1]), tok(nsa_o.shape[1]), tok(dil_o.shape[1]),
                  const(wout), const(gmix), const(gpre), const(wup), const(cw), const(cb),
                  const(wdn), const(gpost)],
        out_specs=tok(d),
        out_shape=jax.ShapeDtypeStruct((n, d), F32),
        scratch_shapes=[pltpu.VMEM((8, 2 * d_ff), F32), pltpu.VMEM((2, 8 + tm, chunk), F32),
                        pltpu.VMEM((tm, d_ff), BF16)],
        compiler_params=_params("arbitrary", "arbitrary"),
        name="mix_ffn",
    )(x, fox_o, nsa_o, dil_o, wout, gmix, gpre, wup, cw, cb, wdn, gpost)


def _rope_tables(positions):
    inv_freq = ROPE_THETA ** (-2.0 * jnp.arange(ROPE_HALF, dtype=F32) / ROPE_DIM)
    ang = positions.astype(F32).reshape(-1, 1) * inv_freq
    cos, sin = jnp.cos(ang), jnp.sin(ang)
    n = ang.shape[0]
    ones = jnp.ones((n, HEAD_DIM - ROPE_DIM), F32)
    zeros_h = jnp.zeros((n, ROPE_HALF), F32)
    zeros_r = jnp.zeros((n, HEAD_DIM - ROPE_DIM), F32)
    rc = jnp.concatenate([cos, cos, ones], axis=1)
    rs1 = jnp.concatenate([zeros_h, sin, zeros_r], axis=1)
    rs2 = jnp.concatenate([-sin, zeros_h, zeros_r], axis=1)
    tile2 = lambda t: jnp.concatenate([t, t], axis=1)
    return tile2(rc), tile2(rs1), tile2(rs2)


def _nsa_head_perm():
    cols = []
    for r in range(NSA_REP):
        for g in range(NSA_GROUPS):
            h = g * NSA_REP + r
            cols.extend(range(h * HEAD_DIM, (h + 1) * HEAD_DIM))
    return np.asarray(cols)


def _regroup_w_in(w):
    fw, nw, kvw, dw = FOX_HEADS * HEAD_DIM, NSA_HEADS * HEAD_DIM, NSA_GROUPS * HEAD_DIM, DIL_HEADS * HEAD_DIM
    sizes = [fw, fw, fw, FOX_HEADS, nw] + [kvw] * 6 + [3 * NSA_HEADS, dw, dw, dw]
    offs = np.concatenate([[0], np.cumsum(sizes)])
    part = lambda idx: w[:, offs[idx]:offs[idx + 1]]
    fq, fk, fv, ff, nq = (part(t) for t in range(5))
    kvs = [part(t) for t in range(5, 11)]
    ng, dq, dk, dv = (part(t) for t in range(11, 15))
    nq = nq[:, _nsa_head_perm()]
    pad = jnp.zeros((w.shape[0], 2 * LANES - FOX_HEADS - 3 * NSA_HEADS), w.dtype)
    cols = [fq * Q_SCALE, fk, fv, nq * Q_SCALE] + kvs + [dq * Q_SCALE, dk, dv, ff, ng, pad]
    return jnp.concatenate(cols, axis=1).astype(BF16)


def _regroup_w_out(w):
    fw, nw = FOX_HEADS * HEAD_DIM, NSA_HEADS * HEAD_DIM
    return jnp.concatenate([w[:fw], w[fw:fw + nw][_nsa_head_perm()], w[fw + nw:]], axis=0).astype(BF16)


def _place_w1(w1):
    hidden = w1.shape[1]
    w = w1.reshape(CMP_BLOCK // CMP_STRIDE, CMP_STRIDE, HEAD_DIM, hidden)
    z = jnp.zeros_like(w)
    per_group = [jnp.concatenate([w, z], axis=2), jnp.concatenate([z, w], axis=2)]
    return jnp.stack(per_group).reshape(NSA_GROUPS, CMP_BLOCK // CMP_STRIDE, CMP_STRIDE * LANES,
                                        hidden).astype(BF16)


def _place_pos(p):
    z = jnp.zeros_like(p)
    row = jnp.concatenate([p, z], axis=1).reshape(1, -1)
    return jnp.concatenate([row, jnp.zeros((7, row.shape[1]), row.dtype)], axis=0)


def _place_w2(w2):
    z = jnp.zeros_like(w2)
    return jnp.stack([jnp.concatenate([w2, z], axis=1), jnp.concatenate([z, w2], axis=1)]).astype(BF16)


def kernel(x, positions, attn_pre_norm, attn_post_norm, ffn_pre_norm, ffn_post_norm, w_in, b_forget, b_nsa_gate, cmp_pos_k, cmp_w1_k, cmp_w2_k, cmp_pos_v, cmp_w1_v, cmp_w2_v, w_out, w_up, conv_w, conv_b, w_down):
    batch, seq, d = x.shape
    depth = w_in.shape[0]
    n = batch * seq
    tm = min(512, seq)
    assert seq % tm == 0 and seq % (16 * LANES) == 0

    rc, rs1, rs2 = _rope_tables(positions)
    xf = x.reshape(n, d)
    for l in range(depth):
        bias = jnp.concatenate([b_forget[l], b_nsa_gate[l],
                                jnp.zeros((LANES - FOX_HEADS - 3 * NSA_HEADS,), F32)]).reshape(1, LANES)
        fox, nq, nkv, dil, dil4, dil16, aux, ck_t, xk, xv = _inproj(
            xf, attn_pre_norm[l].reshape(1, d), _regroup_w_in(w_in[l]), rc, rs1, rs2, bias,
            batch=batch, seq=seq, tm=2 * tm)
        fox_o = _fox(fox, ck_t, batch=batch, seq=seq, tq=tm, tk=tm)

        kcmp, vcmp = _compress(
            xk, xv, _place_w1(cmp_w1_k[l]), _place_w2(cmp_w2_k[l]), _place_pos(cmp_pos_k[l]),
            _place_w1(cmp_w1_v[l]), _place_w2(cmp_w2_v[l]), _place_pos(cmp_pos_v[l]))
        nsa_o = _nsa(nq, nkv, kcmp, vcmp, aux, batch=batch, seq=seq, tq=2 * LANES, tk=tm)

        dil_o = _dilated(dil.reshape(batch, 1, seq, dil.shape[1]), dil4, dil16,
                         batch=batch, seq=seq, tile=16 * LANES)

        xf = _mix_ffn(xf, fox_o, nsa_o, dil_o, _regroup_w_out(w_out[l]), attn_post_norm[l].reshape(1, d),
                      ffn_pre_norm[l].reshape(1, d), w_up[l].astype(BF16), conv_w[l],
                      conv_b[l].reshape(1, -1), w_down[l].astype(BF16), ffn_post_norm[l].reshape(1, d),
                      batch=batch, seq=seq, tm=tm, chunk=256)
    return xf.reshape(batch, seq, d)
```

```python
import functools
import math
from typing import NamedTuple

import numpy as np
import jax
import jax.numpy as jnp
from jax import lax
from jax.experimental import pallas as pl
from jax.experimental.pallas import tpu as pltpu

HEAD_DIM = 64
FOX_HEADS = 4
NSA_HEADS = 8
NSA_GROUPS = 2
NSA_REP = NSA_HEADS // NSA_GROUPS
DIL_HEADS = 4
DIL_DILATIONS = (1, 4, 16)
DIL_SPAN = 128
ROPE_THETA = 500000.0
ROPE_DIM = HEAD_DIM // 4
ROPE_HALF = ROPE_DIM // 2
CMP_BLOCK = 32
CMP_STRIDE = 16
SEL_BLOCK = 64
SEL_TOPK = 16
NSA_WINDOW = 512
CONV_WIDTH = 3
RMS_EPS = 1e-6
NEG_INF = -1e30
FORCE_SCORE = 1e9
ATTN_SCALE = HEAD_DIM ** -0.5
LOG2E = math.log2(math.e)
Q_SCALE = ATTN_SCALE * LOG2E
LANES = 128
VMEM_LIMIT = 56 * 1024 * 1024
FLASH_GROUPS = (2, 1)
NSA_CHAINS = 4
FOX_SPLIT = 1

F32 = jnp.float32
BF16 = jnp.bfloat16


def _dot_nt(a, b):
    return lax.dot_general(a, b, (((1,), (1,)), ((), ())), preferred_element_type=F32)


def _dot(a, b):
    return jnp.dot(a, b, preferred_element_type=F32)


def _rms(x, g):
    return x * lax.rsqrt(jnp.mean(x * x, axis=-1, keepdims=True) + RMS_EPS) * g


def _params(*sem):
    return pltpu.CompilerParams(dimension_semantics=sem, vmem_limit_bytes=VMEM_LIMIT)


def _with_ones(v):
    return jnp.concatenate([v, jnp.ones((v.shape[0], LANES), v.dtype)], axis=1)


def _softmax_pv(s, v1):
    m = jnp.max(s, axis=-1, keepdims=True)
    return m, _dot(jnp.exp2((s - m).astype(BF16)), v1)


def _flash_tiles(n_before, first, score_fn, value_fn):
    state = []
    for s, v1 in zip(score_fn(first, True), value_fn(first)):
        m = jnp.max(s, axis=-1, keepdims=True)
        state.append((m, _dot(jnp.exp2((s - m).astype(BF16)), v1)))

    def step(j, state):
        out = []
        for s, v1, (m, acc) in zip(score_fn(j, False), value_fn(j), state):
            m_new = jnp.maximum(m, jnp.max(s, axis=-1, keepdims=True))
            p = jnp.exp2((s - m_new).astype(BF16))
            out.append((m_new, jnp.exp2(m - m_new) * acc + _dot(p, v1)))
        return tuple(out)

    state = tuple(state)
    done = 0
    for size in FLASH_GROUPS:
        def group(t, state, size=size, done=done):
            for u in range(size):
                state = step(done + t * size + u, state)
            return state

        trips = (n_before - done) // size
        state = lax.fori_loop(0, trips, group, state)
        done = done + trips * size
    return [acc for _, acc in state]


_ROPE_CHUNKS = {"fox": (), "nq": (0, 1, 2, 3), "nkv": (0, 2, 4), "dil": (0, 1, 2, 3)}
_SEG_WIDTH = {"fox": 768, "nq": 512, "nkv": 768, "dil": 768}
_SEG_ORDER = ("fox", "nq", "nkv", "dil")
_AUX_OFFSET = sum(_SEG_WIDTH.values())
_W_IN_COLS = _AUX_OFFSET + 2 * LANES


def _inproj_kernel(x_ref, g_ref, w_ref, rc_ref, rs1_ref, rs2_ref, bias_ref,
                   fox_ref, nq_ref, nkv_ref, dil_ref, dil4_ref, dil16_ref, aux_ref, ck_ref,
                   xk_ref, xv_ref, carry_ref, stage_ref, *, tm):
    @pl.when(pl.program_id(1) == 0)
    def _():
        carry_ref[...] = jnp.zeros_like(carry_ref)

    h = _rms(x_ref[...], g_ref[...]).astype(BF16)
    rc, rs1, rs2 = rc_ref[...], rs1_ref[...], rs2_ref[...]
    outs = {"fox": fox_ref, "nq": nq_ref, "nkv": nkv_ref, "dil": dil_ref}
    col = 0
    for name in _SEG_ORDER:
        for c0 in range(0, _SEG_WIDTH[name], 2 * LANES):
            y = _dot(h, w_ref[:, col + c0:col + c0 + 2 * LANES])
            for half in range(2):
                chunk = c0 // LANES + half
                yc = y[:, half * LANES:(half + 1) * LANES]
                if chunk in _ROPE_CHUNKS[name]:
                    yc = (yc * rc + pltpu.roll(yc, ROPE_HALF, 1) * rs1
                          + pltpu.roll(yc, LANES - ROPE_HALF, 1) * rs2)
                outs[name][:, chunk * LANES:(chunk + 1) * LANES] = yc.astype(BF16)
                if name == "nkv" and chunk < 2:
                    stage_ref[...] = yc
                    for t in range(CMP_STRIDE):
                        part = stage_ref[pl.ds(t, tm // CMP_STRIDE, stride=CMP_STRIDE), :]
                        (xk_ref, xv_ref)[chunk][:, t * LANES:(t + 1) * LANES] = part.astype(BF16)
                if name == "dil":
                    stage_ref[...] = yc
                    for dilation, ref in ((DIL_DILATIONS[1], dil4_ref), (DIL_DILATIONS[2], dil16_ref)):
                        for c in range(dilation):
                            part = stage_ref[pl.ds(c, tm // dilation, stride=dilation), :]
                            ref[c, :, chunk * LANES:(chunk + 1) * LANES] = part.astype(BF16)
        col += _SEG_WIDTH[name]

    ya = _dot(h, w_ref[:, _AUX_OFFSET:_AUX_OFFSET + 2 * LANES])[:, :LANES] + bias_ref[...]
    aux_ref[...] = jax.nn.sigmoid(ya)
    logf = jnp.minimum(ya, 0.0) - jnp.log1p(jnp.exp(-jnp.abs(ya)))
    csum = logf.T[:8, :]
    lane = lax.broadcasted_iota(jnp.int32, csum.shape, 1)
    shift = 1
    while shift < tm:
        csum = csum + jnp.where(lane >= shift, pltpu.roll(csum, shift, 1), 0.0)
        shift *= 2
    csum = csum + jnp.concatenate([carry_ref[...]] * (tm // LANES), axis=1)
    carry_ref[...] = jnp.broadcast_to(csum[:, tm - 1:tm], carry_ref.shape)
    ck_ref[...] = csum * LOG2E


def _inproj(x, g, w, rc, rs1, rs2, bias, *, batch, seq, tm):
    n, d = x.shape
    nt = seq // tm
    tok = lambda width: pl.BlockSpec((tm, width), lambda b, i: (b * nt + i, 0))
    const = lambda shape: pl.BlockSpec(shape, lambda b, i: (0, 0))
    cls = lambda dil: pl.BlockSpec((None, dil, tm // dil, 768), lambda b, i: (b, 0, i, 0))
    cls_shape = lambda dil: jax.ShapeDtypeStruct((batch, dil, seq // dil, 768), BF16)
    d4, d16 = DIL_DILATIONS[1], DIL_DILATIONS[2]
    blk16 = pl.BlockSpec((None, tm // CMP_STRIDE, CMP_STRIDE * LANES), lambda b, i: (b, i, 0))
    blk16_shape = jax.ShapeDtypeStruct((batch, seq // CMP_STRIDE, CMP_STRIDE * LANES), BF16)
    return pl.pallas_call(
        functools.partial(_inproj_kernel, tm=tm),
        grid=(batch, nt),
        in_specs=[tok(d), const((1, d)), const(w.shape), tok(LANES), tok(LANES), tok(LANES),
                  const((1, LANES))],
        out_specs=[tok(768), tok(512), tok(768), tok(768), cls(d4), cls(d16), tok(LANES),
                   pl.BlockSpec((None, 8, tm), lambda b, i: (b, 0, i)), blk16, blk16],
        out_shape=[jax.ShapeDtypeStruct((n, 768), BF16), jax.ShapeDtypeStruct((n, 512), BF16),
                   jax.ShapeDtypeStruct((n, 768), BF16), jax.ShapeDtypeStruct((n, 768), BF16),
                   cls_shape(d4), cls_shape(d16), jax.ShapeDtypeStruct((n, LANES), F32),
                   jax.ShapeDtypeStruct((batch, 8, seq), F32), blk16_shape, blk16_shape],
        scratch_shapes=[pltpu.VMEM((8, LANES), F32), pltpu.VMEM((tm, LANES), F32)],
        compiler_params=_params("arbitrary", "arbitrary"),
        name="inproj",
    )(x, g, w, rc, rs1, rs2, bias)


def _stack_heads(q, heads):
    lane = lax.broadcasted_iota(jnp.int32, (1, q.shape[1]), 1)
    blocks = []
    for h in range(heads):
        head = (lane >= h * HEAD_DIM) & (lane < (h + 1) * HEAD_DIM)
        blocks.append(jnp.where(head, q, jnp.zeros_like(q)))
    return jnp.concatenate(blocks, axis=0)


def _pair_values(v):
    return _with_ones(v[:, :LANES]), _with_ones(v[:, LANES:])


def _pair_pv(p, v_pair):
    half = p.shape[0] // 2
    return jnp.concatenate([_dot(p[:half], v_pair[0]), _dot(p[half:], v_pair[1])], axis=0)


def _unstack_pairs(x, tq):
    return jnp.concatenate(_unstack_halves(x, tq), axis=1)


def _unstack_halves(x, tq):
    lane = lax.broadcasted_iota(jnp.int32, (1, LANES), 1)
    return [jnp.where(lane < HEAD_DIM, x[(2 * c) * tq:(2 * c + 1) * tq],
                      x[(2 * c + 1) * tq:(2 * c + 2) * tq]) for c in range(2)]


def _fox_kernel(q_ref, k_ref, v_ref, ck_ref, o_ref, *, tq, tk):
    i = pl.program_id(1)
    start = i * tq
    rows = FOX_HEADS * tq
    qst = _stack_heads(q_ref[...], FOX_HEADS)
    jd = start // tk
    qpos = start + lax.broadcasted_iota(jnp.int32, (tq, 1), 0)
    kpos = jd * tk + lax.broadcasted_iota(jnp.int32, (1, tk), 1)
    diag_bias = jnp.where(kpos <= qpos, 0.0, NEG_INF).astype(F32)

    crow = tq // FOX_SPLIT

    def scores(j, masked):
        ks0 = pl.multiple_of(j * tk, tk)
        k = k_ref[pl.ds(ks0, tk), :]
        out = []
        for h in range(FOX_HEADS):
            for r0 in range(h * tq, (h + 1) * tq, crow):
                s = _dot_nt(qst[r0:r0 + crow], k) - ck_ref[h:h + 1, pl.ds(ks0, tk)]
                if masked:
                    s = s + diag_bias[r0 - h * tq:r0 - h * tq + crow]
                out.append(s)
        return out

    def values(j):
        v_pair = _pair_values(v_ref[pl.ds(pl.multiple_of(j * tk, tk), tk), :])
        return [v_pair[h // 2] for h in range(FOX_HEADS) for _ in range(FOX_SPLIT)]

    acc = jnp.concatenate(_flash_tiles(jd, jd, scores, values), axis=0)
    o_ref[...] = _unstack_pairs(acc[:, :LANES] / acc[:, LANES:], tq).astype(o_ref.dtype)


def _fox(fox, ck_t, *, batch, seq, tq, tk):
    n = fox.shape[0]
    nt = seq // tq
    width = FOX_HEADS * HEAD_DIM
    return pl.pallas_call(
        functools.partial(_fox_kernel, tq=tq, tk=tk),
        grid=(batch, nt),
        in_specs=[pl.BlockSpec((tq, width), lambda b, i: (b * nt + i, 0)),
                  pl.BlockSpec((seq, width), lambda b, i: (b, 1)),
                  pl.BlockSpec((seq, width), lambda b, i: (b, 2)),
                  pl.BlockSpec((None, 8, seq), lambda b, i: (b, 0, 0))],
        out_specs=pl.BlockSpec((tq, width), lambda b, i: (b * nt + i, 0)),
        out_shape=jax.ShapeDtypeStruct((n, width), BF16),
        compiler_params=_params("arbitrary", "arbitrary"),
        name="fox",
    )(fox, fox, fox, ck_t)


def _gelu_tanh(x):
    return 0.5 * x * (1.0 + jnp.tanh(math.sqrt(2.0 / math.pi) * (x + 0.044715 * (x * x * x))))


def _compress_kernel(xk_ref, xv_ref, w1k_ref, w2k_ref, pk_ref, w1v_ref, w2v_ref, pv_ref,
                     kc_ref, vc_ref):
    feat = CMP_STRIDE * LANES

    def run(x_ref, w1_ref, w2_ref, p_ref, o_ref):
        x = x_ref[...]
        n16 = x.shape[0]
        p = p_ref[...].astype(BF16)
        posb = (_dot(p[:, :feat], w1_ref[0, 0]) + _dot(p[:, feat:], w1_ref[0, 1]))[0:1, :]
        out = None
        for g in range(NSA_GROUPS):
            hid = _dot(x, w1_ref[g, 0]) + pltpu.roll(_dot(x, w1_ref[g, 1]), n16 - 1, 0) + posb
            a = _gelu_tanh(hid).astype(BF16)
            og = _dot(a, w2_ref[g])
            out = og if out is None else out + og
        o_ref[...] = out.astype(o_ref.dtype)

    run(xk_ref, w1k_ref, w2k_ref, pk_ref, kc_ref)
    run(xv_ref, w1v_ref, w2v_ref, pv_ref, vc_ref)


def _compress(xk, xv, w1k, w2k, pk, w1v, w2v, pv):
    batch, n16, feat = xk.shape
    xspec = pl.BlockSpec((None, n16, feat), lambda b: (b, 0, 0))
    full = lambda a: pl.BlockSpec(a.shape, lambda b: (0,) * a.ndim)
    ospec = pl.BlockSpec((None, n16, LANES), lambda b: (b, 0, 0))
    return pl.pallas_call(
        _compress_kernel,
        grid=(batch,),
        in_specs=[xspec, xspec, full(w1k), full(w2k), full(pk), full(w1v), full(w2v), full(pv)],
        out_specs=[ospec, ospec],
        out_shape=[jax.ShapeDtypeStruct((batch, n16, LANES), BF16)] * 2,
        compiler_params=_params("arbitrary"),
        name="nsa_compress",
    )(xk, xv, w1k, w2k, pk, w1v, w2v, pv)


def _nsa_kernel(q_ref, kcmp_ref, vcmp_ref, ks_ref, vs_ref, kw_ref, vw_ref, aux_ref, onehot_ref,
                o_ref, *, tq, tk, seq, chains):
    i = pl.program_id(1)
    start = i * tq
    n_cmp = kcmp_ref.shape[0]
    wspan = NSA_WINDOW + tq

    rows = NSA_HEADS * tq
    crow = rows // chains
    n_sel = seq // SEL_BLOCK

    lane1 = lax.broadcasted_iota(jnp.int32, (1, LANES), 1)
    qpos_col = start + (lax.broadcasted_iota(jnp.int32, (rows, 1), 0) & (tq - 1))
    qpos_row = start + (lax.broadcasted_iota(jnp.int32, (1, NSA_GROUPS * tq), 1) & (tq - 1))

    kcmp = kcmp_ref[...]
    vcmp = vcmp_ref[...]
    cmp_end = lax.broadcasted_iota(jnp.int32, (1, n_cmp), 1) * CMP_STRIDE + (CMP_BLOCK - 1)
    cmp_bias = jnp.where(cmp_end <= qpos_col[:tq], 0.0, NEG_INF).astype(F32)
    cs = lax.broadcasted_iota(jnp.int32, (n_cmp, LANES), 0) * CMP_STRIDE
    ss = lax.broadcasted_iota(jnp.int32, (n_cmp, LANES), 1) * SEL_BLOCK
    overlap = jnp.where((cs < ss + SEL_BLOCK) & (cs + CMP_BLOCK > ss), 1.0, 0.0).astype(F32)

    blk = lax.broadcasted_iota(jnp.int32, (LANES, NSA_GROUPS * tq), 0)
    cur = qpos_row // SEL_BLOCK
    forced = (blk == 0) | (blk == cur) | (blk == cur - 1)
    future = blk * SEL_BLOCK > qpos_row

    wstart = pl.multiple_of(jnp.maximum(start - NSA_WINDOW, 0), tq)
    kwin = kw_ref[pl.ds(wstart, wspan), :]
    vwin = vw_ref[pl.ds(wstart, wspan), :]
    kpos_w = wstart + lax.broadcasted_iota(jnp.int32, (1, wspan), 1)
    win_ok = (kpos_w <= qpos_col) & (kpos_w > qpos_col - NSA_WINDOW)

    jd = start // tk
    col_tk = jd * tk + lax.broadcasted_iota(jnp.int32, (1, tk), 1)
    diag_bias = jnp.concatenate(
        [jnp.where(col_tk <= qpos_col[:tq], 0.0, NEG_INF).astype(F32)] * (crow // tq), axis=0)

    q_blocks = []
    for g in range(NSA_GROUPS):
        grp = (lane1 >= g * HEAD_DIM) & (lane1 < (g + 1) * HEAD_DIM)
        for r in range(NSA_REP):
            qc = q_ref[:, r * LANES:(r + 1) * LANES]
            q_blocks.append(jnp.where(grp, qc, jnp.zeros_like(qc)))
    qst = jnp.concatenate(q_blocks, axis=0)

    s = _dot_nt(qst, kcmp) + jnp.concatenate([cmp_bias] * NSA_HEADS, axis=0)
    m = jnp.max(s, axis=-1, keepdims=True)
    e = jnp.exp2(s - m)
    has_key = qpos_col >= CMP_BLOCK - 1
    p = e * jnp.where(has_key, 1.0 / jnp.sum(e, axis=-1, keepdims=True), 0.0)
    o_cmp = _dot(p.astype(BF16), vcmp)

    p_sum = []
    for g in range(NSA_GROUPS):
        blocks = [p[(g * NSA_REP + r) * tq:(g * NSA_REP + r + 1) * tq] for r in range(NSA_REP)]
        p_sum.append((blocks[0] + blocks[1]) + (blocks[2] + blocks[3]))
    imp = jnp.dot(jnp.concatenate(p_sum, axis=0), overlap, preferred_element_type=F32,
                  precision=lax.Precision.HIGHEST)
    imp_t = jnp.concatenate([imp[g * tq:(g + 1) * tq].T for g in range(NSA_GROUPS)], axis=1)
    imp_t = jnp.where(future, NEG_INF, jnp.where(forced, FORCE_SCORE, imp_t))
    sub = 8
    vals = [imp_t[r0:r0 + sub, :] for r0 in range(0, n_sel, sub)]
    row8 = lax.broadcasted_iota(jnp.int32, vals[0].shape, 0)
    ranks = [jnp.zeros(v.shape, jnp.int32) for v in vals]
    vwin1 = _with_ones(vwin)
    accs = []
    for c in range(chains):
        s = jnp.where(win_ok[:crow], _dot_nt(qst[c * crow:(c + 1) * crow], kwin), NEG_INF)
        accs.append(_softmax_pv(s, vwin1)[1])
        for jp in range(c * n_sel // chains, (c + 1) * n_sel // chains):
            other = imp_t[jp:jp + 1, :]
            for t, v in enumerate(vals):
                if t > jp // sub:
                    beats = other >= v
                elif t < jp // sub:
                    beats = other > v
                else:
                    beats = (other > v) | ((other == v) & (row8 > jp % sub))
                ranks[t] = ranks[t] + jnp.where(beats, 1, 0)
    rank = jnp.concatenate(ranks, axis=0)
    acc = jnp.concatenate(accs, axis=0)
    o_win = acc[:, :LANES] / acc[:, LANES:]

    aux = aux_ref[...]
    gate = lambda head, branch: aux[:, FOX_HEADS + 3 * head + branch:FOX_HEADS + 3 * head + branch + 1]
    partial = [gate(head, 0) * o_cmp[head * tq:(head + 1) * tq]
               + gate(head, 2) * o_win[head * tq:(head + 1) * tq] for head in range(NSA_HEADS)]

    sb_blocks = []
    for g in range(NSA_GROUPS):
        chosen = rank[:, g * tq:(g + 1) * tq] < min(SEL_TOPK, n_sel)
        sb_t = jnp.concatenate([jnp.where(chosen, 0.0, NEG_INF).astype(F32),
                                jnp.full((LANES - n_sel, tq), NEG_INF, F32)], axis=0)
        sb_blocks.extend([sb_t.T.astype(BF16)] * NSA_REP)
    qa = jnp.concatenate([qst, jnp.concatenate(sb_blocks, axis=0)], axis=1)

    def sel_scores(j, masked):
        ks0 = pl.multiple_of(j * tk, tk)
        ka = jnp.concatenate([ks_ref[pl.ds(ks0, tk), :], onehot_ref[pl.ds(ks0, tk), :]], axis=1)
        scores = []
        for c in range(chains):
            s = _dot_nt(qa[c * crow:(c + 1) * crow], ka)
            if masked:
                s = s + diag_bias
            scores.append(s)
        return scores

    def sel_values(j):
        return [_with_ones(vs_ref[pl.ds(pl.multiple_of(j * tk, tk), tk), :])] * chains

    acc = jnp.concatenate(_flash_tiles(jd, jd, sel_scores, sel_values), axis=0)
    o_sel = acc[:, :LANES] / acc[:, LANES:]

    for r in range(NSA_REP):
        per_group = []
        for g in range(NSA_GROUPS):
            head = g * NSA_REP + r
            per_group.append(partial[head] + gate(head, 1) * o_sel[head * tq:(head + 1) * tq])
        o = jnp.where(lane1 < HEAD_DIM, per_group[0], per_group[1])
        o_ref[:, r * LANES:(r + 1) * LANES] = o.astype(o_ref.dtype)


def _nsa(nq, nkv, kcmp, vcmp, aux, *, batch, seq, tq, tk):
    n = nq.shape[0]
    onehot = (jnp.arange(seq)[:, None] // SEL_BLOCK == jnp.arange(LANES)[None, :]).astype(BF16)
    nt = seq // tq
    n16 = kcmp.shape[1]
    kv = lambda c: pl.BlockSpec((seq, LANES), lambda b, i: (b, c))
    cmp_spec = pl.BlockSpec((None, n16, LANES), lambda b, i: (b, 0, 0))
    width = NSA_HEADS * HEAD_DIM
    return pl.pallas_call(
        functools.partial(_nsa_kernel, tq=tq, tk=tk, seq=seq, chains=NSA_CHAINS),
        grid=(batch, nt),
        in_specs=[pl.BlockSpec((tq, width), lambda b, i: (b * nt + i, 0)),
                  cmp_spec, cmp_spec, kv(2), kv(3), kv(4), kv(5),
                  pl.BlockSpec((tq, LANES), lambda b, i: (b * nt + i, 0)),
                  pl.BlockSpec((seq, LANES), lambda b, i: (0, 0))],
        out_specs=pl.BlockSpec((tq, width), lambda b, i: (b * nt + i, 0)),
        out_shape=jax.ShapeDtypeStruct((n, width), BF16),
        compiler_params=_params("arbitrary", "arbitrary"),
        name="nsa",
    )(nq, kcmp, vcmp, nkv, nkv, nkv, nkv, aux, onehot)


def _dilated_kernel(q1_ref, k1_ref, v1_ref, q4_ref, k4_ref, v4_ref, q16_ref, k16_ref, v16_ref,
                    o_ref, osc_ref, lsc_ref, *, tile, seq):
    i = pl.program_id(1)
    tq = DIL_SPAN
    rows = DIL_HEADS * tq
    row_u = lax.broadcasted_iota(jnp.int32, (rows, 1), 0) & (tq - 1)
    refs = ((q1_ref, k1_ref, v1_ref), (q4_ref, k4_ref, v4_ref), (q16_ref, k16_ref, v16_ref))
    for pat, (dilation, (q_ref, k_ref, v_ref)) in enumerate(zip(DIL_DILATIONS, refs)):
        length = seq // dilation
        span = min(2 * tq, length)
        per_class = tile // (tq * dilation)
        n_sub = tile // tq
        base = i * (tile // dilation)
        rel = row_u - lax.broadcasted_iota(jnp.int32, (1, span), 1)

        def band(offset):
            return jnp.where((rel + offset >= 0) & (rel + offset <= DIL_SPAN), 0.0, NEG_INF).astype(F32)

        bias_shifted, bias_aligned = band(min(DIL_SPAN, length - span + DIL_SPAN)), band(0)

        def sub_tile(sub):
            c = sub // per_class
            w = sub % per_class
            u0 = w * tq
            ks0 = pl.multiple_of(jnp.maximum(base + u0 - DIL_SPAN, 0), tq)
            qst = _stack_heads(q_ref[c, pl.ds(u0, tq), :], DIL_HEADS)
            bias = jnp.where(base == 0, bias_aligned, bias_shifted) if u0 == 0 else bias_shifted
            s = _dot_nt(qst, k_ref[c, pl.ds(ks0, span), :]) + bias
            m = jnp.max(s, axis=-1, keepdims=True)
            acc = _pair_pv(jnp.exp2((s - m).astype(BF16)),
                           _pair_values(v_ref[c, pl.ds(ks0, span), :]))
            den = acc[:, LANES:]
            dst = pl.ds(u0 * dilation + c, tq, stride=dilation)
            for half, (o, lse) in enumerate(zip(_unstack_halves(acc[:, :LANES] / den, tq),
                                                _unstack_halves(m + jnp.log2(den), tq))):
                osc_ref[pat, half, dst, :] = o
                lsc_ref[pat, half, dst, :] = lse

        for sub in range(n_sub):
            sub_tile(sub)

    chunk = 2 * tq
    for r0 in range(0, tile, chunk):
        for half in range(2):
            ls = [lsc_ref[p, half, r0:r0 + chunk, :] for p in range(3)]
            top = jnp.maximum(jnp.maximum(ls[0], ls[1]), ls[2])
            es = [jnp.exp2(l - top) for l in ls]
            num = (es[0] * osc_ref[0, half, r0:r0 + chunk, :] + es[1] * osc_ref[1, half, r0:r0 + chunk, :]
                   + es[2] * osc_ref[2, half, r0:r0 + chunk, :])
            o_ref[r0:r0 + chunk, half * LANES:(half + 1) * LANES] = (
                num / (es[0] + es[1] + es[2])).astype(o_ref.dtype)


def _dilated(dil1, dil4, dil16, *, batch, seq, tile):
    width = DIL_HEADS * HEAD_DIM
    nt = seq // tile
    specs = []
    for arr in (dil1, dil4, dil16):
        dilation, length = arr.shape[1], arr.shape[2]
        specs.append(pl.BlockSpec((None, dilation, tile // dilation, width), lambda b, i: (b, 0, i, 0)))
        for part in (1, 2):
            specs.append(pl.BlockSpec((None, dilation, length, width),
                                      lambda b, i, part=part: (b, 0, 0, part)))
    return pl.pallas_call(
        functools.partial(_dilated_kernel, tile=tile, seq=seq),
        grid=(batch, nt),
        in_specs=specs,
        out_specs=pl.BlockSpec((tile, width), lambda b, i: (b * nt + i, 0)),
        out_shape=jax.ShapeDtypeStruct((batch * seq, width), BF16),
        scratch_shapes=[pltpu.VMEM((3, 2, tile, LANES), F32), pltpu.VMEM((3, 2, tile, LANES), F32)],
        compiler_params=_params("arbitrary", "arbitrary"),
        name="dilated",
    )(dil1, dil1, dil1, dil4, dil4, dil4, dil16, dil16, dil16)


def _mix_ffn_kernel(x_ref, fox_ref, nsa_ref, dil_ref, wout_ref, gmix_ref,
                    gpre_ref, wup_ref, cw_ref, cb_ref, wdn_ref, gpost_ref, o_ref,
                    tail_ref, ubuf_ref, act_ref, *, tm, d_ff, chunk):
    @pl.when(pl.program_id(1) == 0)
    def _():
        tail_ref[...] = jnp.zeros_like(tail_ref)

    fw = FOX_HEADS * HEAD_DIM
    nw = NSA_HEADS * HEAD_DIM
    mix = (_dot(fox_ref[...], wout_ref[:fw, :]) + _dot(nsa_ref[...], wout_ref[fw:fw + nw, :])
           + _dot(dil_ref[...], wout_ref[fw + nw:, :]))
    x = x_ref[...] + _rms(mix, gmix_ref[...])
    h = _rms(x, gpre_ref[...]).astype(BF16)
    halo = tail_ref.shape[0]

    def conv(slot, c0):
        u = _dot(h, wup_ref[:, c0:c0 + chunk])
        ubuf_ref[slot, 0:halo, :] = tail_ref[:, c0:c0 + chunk]
        ubuf_ref[slot, halo:halo + tm, :] = u
        tail_ref[:, c0:c0 + chunk] = u[tm - halo:, :]
        u1 = ubuf_ref[slot, halo - 1:halo - 1 + tm, :]
        u2 = ubuf_ref[slot, halo - 2:halo - 2 + tm, :]
        return (cb_ref[:, c0:c0 + chunk] + u2 * cw_ref[0:1, c0:c0 + chunk]
                + u1 * cw_ref[1:2, c0:c0 + chunk] + u * cw_ref[2:3, c0:c0 + chunk])

    for c0 in range(0, d_ff, chunk):
        a = conv(0, c0)
        b = conv(1, d_ff + c0)
        act_ref[:, c0:c0 + chunk] = (a * jax.nn.sigmoid(a) * b).astype(BF16)
    y = _dot(act_ref[...], wdn_ref[...])
    o_ref[...] = x + _rms(y, gpost_ref[...])


def _mix_ffn(x, fox_o, nsa_o, dil_o, wout, gmix, gpre, wup, cw, cb, wdn, gpost, *, batch, seq, tm, chunk):
    n, d = x.shape
    d_ff = wdn.shape[0]
    nt = seq // tm
    tok = lambda width: pl.BlockSpec((tm, width), lambda b, i: (b * nt + i, 0))
    const = lambda a: pl.BlockSpec(a.shape, lambda b, i: (0, 0), pipeline_mode=pl.Buffered(1))
    return pl.pallas_call(
        functools.partial(_mix_ffn_kernel, tm=tm, d_ff=d_ff, chunk=chunk),
        grid=(batch, nt),
        in_specs=[tok(d), tok(fox_o.shape[1]), tok(nsa_o.shape[1]), tok(dil_o.shape[1]),
                  const(wout), const(gmix), const(gpre), const(wup), const(cw), const(cb),
                  const(wdn), const(gpost)],
        out_specs=tok(d),
        out_shape=jax.ShapeDtypeStruct((n, d), F32),
        scratch_shapes=[pltpu.VMEM((8, 2 * d_ff), F32), pltpu.VMEM((2, 8 + tm, chunk), F32),
                        pltpu.VMEM((tm, d_ff), BF16)],
        compiler_params=_params("arbitrary", "arbitrary"),
        name="mix_ffn",
    )(x, fox_o, nsa_o, dil_o, wout, gmix, gpre, wup, cw, cb, wdn, gpost)


def _rope_tables(positions):
    inv_freq = ROPE_THETA ** (-2.0 * jnp.arange(ROPE_HALF, dtype=F32) / ROPE_DIM)
    ang = positions.astype(F32).reshape(-1, 1) * inv_freq
    cos, sin = jnp.cos(ang), jnp.sin(ang)
    n = ang.shape[0]
    ones = jnp.ones((n, HEAD_DIM - ROPE_DIM), F32)
    zeros_h = jnp.zeros((n, ROPE_HALF), F32)
    zeros_r = jnp.zeros((n, HEAD_DIM - ROPE_DIM), F32)
    rc = jnp.concatenate([cos, cos, ones], axis=1)
    rs1 = jnp.concatenate([zeros_h, sin, zeros_r], axis=1)
    rs2 = jnp.concatenate([-sin, zeros_h, zeros_r], axis=1)
    tile2 = lambda t: jnp.concatenate([t, t], axis=1)
    return tile2(rc), tile2(rs1), tile2(rs2)


def _nsa_head_perm():
    cols = []
    for r in range(NSA_REP):
        for g in range(NSA_GROUPS):
            h = g * NSA_REP + r
            cols.extend(range(h * HEAD_DIM, (h + 1) * HEAD_DIM))
    return np.asarray(cols)


def _regroup_w_in(w):
    fw, nw, kvw, dw = FOX_HEADS * HEAD_DIM, NSA_HEADS * HEAD_DIM, NSA_GROUPS * HEAD_DIM, DIL_HEADS * HEAD_DIM
    sizes = [fw, fw, fw, FOX_HEADS, nw] + [kvw] * 6 + [3 * NSA_HEADS, dw, dw, dw]
    offs = np.concatenate([[0], np.cumsum(sizes)])
    part = lambda idx: w[:, offs[idx]:offs[idx + 1]]
    fq, fk, fv, ff, nq = (part(t) for t in range(5))
    kvs = [part(t) for t in range(5, 11)]
    ng, dq, dk, dv = (part(t) for t in range(11, 15))
    nq = nq[:, _nsa_head_perm()]
    pad = jnp.zeros((w.shape[0], 2 * LANES - FOX_HEADS - 3 * NSA_HEADS), w.dtype)
    cols = [fq * Q_SCALE, fk, fv, nq * Q_SCALE] + kvs + [dq * Q_SCALE, dk, dv, ff, ng, pad]
    return jnp.concatenate(cols, axis=1).astype(BF16)


def _regroup_w_out(w):
    fw, nw = FOX_HEADS * HEAD_DIM, NSA_HEADS * HEAD_DIM
    return jnp.concatenate([w[:fw], w[fw:fw + nw][_nsa_head_perm()], w[fw + nw:]], axis=0).astype(BF16)


def _place_w1(w1):
    hidden = w1.shape[1]
    w = w1.reshape(CMP_BLOCK // CMP_STRIDE, CMP_STRIDE, HEAD_DIM, hidden)
    z = jnp.zeros_like(w)
    per_group = [jnp.concatenate([w, z], axis=2), jnp.concatenate([z, w], axis=2)]
    return jnp.stack(per_group).reshape(NSA_GROUPS, CMP_BLOCK // CMP_STRIDE, CMP_STRIDE * LANES,
                                        hidden).astype(BF16)


def _place_pos(p):
    z = jnp.zeros_like(p)
    row = jnp.concatenate([p, z], axis=1).reshape(1, -1)
    return jnp.concatenate([row, jnp.zeros((7, row.shape[1]), row.dtype)], axis=0)


def _place_w2(w2):
    z = jnp.zeros_like(w2)
    return jnp.stack([jnp.concatenate([w2, z], axis=1), jnp.concatenate([z, w2], axis=1)]).astype(BF16)


class _Tiles(NamedTuple):
    inproj: int
    key: int
    fox_q: int
    nsa_q: int
    dilated: int
    ffn: int
    ffn_cols: int


def _tiles(seq):
    base = min(4 * LANES, seq)
    tiles = _Tiles(inproj=min(2 * base, seq), key=base, fox_q=base, nsa_q=min(2 * LANES, base),
                   dilated=DIL_DILATIONS[-1] * DIL_SPAN, ffn=base, ffn_cols=2 * LANES)
    assert seq % tiles.dilated == 0 and seq % tiles.inproj == 0, "sequence must be a multiple of 2048"
    assert seq // SEL_BLOCK <= LANES, "selection blocks must fit one lane tile"
    return tiles


def kernel(x, positions, attn_pre_norm, attn_post_norm, ffn_pre_norm, ffn_post_norm, w_in, b_forget, b_nsa_gate, cmp_pos_k, cmp_w1_k, cmp_w2_k, cmp_pos_v, cmp_w1_v, cmp_w2_v, w_out, w_up, conv_w, conv_b, w_down):
    batch, seq, d = x.shape
    depth = w_in.shape[0]
    n = batch * seq
    tiles = _tiles(seq)

    rc, rs1, rs2 = _rope_tables(positions)
    xf = x.reshape(n, d)
    for l in range(depth):
        bias = jnp.concatenate([b_forget[l], b_nsa_gate[l],
                                jnp.zeros((LANES - FOX_HEADS - 3 * NSA_HEADS,), F32)]).reshape(1, LANES)
        fox, nq, nkv, dil, dil4, dil16, aux, ck_t, xk, xv = _inproj(
            xf, attn_pre_norm[l].reshape(1, d), _regroup_w_in(w_in[l]), rc, rs1, rs2, bias,
            batch=batch, seq=seq, tm=tiles.inproj)
        fox_o = _fox(fox, ck_t, batch=batch, seq=seq, tq=tiles.fox_q, tk=tiles.key)

        kcmp, vcmp = _compress(
            xk, xv, _place_w1(cmp_w1_k[l]), _place_w2(cmp_w2_k[l]), _place_pos(cmp_pos_k[l]),
            _place_w1(cmp_w1_v[l]), _place_w2(cmp_w2_v[l]), _place_pos(cmp_pos_v[l]))
        nsa_o = _nsa(nq, nkv, kcmp, vcmp, aux, batch=batch, seq=seq, tq=tiles.nsa_q, tk=tiles.key)

        dil_o = _dilated(dil.reshape(batch, 1, seq, dil.shape[1]), dil4, dil16,
                         batch=batch, seq=seq, tile=tiles.dilated)

        xf = _mix_ffn(xf, fox_o, nsa_o, dil_o, _regroup_w_out(w_out[l]), attn_post_norm[l].reshape(1, d),
                      ffn_pre_norm[l].reshape(1, d), w_up[l].astype(BF16), conv_w[l],
                      conv_b[l].reshape(1, -1), w_down[l].astype(BF16), ffn_post_norm[l].reshape(1, d),
                      batch=batch, seq=seq, tm=tiles.ffn, chunk=tiles.ffn_cols)
    return xf.reshape(batch, seq, d)
```

```python
import functools
import math
from typing import NamedTuple

import numpy as np
import jax
import jax.numpy as jnp
from jax import lax
from jax.experimental import pallas as pl
from jax.experimental.pallas import tpu as pltpu

HEAD_DIM = 64
FOX_HEADS = 4
NSA_HEADS = 8
NSA_GROUPS = 2
NSA_REP = NSA_HEADS // NSA_GROUPS
DIL_HEADS = 4
DIL_DILATIONS = (1, 4, 16)
DIL_SPAN = 128
ROPE_THETA = 500000.0
ROPE_DIM = HEAD_DIM // 4
ROPE_HALF = ROPE_DIM // 2
CMP_BLOCK = 32
CMP_STRIDE = 16
SEL_BLOCK = 64
SEL_TOPK = 16
NSA_WINDOW = 512
CONV_WIDTH = 3
RMS_EPS = 1e-6
NEG_INF = -1e30
FORCE_SCORE = 1e9
ATTN_SCALE = HEAD_DIM ** -0.5
LOG2E = math.log2(math.e)
Q_SCALE = ATTN_SCALE * LOG2E
LANES = 128
VMEM_LIMIT = 56 * 1024 * 1024
FLASH_GROUPS = (3, 1)
NSA_CHAINS = 4
FOX_SPLIT = 1

F32 = jnp.float32
BF16 = jnp.bfloat16


def _dot_nt(a, b):
    return lax.dot_general(a, b, (((1,), (1,)), ((), ())), preferred_element_type=F32)


def _dot(a, b):
    return jnp.dot(a, b, preferred_element_type=F32)


def _rms(x, g):
    return x * lax.rsqrt(jnp.mean(x * x, axis=-1, keepdims=True) + RMS_EPS) * g


def _params(*sem):
    return pltpu.CompilerParams(dimension_semantics=sem, vmem_limit_bytes=VMEM_LIMIT)


def _with_ones(v):
    return jnp.concatenate([v, jnp.ones((v.shape[0], LANES), v.dtype)], axis=1)


def _softmax_pv(s, v1):
    m = jnp.max(s, axis=-1, keepdims=True)
    return m, _dot(jnp.exp2((s - m).astype(BF16)), v1)


def _flash_tiles(n_before, first, score_fn, value_fn):
    state = []
    for s, v1 in zip(score_fn(first, True), value_fn(first)):
        m = jnp.max(s, axis=-1, keepdims=True)
        state.append((m, _dot(jnp.exp2((s - m).astype(BF16)), v1)))

    def step(j, state):
        out = []
        for s, v1, (m, acc) in zip(score_fn(j, False), value_fn(j), state):
            m_new = jnp.maximum(m, jnp.max(s, axis=-1, keepdims=True))
            p = jnp.exp2((s - m_new).astype(BF16))
            out.append((m_new, jnp.exp2(m - m_new) * acc + _dot(p, v1)))
        return tuple(out)

    state = tuple(state)
    done = 0
    for size in FLASH_GROUPS:
        def group(t, state, size=size, done=done):
            for u in range(size):
                state = step(done + t * size + u, state)
            return state

        trips = (n_before - done) // size
        state = lax.fori_loop(0, trips, group, state)
        done = done + trips * size
    return [acc for _, acc in state]


_ROPE_CHUNKS = {"fox": (), "nq": (0, 1, 2, 3), "nkv": (0, 2, 4), "dil": (0, 1, 2, 3)}
_SEG_WIDTH = {"fox": 768, "nq": 512, "nkv": 768, "dil": 768}
_SEG_ORDER = ("fox", "nq", "nkv", "dil")
_AUX_OFFSET = sum(_SEG_WIDTH.values())
_W_IN_COLS = _AUX_OFFSET + 2 * LANES


def _inproj_kernel(x_ref, g_ref, w_ref, rc_ref, rs1_ref, rs2_ref, bias_ref,
                   fox_ref, nq_ref, nkv_ref, dil_ref, dil4_ref, dil16_ref, aux_ref, ck_ref,
                   xk_ref, xv_ref, carry_ref, stage_ref, *, tm):
    @pl.when(pl.program_id(1) == 0)
    def _():
        carry_ref[...] = jnp.zeros_like(carry_ref)

    h = _rms(x_ref[...], g_ref[...]).astype(BF16)
    rc, rs1, rs2 = rc_ref[...], rs1_ref[...], rs2_ref[...]
    outs = {"fox": fox_ref, "nq": nq_ref, "nkv": nkv_ref, "dil": dil_ref}
    col = 0
    for name in _SEG_ORDER:
        for c0 in range(0, _SEG_WIDTH[name], 2 * LANES):
            y = _dot(h, w_ref[:, col + c0:col + c0 + 2 * LANES])
            for half in range(2):
                chunk = c0 // LANES + half
                yc = y[:, half * LANES:(half + 1) * LANES]
                if chunk in _ROPE_CHUNKS[name]:
                    yc = (yc * rc + pltpu.roll(yc, ROPE_HALF, 1) * rs1
                          + pltpu.roll(yc, LANES - ROPE_HALF, 1) * rs2)
                outs[name][:, chunk * LANES:(chunk + 1) * LANES] = yc.astype(BF16)
                if name == "nkv" and chunk < 2:
                    stage_ref[...] = yc
                    for t in range(CMP_STRIDE):
                        part = stage_ref[pl.ds(t, tm // CMP_STRIDE, stride=CMP_STRIDE), :]
                        (xk_ref, xv_ref)[chunk][:, t * LANES:(t + 1) * LANES] = part.astype(BF16)
                if name == "dil":
                    stage_ref[...] = yc
                    for dilation, ref in ((DIL_DILATIONS[1], dil4_ref), (DIL_DILATIONS[2], dil16_ref)):
                        for c in range(dilation):
                            part = stage_ref[pl.ds(c, tm // dilation, stride=dilation), :]
                            ref[c, :, chunk * LANES:(chunk + 1) * LANES] = part.astype(BF16)
        col += _SEG_WIDTH[name]

    ya = _dot(h, w_ref[:, _AUX_OFFSET:_AUX_OFFSET + 2 * LANES])[:, :LANES] + bias_ref[...]
    aux_ref[...] = jax.nn.sigmoid(ya)
    logf = jnp.minimum(ya, 0.0) - jnp.log1p(jnp.exp(-jnp.abs(ya)))
    csum = logf.T[:8, :]
    lane = lax.broadcasted_iota(jnp.int32, csum.shape, 1)
    shift = 1
    while shift < tm:
        csum = csum + jnp.where(lane >= shift, pltpu.roll(csum, shift, 1), 0.0)
        shift *= 2
    csum = csum + jnp.concatenate([carry_ref[...]] * (tm // LANES), axis=1)
    carry_ref[...] = jnp.broadcast_to(csum[:, tm - 1:tm], carry_ref.shape)
    ck_ref[...] = csum * LOG2E


def _inproj(x, g, w, rc, rs1, rs2, bias, *, batch, seq, tm):
    n, d = x.shape
    nt = seq // tm
    tok = lambda width: pl.BlockSpec((tm, width), lambda b, i: (b * nt + i, 0))
    const = lambda shape: pl.BlockSpec(shape, lambda b, i: (0, 0))
    cls = lambda dil: pl.BlockSpec((None, dil, tm // dil, 768), lambda b, i: (b, 0, i, 0))
    cls_shape = lambda dil: jax.ShapeDtypeStruct((batch, dil, seq // dil, 768), BF16)
    d4, d16 = DIL_DILATIONS[1], DIL_DILATIONS[2]
    blk16 = pl.BlockSpec((None, tm // CMP_STRIDE, CMP_STRIDE * LANES), lambda b, i: (b, i, 0))
    blk16_shape = jax.ShapeDtypeStruct((batch, seq // CMP_STRIDE, CMP_STRIDE * LANES), BF16)
    return pl.pallas_call(
        functools.partial(_inproj_kernel, tm=tm),
        grid=(batch, nt),
        in_specs=[tok(d), const((1, d)), const(w.shape), tok(LANES), tok(LANES), tok(LANES),
                  const((1, LANES))],
        out_specs=[tok(768), tok(512), tok(768), tok(768), cls(d4), cls(d16), tok(LANES),
                   pl.BlockSpec((None, 8, tm), lambda b, i: (b, 0, i)), blk16, blk16],
        out_shape=[jax.ShapeDtypeStruct((n, 768), BF16), jax.ShapeDtypeStruct((n, 512), BF16),
                   jax.ShapeDtypeStruct((n, 768), BF16), jax.ShapeDtypeStruct((n, 768), BF16),
                   cls_shape(d4), cls_shape(d16), jax.ShapeDtypeStruct((n, LANES), F32),
                   jax.ShapeDtypeStruct((batch, 8, seq), F32), blk16_shape, blk16_shape],
        scratch_shapes=[pltpu.VMEM((8, LANES), F32), pltpu.VMEM((tm, LANES), F32)],
        compiler_params=_params("arbitrary", "arbitrary"),
        name="inproj",
    )(x, g, w, rc, rs1, rs2, bias)


def _stack_heads(q, heads):
    lane = lax.broadcasted_iota(jnp.int32, (1, q.shape[1]), 1)
    blocks = []
    for h in range(heads):
        head = (lane >= h * HEAD_DIM) & (lane < (h + 1) * HEAD_DIM)
        blocks.append(jnp.where(head, q, jnp.zeros_like(q)))
    return jnp.concatenate(blocks, axis=0)


def _pair_values(v):
    return _with_ones(v[:, :LANES]), _with_ones(v[:, LANES:])


def _pair_pv(p, v_pair):
    half = p.shape[0] // 2
    return jnp.concatenate([_dot(p[:half], v_pair[0]), _dot(p[half:], v_pair[1])], axis=0)


def _unstack_pairs(x, tq):
    return jnp.concatenate(_unstack_halves(x, tq), axis=1)


def _unstack_halves(x, tq):
    lane = lax.broadcasted_iota(jnp.int32, (1, LANES), 1)
    return [jnp.where(lane < HEAD_DIM, x[(2 * c) * tq:(2 * c + 1) * tq],
                      x[(2 * c + 1) * tq:(2 * c + 2) * tq]) for c in range(2)]


def _fox_kernel(q_ref, k_ref, v_ref, ck_ref, o_ref, *, tq, tk):
    i = pl.program_id(1)
    start = i * tq
    rows = FOX_HEADS * tq
    qst = _stack_heads(q_ref[...], FOX_HEADS)
    jd = start // tk
    qpos = start + lax.broadcasted_iota(jnp.int32, (tq, 1), 0)
    kpos = jd * tk + lax.broadcasted_iota(jnp.int32, (1, tk), 1)
    diag_bias = jnp.where(kpos <= qpos, 0.0, NEG_INF).astype(F32)

    crow = tq // FOX_SPLIT

    def scores(j, masked):
        ks0 = pl.multiple_of(j * tk, tk)
        k = k_ref[pl.ds(ks0, tk), :]
        out = []
        for h in range(FOX_HEADS):
            for r0 in range(h * tq, (h + 1) * tq, crow):
                s = _dot_nt(qst[r0:r0 + crow], k) - ck_ref[h:h + 1, pl.ds(ks0, tk)]
                if masked:
                    s = s + diag_bias[r0 - h * tq:r0 - h * tq + crow]
                out.append(s)
        return out

    def values(j):
        v_pair = _pair_values(v_ref[pl.ds(pl.multiple_of(j * tk, tk), tk), :])
        return [v_pair[h // 2] for h in range(FOX_HEADS) for _ in range(FOX_SPLIT)]

    acc = jnp.concatenate(_flash_tiles(jd, jd, scores, values), axis=0)
    o_ref[...] = _unstack_pairs(acc[:, :LANES] / acc[:, LANES:], tq).astype(o_ref.dtype)


def _fox(fox, ck_t, *, batch, seq, tq, tk):
    n = fox.shape[0]
    nt = seq // tq
    width = FOX_HEADS * HEAD_DIM
    return pl.pallas_call(
        functools.partial(_fox_kernel, tq=tq, tk=tk),
        grid=(batch, nt),
        in_specs=[pl.BlockSpec((tq, width), lambda b, i: (b * nt + i, 0)),
                  pl.BlockSpec((seq, width), lambda b, i: (b, 1)),
                  pl.BlockSpec((seq, width), lambda b, i: (b, 2)),
                  pl.BlockSpec((None, 8, seq), lambda b, i: (b, 0, 0))],
        out_specs=pl.BlockSpec((tq, width), lambda b, i: (b * nt + i, 0)),
        out_shape=jax.ShapeDtypeStruct((n, width), BF16),
        compiler_params=_params("arbitrary", "arbitrary"),
        name="fox",
    )(fox, fox, fox, ck_t)


def _gelu_tanh(x):
    return 0.5 * x * (1.0 + jnp.tanh(math.sqrt(2.0 / math.pi) * (x + 0.044715 * (x * x * x))))


def _compress_kernel(xk_ref, xv_ref, w1k_ref, w2k_ref, pk_ref, w1v_ref, w2v_ref, pv_ref,
                     kc_ref, vc_ref):
    feat = CMP_STRIDE * LANES

    def run(x_ref, w1_ref, w2_ref, p_ref, o_ref):
        x = x_ref[...]
        n16 = x.shape[0]
        p = p_ref[...].astype(BF16)
        posb = (_dot(p[:, :feat], w1_ref[0, 0]) + _dot(p[:, feat:], w1_ref[0, 1]))[0:1, :]
        out = None
        for g in range(NSA_GROUPS):
            hid = _dot(x, w1_ref[g, 0]) + pltpu.roll(_dot(x, w1_ref[g, 1]), n16 - 1, 0) + posb
            a = _gelu_tanh(hid).astype(BF16)
            og = _dot(a, w2_ref[g])
            out = og if out is None else out + og
        o_ref[...] = out.astype(o_ref.dtype)

    run(xk_ref, w1k_ref, w2k_ref, pk_ref, kc_ref)
    run(xv_ref, w1v_ref, w2v_ref, pv_ref, vc_ref)


def _compress(xk, xv, w1k, w2k, pk, w1v, w2v, pv):
    batch, n16, feat = xk.shape
    xspec = pl.BlockSpec((None, n16, feat), lambda b: (b, 0, 0))
    full = lambda a: pl.BlockSpec(a.shape, lambda b: (0,) * a.ndim)
    ospec = pl.BlockSpec((None, n16, LANES), lambda b: (b, 0, 0))
    return pl.pallas_call(
        _compress_kernel,
        grid=(batch,),
        in_specs=[xspec, xspec, full(w1k), full(w2k), full(pk), full(w1v), full(w2v), full(pv)],
        out_specs=[ospec, ospec],
        out_shape=[jax.ShapeDtypeStruct((batch, n16, LANES), BF16)] * 2,
        compiler_params=_params("arbitrary"),
        name="nsa_compress",
    )(xk, xv, w1k, w2k, pk, w1v, w2v, pv)


def _nsa_kernel(q_ref, kcmp_ref, vcmp_ref, ks_ref, vs_ref, kw_ref, vw_ref, aux_ref, onehot_ref,
                o_ref, *, tq, tk, seq, chains):
    i = pl.program_id(1)
    start = i * tq
    n_cmp = kcmp_ref.shape[0]
    wspan = NSA_WINDOW + tq

    rows = NSA_HEADS * tq
    crow = rows // chains
    n_sel = seq // SEL_BLOCK

    lane1 = lax.broadcasted_iota(jnp.int32, (1, LANES), 1)
    qpos_col = start + (lax.broadcasted_iota(jnp.int32, (rows, 1), 0) & (tq - 1))
    qpos_row = start + (lax.broadcasted_iota(jnp.int32, (1, NSA_GROUPS * tq), 1) & (tq - 1))

    kcmp = kcmp_ref[...]
    vcmp = vcmp_ref[...]
    cmp_end = lax.broadcasted_iota(jnp.int32, (1, n_cmp), 1) * CMP_STRIDE + (CMP_BLOCK - 1)
    cmp_bias = jnp.where(cmp_end <= qpos_col[:tq], 0.0, NEG_INF).astype(F32)
    cs = lax.broadcasted_iota(jnp.int32, (n_cmp, LANES), 0) * CMP_STRIDE
    ss = lax.broadcasted_iota(jnp.int32, (n_cmp, LANES), 1) * SEL_BLOCK
    overlap = jnp.where((cs < ss + SEL_BLOCK) & (cs + CMP_BLOCK > ss), 1.0, 0.0).astype(F32)

    blk = lax.broadcasted_iota(jnp.int32, (LANES, NSA_GROUPS * tq), 0)
    cur = qpos_row // SEL_BLOCK
    forced = (blk == 0) | (blk == cur) | (blk == cur - 1)
    future = blk * SEL_BLOCK > qpos_row

    wstart = pl.multiple_of(jnp.maximum(start - NSA_WINDOW, 0), tq)
    kwin = kw_ref[pl.ds(wstart, wspan), :]
    vwin = vw_ref[pl.ds(wstart, wspan), :]
    kpos_w = wstart + lax.broadcasted_iota(jnp.int32, (1, wspan), 1)
    win_ok = (kpos_w <= qpos_col) & (kpos_w > qpos_col - NSA_WINDOW)

    jd = start // tk
    col_tk = jd * tk + lax.broadcasted_iota(jnp.int32, (1, tk), 1)
    diag_bias = jnp.concatenate(
        [jnp.where(col_tk <= qpos_col[:tq], 0.0, NEG_INF).astype(F32)] * (crow // tq), axis=0)

    q_blocks = []
    for g in range(NSA_GROUPS):
        grp = (lane1 >= g * HEAD_DIM) & (lane1 < (g + 1) * HEAD_DIM)
        for r in range(NSA_REP):
            qc = q_ref[:, r * LANES:(r + 1) * LANES]
            q_blocks.append(jnp.where(grp, qc, jnp.zeros_like(qc)))
    qst = jnp.concatenate(q_blocks, axis=0)

    s = _dot_nt(qst, kcmp) + jnp.concatenate([cmp_bias] * NSA_HEADS, axis=0)
    m = jnp.max(s, axis=-1, keepdims=True)
    e = jnp.exp2(s - m)
    has_key = qpos_col >= CMP_BLOCK - 1
    p = e * jnp.where(has_key, 1.0 / jnp.sum(e, axis=-1, keepdims=True), 0.0)
    o_cmp = _dot(p.astype(BF16), vcmp)

    p_sum = []
    for g in range(NSA_GROUPS):
        blocks = [p[(g * NSA_REP + r) * tq:(g * NSA_REP + r + 1) * tq] for r in range(NSA_REP)]
        p_sum.append((blocks[0] + blocks[1]) + (blocks[2] + blocks[3]))
    imp = jnp.dot(jnp.concatenate(p_sum, axis=0), overlap, preferred_element_type=F32,
                  precision=lax.Precision.HIGHEST)
    imp_t = jnp.concatenate([imp[g * tq:(g + 1) * tq].T for g in range(NSA_GROUPS)], axis=1)
    imp_t = jnp.where(future, NEG_INF, jnp.where(forced, FORCE_SCORE, imp_t))
    sub = 8
    vals = [imp_t[r0:r0 + sub, :] for r0 in range(0, n_sel, sub)]
    row8 = lax.broadcasted_iota(jnp.int32, vals[0].shape, 0)
    ranks = [jnp.zeros(v.shape, jnp.int32) for v in vals]
    vwin1 = _with_ones(vwin)
    accs = []
    for c in range(chains):
        s = jnp.where(win_ok[:crow], _dot_nt(qst[c * crow:(c + 1) * crow], kwin), NEG_INF)
        accs.append(_softmax_pv(s, vwin1)[1])
        for jp in range(c * n_sel // chains, (c + 1) * n_sel // chains):
            other = imp_t[jp:jp + 1, :]
            for t, v in enumerate(vals):
                if t > jp // sub:
                    beats = other >= v
                elif t < jp // sub:
                    beats = other > v
                else:
                    beats = (other > v) | ((other == v) & (row8 > jp % sub))
                ranks[t] = ranks[t] + jnp.where(beats, 1, 0)
    rank = jnp.concatenate(ranks, axis=0)
    acc = jnp.concatenate(accs, axis=0)
    o_win = acc[:, :LANES] / acc[:, LANES:]

    aux = aux_ref[...]
    gate = lambda head, branch: aux[:, FOX_HEADS + 3 * head + branch:FOX_HEADS + 3 * head + branch + 1]
    partial = [gate(head, 0) * o_cmp[head * tq:(head + 1) * tq]
               + gate(head, 2) * o_win[head * tq:(head + 1) * tq] for head in range(NSA_HEADS)]

    sb_blocks = []
    for g in range(NSA_GROUPS):
        chosen = rank[:, g * tq:(g + 1) * tq] < min(SEL_TOPK, n_sel)
        sb_t = jnp.concatenate([jnp.where(chosen, 0.0, NEG_INF).astype(F32),
                                jnp.full((LANES - n_sel, tq), NEG_INF, F32)], axis=0)
        sb_blocks.extend([sb_t.T.astype(BF16)] * NSA_REP)
    qa = jnp.concatenate([qst, jnp.concatenate(sb_blocks, axis=0)], axis=1)

    def sel_scores(j, masked):
        ks0 = pl.multiple_of(j * tk, tk)
        ka = jnp.concatenate([ks_ref[pl.ds(ks0, tk), :], onehot_ref[pl.ds(ks0, tk), :]], axis=1)
        scores = []
        for c in range(chains):
            s = _dot_nt(qa[c * crow:(c + 1) * crow], ka)
            if masked:
                s = s + diag_bias
            scores.append(s)
        return scores

    def sel_values(j):
        return [_with_ones(vs_ref[pl.ds(pl.multiple_of(j * tk, tk), tk), :])] * chains

    acc = jnp.concatenate(_flash_tiles(jd, jd, sel_scores, sel_values), axis=0)
    o_sel = acc[:, :LANES] / acc[:, LANES:]

    for r in range(NSA_REP):
        per_group = []
        for g in range(NSA_GROUPS):
            head = g * NSA_REP + r
            per_group.append(partial[head] + gate(head, 1) * o_sel[head * tq:(head + 1) * tq])
        o = jnp.where(lane1 < HEAD_DIM, per_group[0], per_group[1])
        o_ref[:, r * LANES:(r + 1) * LANES] = o.astype(o_ref.dtype)


def _nsa(nq, nkv, kcmp, vcmp, aux, *, batch, seq, tq, tk):
    n = nq.shape[0]
    onehot = (jnp.arange(seq)[:, None] // SEL_BLOCK == jnp.arange(LANES)[None, :]).astype(BF16)
    nt = seq // tq
    n16 = kcmp.shape[1]
    kv = lambda c: pl.BlockSpec((seq, LANES), lambda b, i: (b, c))
    cmp_spec = pl.BlockSpec((None, n16, LANES), lambda b, i: (b, 0, 0))
    width = NSA_HEADS * HEAD_DIM
    return pl.pallas_call(
        functools.partial(_nsa_kernel, tq=tq, tk=tk, seq=seq, chains=NSA_CHAINS),
        grid=(batch, nt),
        in_specs=[pl.BlockSpec((tq, width), lambda b, i: (b * nt + i, 0)),
                  cmp_spec, cmp_spec, kv(2), kv(3), kv(4), kv(5),
                  pl.BlockSpec((tq, LANES), lambda b, i: (b * nt + i, 0)),
                  pl.BlockSpec((seq, LANES), lambda b, i: (0, 0))],
        out_specs=pl.BlockSpec((tq, width), lambda b, i: (b * nt + i, 0)),
        out_shape=jax.ShapeDtypeStruct((n, width), BF16),
        compiler_params=_params("arbitrary", "arbitrary"),
        name="nsa",
    )(nq, kcmp, vcmp, nkv, nkv, nkv, nkv, aux, onehot)


def _dilated_kernel(q1_ref, k1_ref, v1_ref, q4_ref, k4_ref, v4_ref, q16_ref, k16_ref, v16_ref,
                    o_ref, osc_ref, lsc_ref, *, tile, seq):
    i = pl.program_id(1)
    tq = DIL_SPAN
    rows = DIL_HEADS * tq
    row_u = lax.broadcasted_iota(jnp.int32, (rows, 1), 0) & (tq - 1)
    refs = ((q1_ref, k1_ref, v1_ref), (q4_ref, k4_ref, v4_ref), (q16_ref, k16_ref, v16_ref))
    for pat, (dilation, (q_ref, k_ref, v_ref)) in enumerate(zip(DIL_DILATIONS, refs)):
        length = seq // dilation
        span = min(2 * tq, length)
        per_class = tile // (tq * dilation)
        n_sub = tile // tq
        base = i * (tile // dilation)
        rel = row_u - lax.broadcasted_iota(jnp.int32, (1, span), 1)

        def band(offset):
            return jnp.where((rel + offset >= 0) & (rel + offset <= DIL_SPAN), 0.0, NEG_INF).astype(F32)

        bias_shifted, bias_aligned = band(min(DIL_SPAN, length - span + DIL_SPAN)), band(0)

        def sub_tile(sub):
            c = sub // per_class
            w = sub % per_class
            u0 = w * tq
            ks0 = pl.multiple_of(jnp.maximum(base + u0 - DIL_SPAN, 0), tq)
            qst = _stack_heads(q_ref[c, pl.ds(u0, tq), :], DIL_HEADS)
            bias = jnp.where(base == 0, bias_aligned, bias_shifted) if u0 == 0 else bias_shifted
            s = _dot_nt(qst, k_ref[c, pl.ds(ks0, span), :]) + bias
            m = jnp.max(s, axis=-1, keepdims=True)
            acc = _pair_pv(jnp.exp2((s - m).astype(BF16)),
                           _pair_values(v_ref[c, pl.ds(ks0, span), :]))
            den = acc[:, LANES:]
            dst = pl.ds(u0 * dilation + c, tq, stride=dilation)
            for half, (o, lse) in enumerate(zip(_unstack_halves(acc[:, :LANES] / den, tq),
                                                _unstack_halves(m + jnp.log2(den), tq))):
                osc_ref[pat, half, dst, :] = o
                lsc_ref[pat, half, dst, :] = lse

        for sub in range(n_sub):
            sub_tile(sub)

    chunk = 2 * tq
    for r0 in range(0, tile, chunk):
        for half in range(2):
            ls = [lsc_ref[p, half, r0:r0 + chunk, :] for p in range(3)]
            top = jnp.maximum(jnp.maximum(ls[0], ls[1]), ls[2])
            es = [jnp.exp2(l - top) for l in ls]
            num = (es[0] * osc_ref[0, half, r0:r0 + chunk, :] + es[1] * osc_ref[1, half, r0:r0 + chunk, :]
                   + es[2] * osc_ref[2, half, r0:r0 + chunk, :])
            o_ref[r0:r0 + chunk, half * LANES:(half + 1) * LANES] = (
                num / (es[0] + es[1] + es[2])).astype(o_ref.dtype)


def _dilated(dil1, dil4, dil16, *, batch, seq, tile):
    width = DIL_HEADS * HEAD_DIM
    nt = seq // tile
    specs = []
    for arr in (dil1, dil4, dil16):
        dilation, length = arr.shape[1], arr.shape[2]
        specs.append(pl.BlockSpec((None, dilation, tile // dilation, width), lambda b, i: (b, 0, i, 0)))
        for part in (1, 2):
            specs.append(pl.BlockSpec((None, dilation, length, width),
                                      lambda b, i, part=part: (b, 0, 0, part)))
    return pl.pallas_call(
        functools.partial(_dilated_kernel, tile=tile, seq=seq),
        grid=(batch, nt),
        in_specs=specs,
        out_specs=pl.BlockSpec((tile, width), lambda b, i: (b * nt + i, 0)),
        out_shape=jax.ShapeDtypeStruct((batch * seq, width), BF16),
        scratch_shapes=[pltpu.VMEM((3, 2, tile, LANES), F32), pltpu.VMEM((3, 2, tile, LANES), F32)],
        compiler_params=_params("arbitrary", "arbitrary"),
        name="dilated",
    )(dil1, dil1, dil1, dil4, dil4, dil4, dil16, dil16, dil16)


def _mix_ffn_kernel(x_ref, fox_ref, nsa_ref, dil_ref, wout_ref, gmix_ref,
                    gpre_ref, wup_ref, cw_ref, cb_ref, wdn_ref, gpost_ref, o_ref,
                    tail_ref, ubuf_ref, act_ref, *, tm, d_ff, chunk):
    @pl.when(pl.program_id(1) == 0)
    def _():
        tail_ref[...] = jnp.zeros_like(tail_ref)

    fw = FOX_HEADS * HEAD_DIM
    nw = NSA_HEADS * HEAD_DIM
    mix = (_dot(fox_ref[...], wout_ref[:fw, :]) + _dot(nsa_ref[...], wout_ref[fw:fw + nw, :])
           + _dot(dil_ref[...], wout_ref[fw + nw:, :]))
    x = x_ref[...] + _rms(mix, gmix_ref[...])
    h = _rms(x, gpre_ref[...]).astype(BF16)
    halo = tail_ref.shape[0]

    def conv(slot, c0):
        u = _dot(h, wup_ref[:, c0:c0 + chunk])
        ubuf_ref[slot, 0:halo, :] = tail_ref[:, c0:c0 + chunk]
        ubuf_ref[slot, halo:halo + tm, :] = u
        tail_ref[:, c0:c0 + chunk] = u[tm - halo:, :]
        u1 = ubuf_ref[slot, halo - 1:halo - 1 + tm, :]
        u2 = ubuf_ref[slot, halo - 2:halo - 2 + tm, :]
        return (cb_ref[:, c0:c0 + chunk] + u2 * cw_ref[0:1, c0:c0 + chunk]
                + u1 * cw_ref[1:2, c0:c0 + chunk] + u * cw_ref[2:3, c0:c0 + chunk])

    for c0 in range(0, d_ff, chunk):
        a = conv(0, c0)
        b = conv(1, d_ff + c0)
        act_ref[:, c0:c0 + chunk] = (a * jax.nn.sigmoid(a) * b).astype(BF16)
    y = _dot(act_ref[...], wdn_ref[...])
    o_ref[...] = x + _rms(y, gpost_ref[...])


def _mix_ffn(x, fox_o, nsa_o, dil_o, wout, gmix, gpre, wup, cw, cb, wdn, gpost, *, batch, seq, tm, chunk):
    n, d = x.shape
    d_ff = wdn.shape[0]
    nt = seq // tm
    tok = lambda width: pl.BlockSpec((tm, width), lambda b, i: (b * nt + i, 0))
    const = lambda a: pl.BlockSpec(a.shape, lambda b, i: (0, 0), pipeline_mode=pl.Buffered(1))
    return pl.pallas_call(
        functools.partial(_mix_ffn_kernel, tm=tm, d_ff=d_ff, chunk=chunk),
        grid=(batch, nt),
        in_specs=[tok(d), tok(fox_o.shape[1]), tok(nsa_o.shape[1]), tok(dil_o.shape[1]),
                  const(wout), const(gmix), const(gpre), const(wup), const(cw), const(cb),
                  const(wdn), const(gpost)],
        out_specs=tok(d),
        out_shape=jax.ShapeDtypeStruct((n, d), F32),
        scratch_shapes=[pltpu.VMEM((8, 2 * d_ff), F32), pltpu.VMEM((2, 8 + tm, chunk), F32),
                        pltpu.VMEM((tm, d_ff), BF16)],
        compiler_params=_params("arbitrary", "arbitrary"),
        name="mix_ffn",
    )(x, fox_o, nsa_o, dil_o, wout, gmix, gpre, wup, cw, cb, wdn, gpost)


def _rope_tables(positions):
    inv_freq = ROPE_THETA ** (-2.0 * jnp.arange(ROPE_HALF, dtype=F32) / ROPE_DIM)
    ang = positions.astype(F32).reshape(-1, 1) * inv_freq
    cos, sin = jnp.cos(ang), jnp.sin(ang)
    n = ang.shape[0]
    ones = jnp.ones((n, HEAD_DIM - ROPE_DIM), F32)
    zeros_h = jnp.zeros((n, ROPE_HALF), F32)
    zeros_r = jnp.zeros((n, HEAD_DIM - ROPE_DIM), F32)
    rc = jnp.concatenate([cos, cos, ones], axis=1)
    rs1 = jnp.concatenate([zeros_h, sin, zeros_r], axis=1)
    rs2 = jnp.concatenate([-sin, zeros_h, zeros_r], axis=1)
    tile2 = lambda t: jnp.concatenate([t, t], axis=1)
    return tile2(rc), tile2(rs1), tile2(rs2)


def _nsa_head_perm():
    cols = []
    for r in range(NSA_REP):
        for g in range(NSA_GROUPS):
            h = g * NSA_REP + r
            cols.extend(range(h * HEAD_DIM, (h + 1) * HEAD_DIM))
    return np.asarray(cols)


def _regroup_w_in(w):
    fw, nw, kvw, dw = FOX_HEADS * HEAD_DIM, NSA_HEADS * HEAD_DIM, NSA_GROUPS * HEAD_DIM, DIL_HEADS * HEAD_DIM
    sizes = [fw, fw, fw, FOX_HEADS, nw] + [kvw] * 6 + [3 * NSA_HEADS, dw, dw, dw]
    offs = np.concatenate([[0], np.cumsum(sizes)])
    part = lambda idx: w[:, offs[idx]:offs[idx + 1]]
    fq, fk, fv, ff, nq = (part(t) for t in range(5))
    kvs = [part(t) for t in range(5, 11)]
    ng, dq, dk, dv = (part(t) for t in range(11, 15))
    nq = nq[:, _nsa_head_perm()]
    pad = jnp.zeros((w.shape[0], 2 * LANES - FOX_HEADS - 3 * NSA_HEADS), w.dtype)
    cols = [fq * Q_SCALE, fk, fv, nq * Q_SCALE] + kvs + [dq * Q_SCALE, dk, dv, ff, ng, pad]
    return jnp.concatenate(cols, axis=1).astype(BF16)


def _regroup_w_out(w):
    fw, nw = FOX_HEADS * HEAD_DIM, NSA_HEADS * HEAD_DIM
    return jnp.concatenate([w[:fw], w[fw:fw + nw][_nsa_head_perm()], w[fw + nw:]], axis=0).astype(BF16)


def _place_w1(w1):
    hidden = w1.shape[1]
    w = w1.reshape(CMP_BLOCK // CMP_STRIDE, CMP_STRIDE, HEAD_DIM, hidden)
    z = jnp.zeros_like(w)
    per_group = [jnp.concatenate([w, z], axis=2), jnp.concatenate([z, w], axis=2)]
    return jnp.stack(per_group).reshape(NSA_GROUPS, CMP_BLOCK // CMP_STRIDE, CMP_STRIDE * LANES,
                                        hidden).astype(BF16)


def _place_pos(p):
    z = jnp.zeros_like(p)
    row = jnp.concatenate([p, z], axis=1).reshape(1, -1)
    return jnp.concatenate([row, jnp.zeros((7, row.shape[1]), row.dtype)], axis=0)


def _place_w2(w2):
    z = jnp.zeros_like(w2)
    return jnp.stack([jnp.concatenate([w2, z], axis=1), jnp.concatenate([z, w2], axis=1)]).astype(BF16)


class _Tiles(NamedTuple):
    inproj: int
    key: int
    fox_q: int
    nsa_q: int
    dilated: int
    ffn: int
    ffn_cols: int


def _tiles(seq):
    base = min(4 * LANES, seq)
    tiles = _Tiles(inproj=min(2 * base, seq), key=base, fox_q=base, nsa_q=min(2 * LANES, base),
                   dilated=DIL_DILATIONS[-1] * DIL_SPAN, ffn=base, ffn_cols=2 * LANES)
    assert seq % tiles.dilated == 0 and seq % tiles.inproj == 0, "sequence must be a multiple of 2048"
    assert seq // SEL_BLOCK <= LANES, "selection blocks must fit one lane tile"
    return tiles


def kernel(x, positions, attn_pre_norm, attn_post_norm, ffn_pre_norm, ffn_post_norm, w_in, b_forget, b_nsa_gate, cmp_pos_k, cmp_w1_k, cmp_w2_k, cmp_pos_v, cmp_w1_v, cmp_w2_v, w_out, w_up, conv_w, conv_b, w_down):
    batch, seq, d = x.shape
    depth = w_in.shape[0]
    n = batch * seq
    tiles = _tiles(seq)

    rc, rs1, rs2 = _rope_tables(positions)
    xf = x.reshape(n, d)
    for l in range(depth):
        bias = jnp.concatenate([b_forget[l], b_nsa_gate[l],
                                jnp.zeros((LANES - FOX_HEADS - 3 * NSA_HEADS,), F32)]).reshape(1, LANES)
        fox, nq, nkv, dil, dil4, dil16, aux, ck_t, xk, xv = _inproj(
            xf, attn_pre_norm[l].reshape(1, d), _regroup_w_in(w_in[l]), rc, rs1, rs2, bias,
            batch=batch, seq=seq, tm=tiles.inproj)
        fox_o = _fox(fox, ck_t, batch=batch, seq=seq, tq=tiles.fox_q, tk=tiles.key)

        kcmp, vcmp = _compress(
            xk, xv, _place_w1(cmp_w1_k[l]), _place_w2(cmp_w2_k[l]), _place_pos(cmp_pos_k[l]),
            _place_w1(cmp_w1_v[l]), _place_w2(cmp_w2_v[l]), _place_pos(cmp_pos_v[l]))
        nsa_o = _nsa(nq, nkv, kcmp, vcmp, aux, batch=batch, seq=seq, tq=tiles.nsa_q, tk=tiles.key)

        dil_o = _dilated(dil.reshape(batch, 1, seq, dil.shape[1]), dil4, dil16,
                         batch=batch, seq=seq, tile=tiles.dilated)

        xf = _mix_ffn(xf, fox_o, nsa_o, dil_o, _regroup_w_out(w_out[l]), attn_post_norm[l].reshape(1, d),
                      ffn_pre_norm[l].reshape(1, d), w_up[l].astype(BF16), conv_w[l],
                      conv_b[l].reshape(1, -1), w_down[l].astype(BF16), ffn_post_norm[l].reshape(1, d),
                      batch=batch, seq=seq, tm=tiles.ffn, chunk=tiles.ffn_cols)
    return xf.reshape(batch, seq, d)
```

```python
import functools
import math
from typing import NamedTuple

import numpy as np
import jax
import jax.numpy as jnp
from jax import lax
from jax.experimental import pallas as pl
from jax.experimental.pallas import tpu as pltpu

HEAD_DIM = 64
FOX_HEADS = 4
NSA_HEADS = 8
NSA_GROUPS = 2
NSA_REP = NSA_HEADS // NSA_GROUPS
DIL_HEADS = 4
DIL_DILATIONS = (1, 4, 16)
DIL_SPAN = 128
ROPE_THETA = 500000.0
ROPE_DIM = HEAD_DIM // 4
ROPE_HALF = ROPE_DIM // 2
CMP_BLOCK = 32
CMP_STRIDE = 16
SEL_BLOCK = 64
SEL_TOPK = 16
NSA_WINDOW = 512
CONV_WIDTH = 3
RMS_EPS = 1e-6
NEG_INF = -1e30
FORCE_SCORE = 1e9
ATTN_SCALE = HEAD_DIM ** -0.5
LOG2E = math.log2(math.e)
Q_SCALE = ATTN_SCALE * LOG2E
LANES = 128
VMEM_LIMIT = 56 * 1024 * 1024
FLASH_GROUPS = (3, 1)
NSA_CHAINS = 4
FOX_SPLIT = 1

F32 = jnp.float32
BF16 = jnp.bfloat16


def _dot_nt(a, b):
    return lax.dot_general(a, b, (((1,), (1,)), ((), ())), preferred_element_type=F32)


def _dot(a, b):
    return jnp.dot(a, b, preferred_element_type=F32)


def _rms(x, g):
    return x * lax.rsqrt(jnp.mean(x * x, axis=-1, keepdims=True) + RMS_EPS) * g


def _params(*sem):
    return pltpu.CompilerParams(dimension_semantics=sem, vmem_limit_bytes=VMEM_LIMIT)


def _with_ones(v):
    return jnp.concatenate([v, jnp.ones((v.shape[0], LANES), v.dtype)], axis=1)


def _softmax_pv(s, v1):
    m = jnp.max(s, axis=-1, keepdims=True)
    return m, _dot(jnp.exp2((s - m).astype(BF16)), v1)


def _flash_tiles(n_before, first, score_fn, value_fn):
    state = []
    for s, v1 in zip(score_fn(first, True), value_fn(first)):
        m = jnp.max(s, axis=-1, keepdims=True)
        state.append((m, _dot(jnp.exp2((s - m).astype(BF16)), v1)))

    def step(j, state):
        out = []
        for s, v1, (m, acc) in zip(score_fn(j, False), value_fn(j), state):
            m_new = jnp.maximum(m, jnp.max(s, axis=-1, keepdims=True))
            p = jnp.exp2((s - m_new).astype(BF16))
            out.append((m_new, jnp.exp2(m - m_new) * acc + _dot(p, v1)))
        return tuple(out)

    state = tuple(state)
    done = 0
    for size in FLASH_GROUPS:
        def group(t, state, size=size, done=done):
            for u in range(size):
                state = step(done + t * size + u, state)
            return state

        trips = (n_before - done) // size
        state = lax.fori_loop(0, trips, group, state)
        done = done + trips * size
    return [acc for _, acc in state]


_ROPE_CHUNKS = {"fox": (), "nq": (0, 1, 2, 3), "nkv": (0, 2, 4), "dil": (0, 1, 2, 3)}
_SEG_WIDTH = {"fox": 768, "nq": 512, "nkv": 768, "dil": 768}
_SEG_ORDER = ("fox", "nq", "nkv", "dil")
_AUX_OFFSET = sum(_SEG_WIDTH.values())
_W_IN_COLS = _AUX_OFFSET + 2 * LANES


def _inproj_kernel(x_ref, g_ref, w_ref, rc_ref, rs1_ref, rs2_ref, bias_ref,
                   fox_ref, nq_ref, nkv_ref, dil_ref, dil4_ref, dil16_ref, aux_ref, ck_ref,
                   xk_ref, xv_ref, carry_ref, stage_ref, *, tm):
    @pl.when(pl.program_id(1) == 0)
    def _():
        carry_ref[...] = jnp.zeros_like(carry_ref)

    h = _rms(x_ref[...], g_ref[...]).astype(BF16)
    rc, rs1, rs2 = rc_ref[...], rs1_ref[...], rs2_ref[...]
    outs = {"fox": fox_ref, "nq": nq_ref, "nkv": nkv_ref, "dil": dil_ref}
    col = 0
    for name in _SEG_ORDER:
        for c0 in range(0, _SEG_WIDTH[name], 2 * LANES):
            y = _dot(h, w_ref[:, col + c0:col + c0 + 2 * LANES])
            for half in range(2):
                chunk = c0 // LANES + half
                yc = y[:, half * LANES:(half + 1) * LANES]
                if chunk in _ROPE_CHUNKS[name]:
                    yc = (yc * rc + pltpu.roll(yc, ROPE_HALF, 1) * rs1
                          + pltpu.roll(yc, LANES - ROPE_HALF, 1) * rs2)
                outs[name][:, chunk * LANES:(chunk + 1) * LANES] = yc.astype(BF16)
                if name == "nkv" and chunk < 2:
                    stage_ref[...] = yc
                    for t in range(CMP_STRIDE):
                        part = stage_ref[pl.ds(t, tm // CMP_STRIDE, stride=CMP_STRIDE), :]
                        (xk_ref, xv_ref)[chunk][:, t * LANES:(t + 1) * LANES] = part.astype(BF16)
                if name == "dil":
                    stage_ref[...] = yc
                    for dilation, ref in ((DIL_DILATIONS[1], dil4_ref), (DIL_DILATIONS[2], dil16_ref)):
                        for c in range(dilation):
                            part = stage_ref[pl.ds(c, tm // dilation, stride=dilation), :]
                            ref[c, :, chunk * LANES:(chunk + 1) * LANES] = part.astype(BF16)
        col += _SEG_WIDTH[name]

    ya = _dot(h, w_ref[:, _AUX_OFFSET:_AUX_OFFSET + 2 * LANES])[:, :LANES] + bias_ref[...]
    aux_ref[...] = jax.nn.sigmoid(ya)
    logf = jnp.minimum(ya, 0.0) - jnp.log1p(jnp.exp(-jnp.abs(ya)))
    csum = logf.T[:8, :]
    lane = lax.broadcasted_iota(jnp.int32, csum.shape, 1)
    shift = 1
    while shift < tm:
        csum = csum + jnp.where(lane >= shift, pltpu.roll(csum, shift, 1), 0.0)
        shift *= 2
    csum = csum + jnp.concatenate([carry_ref[...]] * (tm // LANES), axis=1)
    carry_ref[...] = jnp.broadcast_to(csum[:, tm - 1:tm], carry_ref.shape)
    ck_ref[...] = csum * LOG2E


def _inproj(x, g, w, rc, rs1, rs2, bias, *, batch, seq, tm):
    n, d = x.shape
    nt = seq // tm
    tok = lambda width: pl.BlockSpec((tm, width), lambda b, i: (b * nt + i, 0))
    const = lambda shape: pl.BlockSpec(shape, lambda b, i: (0, 0))
    cls = lambda dil: pl.BlockSpec((None, dil, tm // dil, 768), lambda b, i: (b, 0, i, 0))
    cls_shape = lambda dil: jax.ShapeDtypeStruct((batch, dil, seq // dil, 768), BF16)
    d4, d16 = DIL_DILATIONS[1], DIL_DILATIONS[2]
    blk16 = pl.BlockSpec((None, tm // CMP_STRIDE, CMP_STRIDE * LANES), lambda b, i: (b, i, 0))
    blk16_shape = jax.ShapeDtypeStruct((batch, seq // CMP_STRIDE, CMP_STRIDE * LANES), BF16)
    return pl.pallas_call(
        functools.partial(_inproj_kernel, tm=tm),
        grid=(batch, nt),
        in_specs=[tok(d), const((1, d)), const(w.shape), tok(LANES), tok(LANES), tok(LANES),
                  const((1, LANES))],
        out_specs=[tok(768), tok(512), tok(768), tok(768), cls(d4), cls(d16), tok(LANES),
                   pl.BlockSpec((None, 8, tm), lambda b, i: (b, 0, i)), blk16, blk16],
        out_shape=[jax.ShapeDtypeStruct((n, 768), BF16), jax.ShapeDtypeStruct((n, 512), BF16),
                   jax.ShapeDtypeStruct((n, 768), BF16), jax.ShapeDtypeStruct((n, 768), BF16),
                   cls_shape(d4), cls_shape(d16), jax.ShapeDtypeStruct((n, LANES), F32),
                   jax.ShapeDtypeStruct((batch, 8, seq), F32), blk16_shape, blk16_shape],
        scratch_shapes=[pltpu.VMEM((8, LANES), F32), pltpu.VMEM((tm, LANES), F32)],
        compiler_params=_params("arbitrary", "arbitrary"),
        name="inproj",
    )(x, g, w, rc, rs1, rs2, bias)


def _stack_heads(q, heads):
    lane = lax.broadcasted_iota(jnp.int32, (1, q.shape[1]), 1)
    blocks = []
    for h in range(heads):
        head = (lane >= h * HEAD_DIM) & (lane < (h + 1) * HEAD_DIM)
        blocks.append(jnp.where(head, q, jnp.zeros_like(q)))
    return jnp.concatenate(blocks, axis=0)


def _pair_values(v):
    return _with_ones(v[:, :LANES]), _with_ones(v[:, LANES:])


def _pair_pv(p, v_pair):
    half = p.shape[0] // 2
    return jnp.concatenate([_dot(p[:half], v_pair[0]), _dot(p[half:], v_pair[1])], axis=0)


def _unstack_pairs(x, tq):
    return jnp.concatenate(_unstack_halves(x, tq), axis=1)


def _unstack_halves(x, tq):
    lane = lax.broadcasted_iota(jnp.int32, (1, LANES), 1)
    return [jnp.where(lane < HEAD_DIM, x[(2 * c) * tq:(2 * c + 1) * tq],
                      x[(2 * c + 1) * tq:(2 * c + 2) * tq]) for c in range(2)]


def _fox_kernel(q_ref, k_ref, v_ref, ck_ref, o_ref, *, tq, tk):
    i = pl.program_id(1)
    start = i * tq
    rows = FOX_HEADS * tq
    qst = _stack_heads(q_ref[...], FOX_HEADS)
    jd = start // tk
    qpos = start + lax.broadcasted_iota(jnp.int32, (tq, 1), 0)
    kpos = jd * tk + lax.broadcasted_iota(jnp.int32, (1, tk), 1)
    diag_bias = jnp.where(kpos <= qpos, 0.0, NEG_INF).astype(F32)

    crow = tq // FOX_SPLIT

    def scores(j, masked):
        ks0 = pl.multiple_of(j * tk, tk)
        k = k_ref[pl.ds(ks0, tk), :]
        out = []
        for h in range(FOX_HEADS):
            for r0 in range(h * tq, (h + 1) * tq, crow):
                s = _dot_nt(qst[r0:r0 + crow], k) - ck_ref[h:h + 1, pl.ds(ks0, tk)]
                if masked:
                    s = s + diag_bias[r0 - h * tq:r0 - h * tq + crow]
                out.append(s)
        return out

    def values(j):
        v_pair = _pair_values(v_ref[pl.ds(pl.multiple_of(j * tk, tk), tk), :])
        return [v_pair[h // 2] for h in range(FOX_HEADS) for _ in range(FOX_SPLIT)]

    acc = jnp.concatenate(_flash_tiles(jd, jd, scores, values), axis=0)
    o_ref[...] = _unstack_pairs(acc[:, :LANES] / acc[:, LANES:], tq).astype(o_ref.dtype)


def _fox(fox, ck_t, *, batch, seq, tq, tk):
    n = fox.shape[0]
    nt = seq // tq
    width = FOX_HEADS * HEAD_DIM
    return pl.pallas_call(
        functools.partial(_fox_kernel, tq=tq, tk=tk),
        grid=(batch, nt),
        in_specs=[pl.BlockSpec((tq, width), lambda b, i: (b * nt + i, 0)),
                  pl.BlockSpec((seq, width), lambda b, i: (b, 1)),
                  pl.BlockSpec((seq, width), lambda b, i: (b, 2)),
                  pl.BlockSpec((None, 8, seq), lambda b, i: (b, 0, 0))],
        out_specs=pl.BlockSpec((tq, width), lambda b, i: (b * nt + i, 0)),
        out_shape=jax.ShapeDtypeStruct((n, width), BF16),
        compiler_params=_params("arbitrary", "arbitrary"),
        name="fox",
    )(fox, fox, fox, ck_t)


def _gelu_tanh(x):
    return 0.5 * x * (1.0 + jnp.tanh(math.sqrt(2.0 / math.pi) * (x + 0.044715 * (x * x * x))))


def _compress_kernel(xk_ref, xv_ref, w1k_ref, w2k_ref, pk_ref, w1v_ref, w2v_ref, pv_ref,
                     kc_ref, vc_ref):
    feat = CMP_STRIDE * LANES

    def run(x_ref, w1_ref, w2_ref, p_ref, o_ref):
        x = x_ref[...]
        n16 = x.shape[0]
        p = p_ref[...].astype(BF16)
        posb = (_dot(p[:, :feat], w1_ref[0, 0]) + _dot(p[:, feat:], w1_ref[0, 1]))[0:1, :]
        out = None
        for g in range(NSA_GROUPS):
            hid = _dot(x, w1_ref[g, 0]) + pltpu.roll(_dot(x, w1_ref[g, 1]), n16 - 1, 0) + posb
            a = _gelu_tanh(hid).astype(BF16)
            og = _dot(a, w2_ref[g])
            out = og if out is None else out + og
        o_ref[...] = out.astype(o_ref.dtype)

    run(xk_ref, w1k_ref, w2k_ref, pk_ref, kc_ref)
    run(xv_ref, w1v_ref, w2v_ref, pv_ref, vc_ref)


def _compress(xk, xv, w1k, w2k, pk, w1v, w2v, pv):
    batch, n16, feat = xk.shape
    xspec = pl.BlockSpec((None, n16, feat), lambda b: (b, 0, 0))
    full = lambda a: pl.BlockSpec(a.shape, lambda b: (0,) * a.ndim)
    ospec = pl.BlockSpec((None, n16, LANES), lambda b: (b, 0, 0))
    return pl.pallas_call(
        _compress_kernel,
        grid=(batch,),
        in_specs=[xspec, xspec, full(w1k), full(w2k), full(pk), full(w1v), full(w2v), full(pv)],
        out_specs=[ospec, ospec],
        out_shape=[jax.ShapeDtypeStruct((batch, n16, LANES), BF16)] * 2,
        compiler_params=_params("arbitrary"),
        name="nsa_compress",
    )(xk, xv, w1k, w2k, pk, w1v, w2v, pv)


def _nsa_kernel(q_ref, kcmp_ref, vcmp_ref, ks_ref, vs_ref, kw_ref, vw_ref, aux_ref, onehot_ref,
                o_ref, *, tq, tk, seq, chains):
    i = pl.program_id(1)
    start = i * tq
    n_cmp = kcmp_ref.shape[0]
    wspan = NSA_WINDOW + tq

    rows = NSA_HEADS * tq
    crow = rows // chains
    n_sel = seq // SEL_BLOCK

    lane1 = lax.broadcasted_iota(jnp.int32, (1, LANES), 1)
    qpos_col = start + (lax.broadcasted_iota(jnp.int32, (rows, 1), 0) & (tq - 1))
    qpos_row = start + (lax.broadcasted_iota(jnp.int32, (1, NSA_GROUPS * tq), 1) & (tq - 1))

    kcmp = kcmp_ref[...]
    vcmp = vcmp_ref[...]
    cmp_end = lax.broadcasted_iota(jnp.int32, (1, n_cmp), 1) * CMP_STRIDE + (CMP_BLOCK - 1)
    cmp_bias = jnp.where(cmp_end <= qpos_col[:tq], 0.0, NEG_INF).astype(F32)
    cs = lax.broadcasted_iota(jnp.int32, (n_cmp, LANES), 0) * CMP_STRIDE
    ss = lax.broadcasted_iota(jnp.int32, (n_cmp, LANES), 1) * SEL_BLOCK
    overlap = jnp.where((cs < ss + SEL_BLOCK) & (cs + CMP_BLOCK > ss), 1.0, 0.0).astype(F32)

    blk = lax.broadcasted_iota(jnp.int32, (LANES, NSA_GROUPS * tq), 0)
    cur = qpos_row // SEL_BLOCK
    forced = (blk == 0) | (blk == cur) | (blk == cur - 1)
    future = blk * SEL_BLOCK > qpos_row

    wstart = pl.multiple_of(jnp.maximum(start - NSA_WINDOW, 0), tq)
    kwin = kw_ref[pl.ds(wstart, wspan), :]
    vwin = vw_ref[pl.ds(wstart, wspan), :]
    kpos_w = wstart + lax.broadcasted_iota(jnp.int32, (1, wspan), 1)
    win_ok = (kpos_w <= qpos_col) & (kpos_w > qpos_col - NSA_WINDOW)

    jd = start // tk
    col_tk = jd * tk + lax.broadcasted_iota(jnp.int32, (1, tk), 1)
    diag_bias = jnp.concatenate(
        [jnp.where(col_tk <= qpos_col[:tq], 0.0, NEG_INF).astype(F32)] * (crow // tq), axis=0)

    q_blocks = []
    for g in range(NSA_GROUPS):
        grp = (lane1 >= g * HEAD_DIM) & (lane1 < (g + 1) * HEAD_DIM)
        for r in range(NSA_REP):
            qc = q_ref[:, r * LANES:(r + 1) * LANES]
            q_blocks.append(jnp.where(grp, qc, jnp.zeros_like(qc)))
    qst = jnp.concatenate(q_blocks, axis=0)

    s = _dot_nt(qst, kcmp) + jnp.concatenate([cmp_bias] * NSA_HEADS, axis=0)
    m = jnp.max(s, axis=-1, keepdims=True)
    e = jnp.exp2(s - m)
    has_key = qpos_col >= CMP_BLOCK - 1
    p = e * jnp.where(has_key, 1.0 / jnp.sum(e, axis=-1, keepdims=True), 0.0)
    o_cmp = _dot(p.astype(BF16), vcmp)

    p_sum = []
    for g in range(NSA_GROUPS):
        blocks = [p[(g * NSA_REP + r) * tq:(g * NSA_REP + r + 1) * tq] for r in range(NSA_REP)]
        p_sum.append((blocks[0] + blocks[1]) + (blocks[2] + blocks[3]))
    imp = jnp.dot(jnp.concatenate(p_sum, axis=0), overlap, preferred_element_type=F32,
                  precision=lax.Precision.HIGHEST)
    imp_t = jnp.concatenate([imp[g * tq:(g + 1) * tq].T for g in range(NSA_GROUPS)], axis=1)
    imp_t = jnp.where(future, NEG_INF, jnp.where(forced, FORCE_SCORE, imp_t))
    sub = 8
    vals = [imp_t[r0:r0 + sub, :] for r0 in range(0, n_sel, sub)]
    row8 = lax.broadcasted_iota(jnp.int32, vals[0].shape, 0)
    ranks = [jnp.zeros(v.shape, jnp.int32) for v in vals]
    vwin1 = _with_ones(vwin)
    accs = []
    for c in range(chains):
        s = jnp.where(win_ok[:crow], _dot_nt(qst[c * crow:(c + 1) * crow], kwin), NEG_INF)
        accs.append(_softmax_pv(s, vwin1)[1])
        for jp in range(c * n_sel // chains, (c + 1) * n_sel // chains):
            other = imp_t[jp:jp + 1, :]
            for t, v in enumerate(vals):
                if t > jp // sub:
                    beats = other >= v
                elif t < jp // sub:
                    beats = other > v
                else:
                    beats = (other > v) | ((other == v) & (row8 > jp % sub))
                ranks[t] = ranks[t] + jnp.where(beats, 1, 0)
    rank = jnp.concatenate(ranks, axis=0)
    acc = jnp.concatenate(accs, axis=0)
    o_win = acc[:, :LANES] / acc[:, LANES:]

    aux = aux_ref[...]
    gate = lambda head, branch: aux[:, FOX_HEADS + 3 * head + branch:FOX_HEADS + 3 * head + branch + 1]
    partial = [gate(head, 0) * o_cmp[head * tq:(head + 1) * tq]
               + gate(head, 2) * o_win[head * tq:(head + 1) * tq] for head in range(NSA_HEADS)]

    sb_blocks = []
    for g in range(NSA_GROUPS):
        chosen = rank[:, g * tq:(g + 1) * tq] < min(SEL_TOPK, n_sel)
        sb_t = jnp.concatenate([jnp.where(chosen, 0.0, NEG_INF).astype(F32),
                                jnp.full((LANES - n_sel, tq), NEG_INF, F32)], axis=0)
        sb_blocks.extend([sb_t.T.astype(BF16)] * NSA_REP)
    qa = jnp.concatenate([qst, jnp.concatenate(sb_blocks, axis=0)], axis=1)

    def sel_scores(j, masked):
        ks0 = pl.multiple_of(j * tk, tk)
        ka = jnp.concatenate([ks_ref[pl.ds(ks0, tk), :], onehot_ref[pl.ds(ks0, tk), :]], axis=1)
        scores = []
        for c in range(chains):
            s = _dot_nt(qa[c * crow:(c + 1) * crow], ka)
            if masked:
                s = s + diag_bias
            scores.append(s)
        return scores

    def sel_values(j):
        return [_with_ones(vs_ref[pl.ds(pl.multiple_of(j * tk, tk), tk), :])] * chains

    acc = jnp.concatenate(_flash_tiles(jd, jd, sel_scores, sel_values), axis=0)
    o_sel = acc[:, :LANES] / acc[:, LANES:]

    for r in range(NSA_REP):
        per_group = []
        for g in range(NSA_GROUPS):
            head = g * NSA_REP + r
            per_group.append(partial[head] + gate(head, 1) * o_sel[head * tq:(head + 1) * tq])
        o = jnp.where(lane1 < HEAD_DIM, per_group[0], per_group[1])
        o_ref[:, r * LANES:(r + 1) * LANES] = o.astype(o_ref.dtype)


def _nsa(nq, nkv, kcmp, vcmp, aux, *, batch, seq, tq, tk):
    n = nq.shape[0]
    onehot = (jnp.arange(seq)[:, None] // SEL_BLOCK == jnp.arange(LANES)[None, :]).astype(BF16)
    nt = seq // tq
    n16 = kcmp.shape[1]
    kv = lambda c: pl.BlockSpec((seq, LANES), lambda b, i: (b, c))
    cmp_spec = pl.BlockSpec((None, n16, LANES), lambda b, i: (b, 0, 0))
    width = NSA_HEADS * HEAD_DIM
    return pl.pallas_call(
        functools.partial(_nsa_kernel, tq=tq, tk=tk, seq=seq, chains=NSA_CHAINS),
        grid=(batch, nt),
        in_specs=[pl.BlockSpec((tq, width), lambda b, i: (b * nt + i, 0)),
                  cmp_spec, cmp_spec, kv(2), kv(3), kv(4), kv(5),
                  pl.BlockSpec((tq, LANES), lambda b, i: (b * nt + i, 0)),
                  pl.BlockSpec((seq, LANES), lambda b, i: (0, 0))],
        out_specs=pl.BlockSpec((tq, width), lambda b, i: (b * nt + i, 0)),
        out_shape=jax.ShapeDtypeStruct((n, width), BF16),
        compiler_params=_params("arbitrary", "arbitrary"),
        name="nsa",
    )(nq, kcmp, vcmp, nkv, nkv, nkv, nkv, aux, onehot)


def _dilated_kernel(q1_ref, k1_ref, v1_ref, q4_ref, k4_ref, v4_ref, q16_ref, k16_ref, v16_ref,
                    o_ref, osc_ref, lsc_ref, *, tile, seq):
    i = pl.program_id(1)
    tq = DIL_SPAN
    rows = DIL_HEADS * tq
    row_u = lax.broadcasted_iota(jnp.int32, (rows, 1), 0) & (tq - 1)
    refs = ((q1_ref, k1_ref, v1_ref), (q4_ref, k4_ref, v4_ref), (q16_ref, k16_ref, v16_ref))
    for pat, (dilation, (q_ref, k_ref, v_ref)) in enumerate(zip(DIL_DILATIONS, refs)):
        length = seq // dilation
        span = min(2 * tq, length)
        per_class = tile // (tq * dilation)
        n_sub = tile // tq
        base = i * (tile // dilation)
        rel = row_u - lax.broadcasted_iota(jnp.int32, (1, span), 1)

        def band(offset):
            return jnp.where((rel + offset >= 0) & (rel + offset <= DIL_SPAN), 0.0, NEG_INF).astype(F32)

        bias_shifted, bias_aligned = band(min(DIL_SPAN, length - span + DIL_SPAN)), band(0)

        def sub_tile(sub):
            c = sub // per_class
            w = sub % per_class
            u0 = w * tq
            ks0 = pl.multiple_of(jnp.maximum(base + u0 - DIL_SPAN, 0), tq)
            qst = _stack_heads(q_ref[c, pl.ds(u0, tq), :], DIL_HEADS)
            bias = jnp.where(base == 0, bias_aligned, bias_shifted) if u0 == 0 else bias_shifted
            s = _dot_nt(qst, k_ref[c, pl.ds(ks0, span), :]) + bias
            m = jnp.max(s, axis=-1, keepdims=True)
            acc = _pair_pv(jnp.exp2((s - m).astype(BF16)),
                           _pair_values(v_ref[c, pl.ds(ks0, span), :]))
            den = acc[:, LANES:]
            dst = pl.ds(u0 * dilation + c, tq, stride=dilation)
            for half, (o, lse) in enumerate(zip(_unstack_halves(acc[:, :LANES] / den, tq),
                                                _unstack_halves(m + jnp.log2(den), tq))):
                osc_ref[pat, half, dst, :] = o
                lsc_ref[pat, half, dst, :] = lse

        for sub in range(n_sub):
            sub_tile(sub)

    chunk = 2 * tq
    for r0 in range(0, tile, chunk):
        for half in range(2):
            ls = [lsc_ref[p, half, r0:r0 + chunk, :] for p in range(3)]
            top = jnp.maximum(jnp.maximum(ls[0], ls[1]), ls[2])
            es = [jnp.exp2(l - top) for l in ls]
            num = (es[0] * osc_ref[0, half, r0:r0 + chunk, :] + es[1] * osc_ref[1, half, r0:r0 + chunk, :]
                   + es[2] * osc_ref[2, half, r0:r0 + chunk, :])
            o_ref[r0:r0 + chunk, half * LANES:(half + 1) * LANES] = (
                num / (es[0] + es[1] + es[2])).astype(o_ref.dtype)


def _dilated(dil1, dil4, dil16, *, batch, seq, tile):
    width = DIL_HEADS * HEAD_DIM
    nt = seq // tile
    specs = []
    for arr in (dil1, dil4, dil16):
        dilation, length = arr.shape[1], arr.shape[2]
        specs.append(pl.BlockSpec((None, dilation, tile // dilation, width), lambda b, i: (b, 0, i, 0)))
        for part in (1, 2):
            specs.append(pl.BlockSpec((None, dilation, length, width),
                                      lambda b, i, part=part: (b, 0, 0, part)))
    return pl.pallas_call(
        functools.partial(_dilated_kernel, tile=tile, seq=seq),
        grid=(batch, nt),
        in_specs=specs,
        out_specs=pl.BlockSpec((tile, width), lambda b, i: (b * nt + i, 0)),
        out_shape=jax.ShapeDtypeStruct((batch * seq, width), BF16),
        scratch_shapes=[pltpu.VMEM((3, 2, tile, LANES), F32), pltpu.VMEM((3, 2, tile, LANES), F32)],
        compiler_params=_params("arbitrary", "arbitrary"),
        name="dilated",
    )(dil1, dil1, dil1, dil4, dil4, dil4, dil16, dil16, dil16)


SUBLANES = 8
ROW_BLOCK = SUBLANES * SUBLANES


def _swap_rows(x, stage_ref):
    rows, width = x.shape
    for c in range(width // LANES):
        stage_ref[c] = x[:, c * LANES:(c + 1) * LANES]
    cols = []
    for c in range(width // LANES):
        slabs = [stage_ref[c, pl.ds(b0 + k, SUBLANES, stride=SUBLANES), :]
                 for b0 in range(0, rows, ROW_BLOCK) for k in range(SUBLANES)]
        cols.append(jnp.concatenate(slabs, axis=0))
    return jnp.concatenate(cols, axis=1)


def _mix_ffn_kernel(x_ref, fox_ref, nsa_ref, dil_ref, wout_ref, gmix_ref,
                    gpre_ref, wup_ref, cw_ref, cb_ref, wdn_ref, gpost_ref, o_ref,
                    tail_ref, stage_ref, act_ref, *, tm, d_ff, chunk):
    @pl.when(pl.program_id(1) == 0)
    def _():
        tail_ref[...] = jnp.zeros_like(tail_ref)

    fw = FOX_HEADS * HEAD_DIM
    nw = NSA_HEADS * HEAD_DIM
    mix = (_dot(fox_ref[...], wout_ref[:fw, :]) + _dot(nsa_ref[...], wout_ref[fw:fw + nw, :])
           + _dot(dil_ref[...], wout_ref[fw + nw:, :]))
    x = x_ref[...] + _rms(mix, gmix_ref[...])
    h = _swap_rows(_rms(x, gpre_ref[...]), stage_ref).astype(BF16)
    row8 = lax.broadcasted_iota(jnp.int32, (SUBLANES, chunk), 0)

    def down_one(vreg_row, prev_last):
        return jnp.where(row8 == 0, prev_last, pltpu.roll(vreg_row, 1, 0))

    def conv(c0):
        u = _dot(h, wup_ref[:, c0:c0 + chunk])
        tail = tail_ref[:, c0:c0 + chunk]
        prev6, prev7 = tail[SUBLANES - 1:SUBLANES], tail[2 * SUBLANES - 1:]
        u1, u2 = [], []
        for b0 in range(0, tm, ROW_BLOCK):
            v = [u[b0 + k * SUBLANES:b0 + (k + 1) * SUBLANES] for k in range(SUBLANES)]
            s6, s7 = down_one(v[6], prev6), down_one(v[7], prev7)
            u1 += [s7] + v[:7]
            u2 += [s6, s7] + v[:6]
            prev6, prev7 = v[6][SUBLANES - 1:], v[7][SUBLANES - 1:]
        tail_ref[:, c0:c0 + chunk] = u[tm - 2 * SUBLANES:, :]
        return (cb_ref[:, c0:c0 + chunk] + jnp.concatenate(u2, axis=0) * cw_ref[0:1, c0:c0 + chunk]
                + jnp.concatenate(u1, axis=0) * cw_ref[1:2, c0:c0 + chunk] + u * cw_ref[2:3, c0:c0 + chunk])

    for c0 in range(0, d_ff, chunk):
        a = conv(c0)
        b = conv(d_ff + c0)
        act_ref[:, c0:c0 + chunk] = (a * jax.nn.sigmoid(a) * b).astype(BF16)
    y = _dot(act_ref[...], wdn_ref[...])
    o_ref[...] = x + _swap_rows(_rms(y, gpost_ref[...]), stage_ref)


def _mix_ffn(x, fox_o, nsa_o, dil_o, wout, gmix, gpre, wup, cw, cb, wdn, gpost, *, batch, seq, tm, chunk):
    n, d = x.shape
    d_ff = wdn.shape[0]
    nt = seq // tm
    tok = lambda width: pl.BlockSpec((tm, width), lambda b, i: (b * nt + i, 0))
    const = lambda a: pl.BlockSpec(a.shape, lambda b, i: (0, 0), pipeline_mode=pl.Buffered(1))
    return pl.pallas_call(
        functools.partial(_mix_ffn_kernel, tm=tm, d_ff=d_ff, chunk=chunk),
        grid=(batch, nt),
        in_specs=[tok(d), tok(fox_o.shape[1]), tok(nsa_o.shape[1]), tok(dil_o.shape[1]),
                  const(wout), const(gmix), const(gpre), const(wup), const(cw), const(cb),
                  const(wdn), const(gpost)],
        out_specs=tok(d),
        out_shape=jax.ShapeDtypeStruct((n, d), F32),
        scratch_shapes=[pltpu.VMEM((2 * SUBLANES, 2 * d_ff), F32), pltpu.VMEM((d // LANES, tm, LANES), F32),
                        pltpu.VMEM((tm, d_ff), BF16)],
        compiler_params=_params("arbitrary", "arbitrary"),
        name="mix_ffn",
    )(x, fox_o, nsa_o, dil_o, wout, gmix, gpre, wup, cw, cb, wdn, gpost)


def _rope_tables(positions):
    inv_freq = ROPE_THETA ** (-2.0 * jnp.arange(ROPE_HALF, dtype=F32) / ROPE_DIM)
    ang = positions.astype(F32).reshape(-1, 1) * inv_freq
    cos, sin = jnp.cos(ang), jnp.sin(ang)
    n = ang.shape[0]
    ones = jnp.ones((n, HEAD_DIM - ROPE_DIM), F32)
    zeros_h = jnp.zeros((n, ROPE_HALF), F32)
    zeros_r = jnp.zeros((n, HEAD_DIM - ROPE_DIM), F32)
    rc = jnp.concatenate([cos, cos, ones], axis=1)
    rs1 = jnp.concatenate([zeros_h, sin, zeros_r], axis=1)
    rs2 = jnp.concatenate([-sin, zeros_h, zeros_r], axis=1)
    tile2 = lambda t: jnp.concatenate([t, t], axis=1)
    return tile2(rc), tile2(rs1), tile2(rs2)


def _nsa_head_perm():
    cols = []
    for r in range(NSA_REP):
        for g in range(NSA_GROUPS):
            h = g * NSA_REP + r
            cols.extend(range(h * HEAD_DIM, (h + 1) * HEAD_DIM))
    return np.asarray(cols)


def _regroup_w_in(w):
    fw, nw, kvw, dw = FOX_HEADS * HEAD_DIM, NSA_HEADS * HEAD_DIM, NSA_GROUPS * HEAD_DIM, DIL_HEADS * HEAD_DIM
    sizes = [fw, fw, fw, FOX_HEADS, nw] + [kvw] * 6 + [3 * NSA_HEADS, dw, dw, dw]
    offs = np.concatenate([[0], np.cumsum(sizes)])
    part = lambda idx: w[:, offs[idx]:offs[idx + 1]]
    fq, fk, fv, ff, nq = (part(t) for t in range(5))
    kvs = [part(t) for t in range(5, 11)]
    ng, dq, dk, dv = (part(t) for t in range(11, 15))
    nq = nq[:, _nsa_head_perm()]
    pad = jnp.zeros((w.shape[0], 2 * LANES - FOX_HEADS - 3 * NSA_HEADS), w.dtype)
    cols = [fq * Q_SCALE, fk, fv, nq * Q_SCALE] + kvs + [dq * Q_SCALE, dk, dv, ff, ng, pad]
    return jnp.concatenate(cols, axis=1).astype(BF16)


def _regroup_w_out(w):
    fw, nw = FOX_HEADS * HEAD_DIM, NSA_HEADS * HEAD_DIM
    return jnp.concatenate([w[:fw], w[fw:fw + nw][_nsa_head_perm()], w[fw + nw:]], axis=0).astype(BF16)


def _place_w1(w1):
    hidden = w1.shape[1]
    w = w1.reshape(CMP_BLOCK // CMP_STRIDE, CMP_STRIDE, HEAD_DIM, hidden)
    z = jnp.zeros_like(w)
    per_group = [jnp.concatenate([w, z], axis=2), jnp.concatenate([z, w], axis=2)]
    return jnp.stack(per_group).reshape(NSA_GROUPS, CMP_BLOCK // CMP_STRIDE, CMP_STRIDE * LANES,
                                        hidden).astype(BF16)


def _place_pos(p):
    z = jnp.zeros_like(p)
    row = jnp.concatenate([p, z], axis=1).reshape(1, -1)
    return jnp.concatenate([row, jnp.zeros((7, row.shape[1]), row.dtype)], axis=0)


def _place_w2(w2):
    z = jnp.zeros_like(w2)
    return jnp.stack([jnp.concatenate([w2, z], axis=1), jnp.concatenate([z, w2], axis=1)]).astype(BF16)


class _Tiles(NamedTuple):
    inproj: int
    key: int
    fox_q: int
    nsa_q: int
    dilated: int
    ffn: int
    ffn_cols: int


def _tiles(seq):
    base = min(4 * LANES, seq)
    tiles = _Tiles(inproj=min(2 * base, seq), key=base, fox_q=base, nsa_q=min(2 * LANES, base),
                   dilated=DIL_DILATIONS[-1] * DIL_SPAN, ffn=base, ffn_cols=2 * LANES)
    assert seq % tiles.dilated == 0 and seq % tiles.inproj == 0, "sequence must be a multiple of 2048"
    assert seq // SEL_BLOCK <= LANES, "selection blocks must fit one lane tile"
    return tiles


def kernel(x, positions, attn_pre_norm, attn_post_norm, ffn_pre_norm, ffn_post_norm, w_in, b_forget, b_nsa_gate, cmp_pos_k, cmp_w1_k, cmp_w2_k, cmp_pos_v, cmp_w1_v, cmp_w2_v, w_out, w_up, conv_w, conv_b, w_down):
    batch, seq, d = x.shape
    depth = w_in.shape[0]
    n = batch * seq
    tiles = _tiles(seq)

    rc, rs1, rs2 = _rope_tables(positions)
    xf = x.reshape(n, d)
    for l in range(depth):
        bias = jnp.concatenate([b_forget[l], b_nsa_gate[l],
                                jnp.zeros((LANES - FOX_HEADS - 3 * NSA_HEADS,), F32)]).reshape(1, LANES)
        fox, nq, nkv, dil, dil4, dil16, aux, ck_t, xk, xv = _inproj(
            xf, attn_pre_norm[l].reshape(1, d), _regroup_w_in(w_in[l]), rc, rs1, rs2, bias,
            batch=batch, seq=seq, tm=tiles.inproj)
        fox_o = _fox(fox, ck_t, batch=batch, seq=seq, tq=tiles.fox_q, tk=tiles.key)

        kcmp, vcmp = _compress(
            xk, xv, _place_w1(cmp_w1_k[l]), _place_w2(cmp_w2_k[l]), _place_pos(cmp_pos_k[l]),
            _place_w1(cmp_w1_v[l]), _place_w2(cmp_w2_v[l]), _place_pos(cmp_pos_v[l]))
        nsa_o = _nsa(nq, nkv, kcmp, vcmp, aux, batch=batch, seq=seq, tq=tiles.nsa_q, tk=tiles.key)

        dil_o = _dilated(dil.reshape(batch, 1, seq, dil.shape[1]), dil4, dil16,
                         batch=batch, seq=seq, tile=tiles.dilated)

        xf = _mix_ffn(xf, fox_o, nsa_o, dil_o, _regroup_w_out(w_out[l]), attn_post_norm[l].reshape(1, d),
                      ffn_pre_norm[l].reshape(1, d), w_up[l].astype(BF16), conv_w[l],
                      conv_b[l].reshape(1, -1), w_down[l].astype(BF16), ffn_post_norm[l].reshape(1, d),
                      batch=batch, seq=seq, tm=tiles.ffn, chunk=tiles.ffn_cols)
    return xf.reshape(batch, seq, d)
```

```python
import functools
import math
from typing import NamedTuple

import numpy as np
import jax
import jax.numpy as jnp
from jax import lax
from jax.experimental import pallas as pl
from jax.experimental.pallas import tpu as pltpu

HEAD_DIM = 64
FOX_HEADS = 4
NSA_HEADS = 8
NSA_GROUPS = 2
NSA_REP = NSA_HEADS // NSA_GROUPS
DIL_HEADS = 4
DIL_DILATIONS = (1, 4, 16)
DIL_SPAN = 128
ROPE_THETA = 500000.0
ROPE_DIM = HEAD_DIM // 4
ROPE_HALF = ROPE_DIM // 2
CMP_BLOCK = 32
CMP_STRIDE = 16
SEL_BLOCK = 64
SEL_TOPK = 16
NSA_WINDOW = 512
CONV_WIDTH = 3
RMS_EPS = 1e-6
NEG_INF = -1e30
FORCE_SCORE = 1e9
ATTN_SCALE = HEAD_DIM ** -0.5
LOG2E = math.log2(math.e)
Q_SCALE = ATTN_SCALE * LOG2E
LANES = 128
VMEM_LIMIT = 56 * 1024 * 1024
FLASH_GROUPS = (3, 1)
NSA_CHAINS = 4

F32 = jnp.float32
BF16 = jnp.bfloat16


def _dot_nt(a, b):
    return lax.dot_general(a, b, (((1,), (1,)), ((), ())), preferred_element_type=F32)


def _dot(a, b):
    return jnp.dot(a, b, preferred_element_type=F32)


def _rms(x, g):
    return x * lax.rsqrt(jnp.mean(x * x, axis=-1, keepdims=True) + RMS_EPS) * g


def _params(*sem):
    return pltpu.CompilerParams(dimension_semantics=sem, vmem_limit_bytes=VMEM_LIMIT)


def _with_ones(v):
    return jnp.concatenate([v, jnp.ones((v.shape[0], LANES), v.dtype)], axis=1)


def _softmax_pv(s, v1):
    m = jnp.max(s, axis=-1, keepdims=True)
    return m, _dot(jnp.exp2((s - m).astype(BF16)), v1)


def _flash_tiles(n_before, first, score_fn, value_fn):
    state = [_softmax_pv(s, v1) for s, v1 in zip(score_fn(first, True), value_fn(first))]

    def step(j, state):
        out = []
        for s, v1, (m, acc) in zip(score_fn(j, False), value_fn(j), state):
            m_new = jnp.maximum(m, jnp.max(s, axis=-1, keepdims=True))
            p = jnp.exp2((s - m_new).astype(BF16))
            out.append((m_new, jnp.exp2(m - m_new) * acc + _dot(p, v1)))
        return tuple(out)

    state = tuple(state)
    done = 0
    for size in FLASH_GROUPS:
        def group(t, state, size=size, done=done):
            for u in range(size):
                state = step(done + t * size + u, state)
            return state

        trips = (n_before - done) // size
        state = lax.fori_loop(0, trips, group, state)
        done = done + trips * size
    return [acc for _, acc in state]


_ROPE_CHUNKS = {"fox": (), "nq": (0, 1, 2, 3), "nkv": (0, 2, 4), "dil": (0, 1, 2, 3)}
_QUERY_CHUNKS = {"fox": (0, 1), "nq": (0, 1, 2, 3), "nkv": (), "dil": (0, 1)}
_SEG_WIDTH = {"fox": 768, "nq": 512, "nkv": 768, "dil": 768}
_SEG_ORDER = ("fox", "nq", "nkv", "dil")
_AUX_OFFSET = sum(_SEG_WIDTH.values())
_W_IN_COLS = _AUX_OFFSET + 2 * LANES


def _inproj_kernel(x_ref, g_ref, w_ref, rc_ref, rs1_ref, rs2_ref, bias_ref,
                   fox_ref, nq_ref, nkv_ref, dil_ref, dil4_ref, dil16_ref, aux_ref, ck_ref,
                   xk_ref, xv_ref, carry_ref, stage_ref, *, tm):
    @pl.when(pl.program_id(1) == 0)
    def _():
        carry_ref[...] = jnp.zeros_like(carry_ref)

    h = _rms(x_ref[...], g_ref[...]).astype(BF16)
    rc, rs1, rs2 = rc_ref[...], rs1_ref[...], rs2_ref[...]
    outs = {"fox": fox_ref, "nq": nq_ref, "nkv": nkv_ref, "dil": dil_ref}
    col = 0
    for name in _SEG_ORDER:
        for c0 in range(0, _SEG_WIDTH[name], 2 * LANES):
            y = _dot(h, w_ref[:, col + c0:col + c0 + 2 * LANES])
            for half in range(2):
                chunk = c0 // LANES + half
                yc = y[:, half * LANES:(half + 1) * LANES]
                if chunk in _QUERY_CHUNKS[name]:
                    yc = yc * Q_SCALE
                if chunk in _ROPE_CHUNKS[name]:
                    yc = (yc * rc + pltpu.roll(yc, ROPE_HALF, 1) * rs1
                          + pltpu.roll(yc, LANES - ROPE_HALF, 1) * rs2)
                outs[name][:, chunk * LANES:(chunk + 1) * LANES] = yc.astype(BF16)
                if name == "nkv" and chunk < 2:
                    stage_ref[...] = yc
                    for t in range(CMP_STRIDE):
                        part = stage_ref[pl.ds(t, tm // CMP_STRIDE, stride=CMP_STRIDE), :]
                        (xk_ref, xv_ref)[chunk][:, t * LANES:(t + 1) * LANES] = part.astype(BF16)
                if name == "dil":
                    stage_ref[...] = yc
                    for dilation, ref in ((DIL_DILATIONS[1], dil4_ref), (DIL_DILATIONS[2], dil16_ref)):
                        for c in range(dilation):
                            part = stage_ref[pl.ds(c, tm // dilation, stride=dilation), :]
                            ref[c, :, chunk * LANES:(chunk + 1) * LANES] = part.astype(BF16)
        col += _SEG_WIDTH[name]

    ya = _dot(h, w_ref[:, _AUX_OFFSET:_AUX_OFFSET + 2 * LANES])[:, :LANES] + bias_ref[...]
    aux_ref[...] = jax.nn.sigmoid(ya)
    logf = jnp.minimum(ya, 0.0) - jnp.log1p(jnp.exp(-jnp.abs(ya)))
    csum = logf.T[:8, :]
    lane = lax.broadcasted_iota(jnp.int32, csum.shape, 1)
    shift = 1
    while shift < tm:
        csum = csum + jnp.where(lane >= shift, pltpu.roll(csum, shift, 1), 0.0)
        shift *= 2
    csum = csum + jnp.concatenate([carry_ref[...]] * (tm // LANES), axis=1)
    carry_ref[...] = jnp.broadcast_to(csum[:, tm - 1:tm], carry_ref.shape)
    ck_ref[...] = csum * LOG2E


def _inproj(x, g, w, rc, rs1, rs2, bias, *, batch, seq, tm):
    n, d = x.shape
    nt = seq // tm
    tok = lambda width: pl.BlockSpec((tm, width), lambda b, i: (b * nt + i, 0))
    const = lambda shape: pl.BlockSpec(shape, lambda b, i: (0, 0))
    cls = lambda dil: pl.BlockSpec((None, dil, tm // dil, 768), lambda b, i: (b, 0, i, 0))
    cls_shape = lambda dil: jax.ShapeDtypeStruct((batch, dil, seq // dil, 768), BF16)
    d4, d16 = DIL_DILATIONS[1], DIL_DILATIONS[2]
    blk16 = pl.BlockSpec((None, tm // CMP_STRIDE, CMP_STRIDE * LANES), lambda b, i: (b, i, 0))
    blk16_shape = jax.ShapeDtypeStruct((batch, seq // CMP_STRIDE, CMP_STRIDE * LANES), BF16)
    return pl.pallas_call(
        functools.partial(_inproj_kernel, tm=tm),
        grid=(batch, nt),
        in_specs=[tok(d), const((1, d)), const(w.shape), tok(LANES), tok(LANES), tok(LANES),
                  const((1, LANES))],
        out_specs=[tok(768), tok(512), tok(768), tok(768), cls(d4), cls(d16), tok(LANES),
                   pl.BlockSpec((None, 8, tm), lambda b, i: (b, 0, i)), blk16, blk16],
        out_shape=[jax.ShapeDtypeStruct((n, 768), BF16), jax.ShapeDtypeStruct((n, 512), BF16),
                   jax.ShapeDtypeStruct((n, 768), BF16), jax.ShapeDtypeStruct((n, 768), BF16),
                   cls_shape(d4), cls_shape(d16), jax.ShapeDtypeStruct((n, LANES), F32),
                   jax.ShapeDtypeStruct((batch, 8, seq), F32), blk16_shape, blk16_shape],
        scratch_shapes=[pltpu.VMEM((8, LANES), F32), pltpu.VMEM((tm, LANES), F32)],
        compiler_params=_params("arbitrary", "arbitrary"),
        name="inproj",
    )(x, g, w, rc, rs1, rs2, bias)


def _stack_heads(q, heads):
    lane = lax.broadcasted_iota(jnp.int32, (1, q.shape[1]), 1)
    blocks = []
    for h in range(heads):
        head = (lane >= h * HEAD_DIM) & (lane < (h + 1) * HEAD_DIM)
        blocks.append(jnp.where(head, q, jnp.zeros_like(q)))
    return jnp.concatenate(blocks, axis=0)


def _pair_values(v):
    return _with_ones(v[:, :LANES]), _with_ones(v[:, LANES:])


def _pair_pv(p, v_pair):
    half = p.shape[0] // 2
    return jnp.concatenate([_dot(p[:half], v_pair[0]), _dot(p[half:], v_pair[1])], axis=0)


def _unstack_pairs(x, tq):
    return jnp.concatenate(_unstack_halves(x, tq), axis=1)


def _unstack_halves(x, tq):
    lane = lax.broadcasted_iota(jnp.int32, (1, LANES), 1)
    return [jnp.where(lane < HEAD_DIM, x[(2 * c) * tq:(2 * c + 1) * tq],
                      x[(2 * c + 1) * tq:(2 * c + 2) * tq]) for c in range(2)]


def _fox_kernel(q_ref, k_ref, v_ref, ck_ref, o_ref, *, tq, tk):
    i = pl.program_id(1)
    start = i * tq
    qst = _stack_heads(q_ref[...], FOX_HEADS)
    jd = start // tk
    qpos = start + lax.broadcasted_iota(jnp.int32, (tq, 1), 0)
    kpos = jd * tk + lax.broadcasted_iota(jnp.int32, (1, tk), 1)
    diag_bias = jnp.where(kpos <= qpos, 0.0, NEG_INF).astype(F32)

    def scores(j, masked):
        ks0 = pl.multiple_of(j * tk, tk)
        k = k_ref[pl.ds(ks0, tk), :]
        out = []
        for h in range(FOX_HEADS):
            s = _dot_nt(qst[h * tq:(h + 1) * tq], k) - ck_ref[h:h + 1, pl.ds(ks0, tk)]
            out.append(s + diag_bias if masked else s)
        return out

    def values(j):
        v_pair = _pair_values(v_ref[pl.ds(pl.multiple_of(j * tk, tk), tk), :])
        return [v_pair[h // 2] for h in range(FOX_HEADS)]

    acc = jnp.concatenate(_flash_tiles(jd, jd, scores, values), axis=0)
    o_ref[...] = _unstack_pairs(acc[:, :LANES] / acc[:, LANES:], tq).astype(o_ref.dtype)


def _fox(fox, ck_t, *, batch, seq, tq, tk):
    n = fox.shape[0]
    nt = seq // tq
    width = FOX_HEADS * HEAD_DIM
    return pl.pallas_call(
        functools.partial(_fox_kernel, tq=tq, tk=tk),
        grid=(batch, nt),
        in_specs=[pl.BlockSpec((tq, width), lambda b, i: (b * nt + i, 0)),
                  pl.BlockSpec((seq, width), lambda b, i: (b, 1)),
                  pl.BlockSpec((seq, width), lambda b, i: (b, 2)),
                  pl.BlockSpec((None, 8, seq), lambda b, i: (b, 0, 0))],
        out_specs=pl.BlockSpec((tq, width), lambda b, i: (b * nt + i, 0)),
        out_shape=jax.ShapeDtypeStruct((n, width), BF16),
        compiler_params=_params("arbitrary", "arbitrary"),
        name="fox",
    )(fox, fox, fox, ck_t)


def _gelu_tanh(x):
    return 0.5 * x * (1.0 + jnp.tanh(math.sqrt(2.0 / math.pi) * (x + 0.044715 * (x * x * x))))


def _compress_kernel(xk_ref, xv_ref, w1k_ref, w2k_ref, pk_ref, w1v_ref, w2v_ref, pv_ref,
                     kc_ref, vc_ref):
    feat = CMP_STRIDE * LANES

    def run(x_ref, w1_ref, w2_ref, p_ref, o_ref):
        x = x_ref[...]
        n16 = x.shape[0]
        p = p_ref[...].astype(BF16)
        posb = (_dot(p[:, :feat], w1_ref[0, 0]) + _dot(p[:, feat:], w1_ref[0, 1]))[0:1, :]
        out = None
        for g in range(NSA_GROUPS):
            hid = _dot(x, w1_ref[g, 0]) + pltpu.roll(_dot(x, w1_ref[g, 1]), n16 - 1, 0) + posb
            a = _gelu_tanh(hid).astype(BF16)
            og = _dot(a, w2_ref[g])
            out = og if out is None else out + og
        o_ref[...] = out.astype(o_ref.dtype)

    run(xk_ref, w1k_ref, w2k_ref, pk_ref, kc_ref)
    run(xv_ref, w1v_ref, w2v_ref, pv_ref, vc_ref)


def _compress(xk, xv, w1k, w2k, pk, w1v, w2v, pv):
    batch, n16, feat = xk.shape
    xspec = pl.BlockSpec((None, n16, feat), lambda b: (b, 0, 0))
    full = lambda a: pl.BlockSpec(a.shape, lambda b: (0,) * a.ndim)
    ospec = pl.BlockSpec((None, n16, LANES), lambda b: (b, 0, 0))
    return pl.pallas_call(
        _compress_kernel,
        grid=(batch,),
        in_specs=[xspec, xspec, full(w1k), full(w2k), full(pk), full(w1v), full(w2v), full(pv)],
        out_specs=[ospec, ospec],
        out_shape=[jax.ShapeDtypeStruct((batch, n16, LANES), BF16)] * 2,
        compiler_params=_params("arbitrary"),
        name="nsa_compress",
    )(xk, xv, w1k, w2k, pk, w1v, w2v, pv)


def _nsa_kernel(q_ref, kcmp_ref, vcmp_ref, ks_ref, vs_ref, kw_ref, vw_ref, aux_ref, onehot_ref,
                o_ref, *, tq, tk, seq, chains):
    i = pl.program_id(1)
    start = i * tq
    n_cmp = kcmp_ref.shape[0]
    wspan = NSA_WINDOW + tq

    rows = NSA_HEADS * tq
    crow = rows // chains
    n_sel = seq // SEL_BLOCK

    lane1 = lax.broadcasted_iota(jnp.int32, (1, LANES), 1)
    qpos_col = start + (lax.broadcasted_iota(jnp.int32, (rows, 1), 0) & (tq - 1))
    qpos_row = start + (lax.broadcasted_iota(jnp.int32, (1, NSA_GROUPS * tq), 1) & (tq - 1))

    kcmp = kcmp_ref[...]
    vcmp = vcmp_ref[...]
    cmp_end = lax.broadcasted_iota(jnp.int32, (1, n_cmp), 1) * CMP_STRIDE + (CMP_BLOCK - 1)
    cmp_bias = jnp.where(cmp_end <= qpos_col[:tq], 0.0, NEG_INF).astype(F32)
    cs = lax.broadcasted_iota(jnp.int32, (n_cmp, LANES), 0) * CMP_STRIDE
    ss = lax.broadcasted_iota(jnp.int32, (n_cmp, LANES), 1) * SEL_BLOCK
    overlap = jnp.where((cs < ss + SEL_BLOCK) & (cs + CMP_BLOCK > ss), 1.0, 0.0).astype(F32)

    blk = lax.broadcasted_iota(jnp.int32, (LANES, NSA_GROUPS * tq), 0)
    cur = qpos_row // SEL_BLOCK
    forced = (blk == 0) | (blk == cur) | (blk == cur - 1)
    future = blk * SEL_BLOCK > qpos_row

    wstart = pl.multiple_of(jnp.maximum(start - NSA_WINDOW, 0), tq)
    kwin = kw_ref[pl.ds(wstart, wspan), :]
    vwin = vw_ref[pl.ds(wstart, wspan), :]
    kpos_w = wstart + lax.broadcasted_iota(jnp.int32, (1, wspan), 1)
    win_ok = (kpos_w <= qpos_col) & (kpos_w > qpos_col - NSA_WINDOW)

    jd = start // tk
    col_tk = jd * tk + lax.broadcasted_iota(jnp.int32, (1, tk), 1)
    diag_bias = jnp.concatenate(
        [jnp.where(col_tk <= qpos_col[:tq], 0.0, NEG_INF).astype(F32)] * (crow // tq), axis=0)

    q_blocks = []
    for g in range(NSA_GROUPS):
        grp = (lane1 >= g * HEAD_DIM) & (lane1 < (g + 1) * HEAD_DIM)
        for r in range(NSA_REP):
            qc = q_ref[:, r * LANES:(r + 1) * LANES]
            q_blocks.append(jnp.where(grp, qc, jnp.zeros_like(qc)))
    qst = jnp.concatenate(q_blocks, axis=0)

    s = _dot_nt(qst, kcmp) + jnp.concatenate([cmp_bias] * NSA_HEADS, axis=0)
    m = jnp.max(s, axis=-1, keepdims=True)
    e = jnp.exp2(s - m)
    has_key = qpos_col >= CMP_BLOCK - 1
    p = e * jnp.where(has_key, 1.0 / jnp.sum(e, axis=-1, keepdims=True), 0.0)
    o_cmp = _dot(p.astype(BF16), vcmp)

    p_sum = []
    for g in range(NSA_GROUPS):
        blocks = [p[(g * NSA_REP + r) * tq:(g * NSA_REP + r + 1) * tq] for r in range(NSA_REP)]
        p_sum.append((blocks[0] + blocks[1]) + (blocks[2] + blocks[3]))
    imp = jnp.dot(jnp.concatenate(p_sum, axis=0), overlap, preferred_element_type=F32,
                  precision=lax.Precision.HIGHEST)
    imp_t = jnp.concatenate([imp[g * tq:(g + 1) * tq].T for g in range(NSA_GROUPS)], axis=1)
    imp_t = jnp.where(future, NEG_INF, jnp.where(forced, FORCE_SCORE, imp_t))
    sub = 8
    vals = [imp_t[r0:r0 + sub, :] for r0 in range(0, n_sel, sub)]
    row8 = lax.broadcasted_iota(jnp.int32, vals[0].shape, 0)
    ranks = [jnp.zeros(v.shape, jnp.int32) for v in vals]
    vwin1 = _with_ones(vwin)
    accs = []
    for c in range(chains):
        s = jnp.where(win_ok[:crow], _dot_nt(qst[c * crow:(c + 1) * crow], kwin), NEG_INF)
        accs.append(_softmax_pv(s, vwin1)[1])
        for jp in range(c * n_sel // chains, (c + 1) * n_sel // chains):
            other = imp_t[jp:jp + 1, :]
            for t, v in enumerate(vals):
                if t > jp // sub:
                    beats = other >= v
                elif t < jp // sub:
                    beats = other > v
                else:
                    beats = (other > v) | ((other == v) & (row8 > jp % sub))
                ranks[t] = ranks[t] + jnp.where(beats, 1, 0)
    rank = jnp.concatenate(ranks, axis=0)
    acc = jnp.concatenate(accs, axis=0)
    o_win = acc[:, :LANES] / acc[:, LANES:]

    aux = aux_ref[...]
    gate = lambda head, branch: aux[:, FOX_HEADS + 3 * head + branch:FOX_HEADS + 3 * head + branch + 1]
    partial = [gate(head, 0) * o_cmp[head * tq:(head + 1) * tq]
               + gate(head, 2) * o_win[head * tq:(head + 1) * tq] for head in range(NSA_HEADS)]

    sb_blocks = []
    for g in range(NSA_GROUPS):
        chosen = rank[:, g * tq:(g + 1) * tq] < min(SEL_TOPK, n_sel)
        sb_t = jnp.concatenate([jnp.where(chosen, 0.0, NEG_INF).astype(F32),
                                jnp.full((LANES - n_sel, tq), NEG_INF, F32)], axis=0)
        sb_blocks.extend([sb_t.T.astype(BF16)] * NSA_REP)
    qa = jnp.concatenate([qst, jnp.concatenate(sb_blocks, axis=0)], axis=1)

    def sel_scores(j, masked):
        ks0 = pl.multiple_of(j * tk, tk)
        ka = jnp.concatenate([ks_ref[pl.ds(ks0, tk), :], onehot_ref[pl.ds(ks0, tk), :]], axis=1)
        scores = []
        for c in range(chains):
            s = _dot_nt(qa[c * crow:(c + 1) * crow], ka)
            if masked:
                s = s + diag_bias
            scores.append(s)
        return scores

    def sel_values(j):
        return [_with_ones(vs_ref[pl.ds(pl.multiple_of(j * tk, tk), tk), :])] * chains

    acc = jnp.concatenate(_flash_tiles(jd, jd, sel_scores, sel_values), axis=0)
    o_sel = acc[:, :LANES] / acc[:, LANES:]

    for r in range(NSA_REP):
        per_group = []
        for g in range(NSA_GROUPS):
            head = g * NSA_REP + r
            per_group.append(partial[head] + gate(head, 1) * o_sel[head * tq:(head + 1) * tq])
        o = jnp.where(lane1 < HEAD_DIM, per_group[0], per_group[1])
        o_ref[:, r * LANES:(r + 1) * LANES] = o.astype(o_ref.dtype)


def _nsa(nq, nkv, kcmp, vcmp, aux, *, batch, seq, tq, tk):
    n = nq.shape[0]
    onehot = (jnp.arange(seq)[:, None] // SEL_BLOCK == jnp.arange(LANES)[None, :]).astype(BF16)
    nt = seq // tq
    n16 = kcmp.shape[1]
    kv = lambda c: pl.BlockSpec((seq, LANES), lambda b, i: (b, c))
    cmp_spec = pl.BlockSpec((None, n16, LANES), lambda b, i: (b, 0, 0))
    width = NSA_HEADS * HEAD_DIM
    return pl.pallas_call(
        functools.partial(_nsa_kernel, tq=tq, tk=tk, seq=seq, chains=NSA_CHAINS),
        grid=(batch, nt),
        in_specs=[pl.BlockSpec((tq, width), lambda b, i: (b * nt + i, 0)),
                  cmp_spec, cmp_spec, kv(2), kv(3), kv(4), kv(5),
                  pl.BlockSpec((tq, LANES), lambda b, i: (b * nt + i, 0)),
                  pl.BlockSpec((seq, LANES), lambda b, i: (0, 0))],
        out_specs=pl.BlockSpec((tq, width), lambda b, i: (b * nt + i, 0)),
        out_shape=jax.ShapeDtypeStruct((n, width), BF16),
        compiler_params=_params("arbitrary", "arbitrary"),
        name="nsa",
    )(nq, kcmp, vcmp, nkv, nkv, nkv, nkv, aux, onehot)


def _dilated_kernel(q1_ref, k1_ref, v1_ref, q4_ref, k4_ref, v4_ref, q16_ref, k16_ref, v16_ref,
                    o_ref, osc_ref, lsc_ref, *, tile, seq):
    i = pl.program_id(1)
    tq = DIL_SPAN
    rows = DIL_HEADS * tq
    row_u = lax.broadcasted_iota(jnp.int32, (rows, 1), 0) & (tq - 1)
    refs = ((q1_ref, k1_ref, v1_ref), (q4_ref, k4_ref, v4_ref), (q16_ref, k16_ref, v16_ref))
    for pat, (dilation, (q_ref, k_ref, v_ref)) in enumerate(zip(DIL_DILATIONS, refs)):
        length = seq // dilation
        span = min(2 * tq, length)
        per_class = tile // (tq * dilation)
        n_sub = tile // tq
        base = i * (tile // dilation)
        rel = row_u - lax.broadcasted_iota(jnp.int32, (1, span), 1)

        def band(offset):
            return jnp.where((rel + offset >= 0) & (rel + offset <= DIL_SPAN), 0.0, NEG_INF).astype(F32)

        bias_shifted, bias_aligned = band(min(DIL_SPAN, length - span + DIL_SPAN)), band(0)

        def sub_tile(sub):
            c = sub // per_class
            w = sub % per_class
            u0 = w * tq
            ks0 = pl.multiple_of(jnp.maximum(base + u0 - DIL_SPAN, 0), tq)
            qst = _stack_heads(q_ref[c, pl.ds(u0, tq), :], DIL_HEADS)
            bias = jnp.where(base == 0, bias_aligned, bias_shifted) if u0 == 0 else bias_shifted
            s = _dot_nt(qst, k_ref[c, pl.ds(ks0, span), :]) + bias
            m = jnp.max(s, axis=-1, keepdims=True)
            acc = _pair_pv(jnp.exp2((s - m).astype(BF16)),
                           _pair_values(v_ref[c, pl.ds(ks0, span), :]))
            den = acc[:, LANES:]
            dst = pl.ds(u0 * dilation + c, tq, stride=dilation)
            for half, (o, lse) in enumerate(zip(_unstack_halves(acc[:, :LANES] / den, tq),
                                                _unstack_halves(m + jnp.log2(den), tq))):
                osc_ref[pat, half, dst, :] = o
                lsc_ref[pat, half, dst, :] = lse

        for sub in range(n_sub):
            sub_tile(sub)

    chunk = 2 * tq
    for r0 in range(0, tile, chunk):
        for half in range(2):
            ls = [lsc_ref[p, half, r0:r0 + chunk, :] for p in range(3)]
            top = jnp.maximum(jnp.maximum(ls[0], ls[1]), ls[2])
            es = [jnp.exp2(l - top) for l in ls]
            num = (es[0] * osc_ref[0, half, r0:r0 + chunk, :] + es[1] * osc_ref[1, half, r0:r0 + chunk, :]
                   + es[2] * osc_ref[2, half, r0:r0 + chunk, :])
            o_ref[r0:r0 + chunk, half * LANES:(half + 1) * LANES] = (
                num / (es[0] + es[1] + es[2])).astype(o_ref.dtype)


def _dilated(dil1, dil4, dil16, *, batch, seq, tile):
    width = DIL_HEADS * HEAD_DIM
    nt = seq // tile
    specs = []
    for arr in (dil1, dil4, dil16):
        dilation, length = arr.shape[1], arr.shape[2]
        specs.append(pl.BlockSpec((None, dilation, tile // dilation, width), lambda b, i: (b, 0, i, 0)))
        for part in (1, 2):
            specs.append(pl.BlockSpec((None, dilation, length, width),
                                      lambda b, i, part=part: (b, 0, 0, part)))
    return pl.pallas_call(
        functools.partial(_dilated_kernel, tile=tile, seq=seq),
        grid=(batch, nt),
        in_specs=specs,
        out_specs=pl.BlockSpec((tile, width), lambda b, i: (b * nt + i, 0)),
        out_shape=jax.ShapeDtypeStruct((batch * seq, width), BF16),
        scratch_shapes=[pltpu.VMEM((3, 2, tile, LANES), F32), pltpu.VMEM((3, 2, tile, LANES), F32)],
        compiler_params=_params("arbitrary", "arbitrary"),
        name="dilated",
    )(dil1, dil1, dil1, dil4, dil4, dil4, dil16, dil16, dil16)


SUBLANES = 8
ROW_BLOCK = SUBLANES * SUBLANES


def _swap_rows(x, stage_ref):
    rows, width = x.shape
    for c in range(width // LANES):
        stage_ref[c] = x[:, c * LANES:(c + 1) * LANES]
    cols = []
    for c in range(width // LANES):
        slabs = [stage_ref[c, pl.ds(b0 + k, SUBLANES, stride=SUBLANES), :]
                 for b0 in range(0, rows, ROW_BLOCK) for k in range(SUBLANES)]
        cols.append(jnp.concatenate(slabs, axis=0))
    return jnp.concatenate(cols, axis=1)


def _mix_ffn_kernel(x_ref, fox_ref, nsa_ref, dil_ref, wout_ref, gmix_ref,
                    gpre_ref, wup_ref, cw_ref, cb_ref, wdn_ref, gpost_ref, o_ref,
                    tail_ref, stage_ref, act_ref, *, tm, d_ff, chunk):
    @pl.when(pl.program_id(1) == 0)
    def _():
        tail_ref[...] = jnp.zeros_like(tail_ref)

    fw = FOX_HEADS * HEAD_DIM
    nw = NSA_HEADS * HEAD_DIM
    mix = (_dot(fox_ref[...], wout_ref[:fw, :]) + _dot(nsa_ref[...], wout_ref[fw:fw + nw, :])
           + _dot(dil_ref[...], wout_ref[fw + nw:, :]))
    x = x_ref[...] + _rms(mix, gmix_ref[...])
    h = _swap_rows(_rms(x, gpre_ref[...]), stage_ref).astype(BF16)
    row8 = lax.broadcasted_iota(jnp.int32, (SUBLANES, chunk), 0)

    def down_one(vreg_row, prev_last):
        return jnp.where(row8 == 0, prev_last, pltpu.roll(vreg_row, 1, 0))

    def conv(c0):
        u = _dot(h, wup_ref[:, c0:c0 + chunk])
        tail = tail_ref[:, c0:c0 + chunk]
        prev6, prev7 = tail[SUBLANES - 1:SUBLANES], tail[2 * SUBLANES - 1:]
        u1, u2 = [], []
        for b0 in range(0, tm, ROW_BLOCK):
            v = [u[b0 + k * SUBLANES:b0 + (k + 1) * SUBLANES] for k in range(SUBLANES)]
            s6, s7 = down_one(v[6], prev6), down_one(v[7], prev7)
            u1 += [s7] + v[:7]
            u2 += [s6, s7] + v[:6]
            prev6, prev7 = v[6][SUBLANES - 1:], v[7][SUBLANES - 1:]
        tail_ref[:, c0:c0 + chunk] = u[tm - 2 * SUBLANES:, :]
        return (cb_ref[:, c0:c0 + chunk] + jnp.concatenate(u2, axis=0) * cw_ref[0:1, c0:c0 + chunk]
                + jnp.concatenate(u1, axis=0) * cw_ref[1:2, c0:c0 + chunk] + u * cw_ref[2:3, c0:c0 + chunk])

    for c0 in range(0, d_ff, chunk):
        a = conv(c0)
        b = conv(d_ff + c0)
        act_ref[:, c0:c0 + chunk] = (a * jax.nn.sigmoid(a) * b).astype(BF16)
    y = _dot(act_ref[...], wdn_ref[...])
    o_ref[...] = x + _swap_rows(_rms(y, gpost_ref[...]), stage_ref)


def _mix_ffn(x, fox_o, nsa_o, dil_o, wout, gmix, gpre, wup, cw, cb, wdn, gpost, *, batch, seq, tm, chunk):
    n, d = x.shape
    d_ff = wdn.shape[0]
    nt = seq // tm
    tok = lambda width: pl.BlockSpec((tm, width), lambda b, i: (b * nt + i, 0))
    const = lambda a: pl.BlockSpec(a.shape, lambda b, i: (0, 0), pipeline_mode=pl.Buffered(1))
    return pl.pallas_call(
        functools.partial(_mix_ffn_kernel, tm=tm, d_ff=d_ff, chunk=chunk),
        grid=(batch, nt),
        in_specs=[tok(d), tok(fox_o.shape[1]), tok(nsa_o.shape[1]), tok(dil_o.shape[1]),
                  const(wout), const(gmix), const(gpre), const(wup), const(cw), const(cb),
                  const(wdn), const(gpost)],
        out_specs=tok(d),
        out_shape=jax.ShapeDtypeStruct((n, d), F32),
        scratch_shapes=[pltpu.VMEM((2 * SUBLANES, 2 * d_ff), F32), pltpu.VMEM((d // LANES, tm, LANES), F32),
                        pltpu.VMEM((tm, d_ff), BF16)],
        compiler_params=_params("arbitrary", "arbitrary"),
        name="mix_ffn",
    )(x, fox_o, nsa_o, dil_o, wout, gmix, gpre, wup, cw, cb, wdn, gpost)


def _rope_tables(positions):
    inv_freq = ROPE_THETA ** (-2.0 * jnp.arange(ROPE_HALF, dtype=F32) / ROPE_DIM)
    ang = positions.astype(F32).reshape(-1, 1) * inv_freq
    cos, sin = jnp.cos(ang), jnp.sin(ang)
    n = ang.shape[0]
    ones = jnp.ones((n, HEAD_DIM - ROPE_DIM), F32)
    zeros_h = jnp.zeros((n, ROPE_HALF), F32)
    zeros_r = jnp.zeros((n, HEAD_DIM - ROPE_DIM), F32)
    rc = jnp.concatenate([cos, cos, ones], axis=1)
    rs1 = jnp.concatenate([zeros_h, sin, zeros_r], axis=1)
    rs2 = jnp.concatenate([-sin, zeros_h, zeros_r], axis=1)
    tile2 = lambda t: jnp.concatenate([t, t], axis=1)
    return tile2(rc), tile2(rs1), tile2(rs2)


def _nsa_head_perm():
    cols = []
    for r in range(NSA_REP):
        for g in range(NSA_GROUPS):
            h = g * NSA_REP + r
            cols.extend(range(h * HEAD_DIM, (h + 1) * HEAD_DIM))
    return np.asarray(cols)


def _regroup_w_in(w_all):
    fw, nw, kvw, dw = FOX_HEADS * HEAD_DIM, NSA_HEADS * HEAD_DIM, NSA_GROUPS * HEAD_DIM, DIL_HEADS * HEAD_DIM
    sizes = [fw, fw, fw, FOX_HEADS, nw] + [kvw] * 6 + [3 * NSA_HEADS, dw, dw, dw]
    offs = np.concatenate([[0], np.cumsum(sizes)])
    part = lambda idx: np.arange(offs[idx], offs[idx + 1])
    fq, fk, fv, ff, nq = (part(t) for t in range(5))
    kvs = [part(t) for t in range(5, 11)]
    ng, dq, dk, dv = (part(t) for t in range(11, 15))
    pad = np.full((2 * LANES - FOX_HEADS - 3 * NSA_HEADS,), -1)
    src = np.concatenate([fq, fk, fv, nq[_nsa_head_perm()]] + kvs + [dq, dk, dv, ff, ng, pad])
    select = (jnp.arange(w_all.shape[2])[:, None] == jnp.asarray(src)[None, :]).astype(BF16)
    return jnp.einsum("ldc,cn->ldn", w_all.astype(BF16), select, preferred_element_type=BF16)


def _regroup_w_out(w):
    fw, nw = FOX_HEADS * HEAD_DIM, NSA_HEADS * HEAD_DIM
    return jnp.concatenate([w[:fw], w[fw:fw + nw][_nsa_head_perm()], w[fw + nw:]], axis=0).astype(BF16)


def _place_w1(w1):
    hidden = w1.shape[1]
    w = w1.reshape(CMP_BLOCK // CMP_STRIDE, CMP_STRIDE, HEAD_DIM, hidden)
    z = jnp.zeros_like(w)
    per_group = [jnp.concatenate([w, z], axis=2), jnp.concatenate([z, w], axis=2)]
    return jnp.stack(per_group).reshape(NSA_GROUPS, CMP_BLOCK // CMP_STRIDE, CMP_STRIDE * LANES,
                                        hidden).astype(BF16)


def _place_pos(p):
    z = jnp.zeros_like(p)
    row = jnp.concatenate([p, z], axis=1).reshape(1, -1)
    return jnp.concatenate([row, jnp.zeros((7, row.shape[1]), row.dtype)], axis=0)


def _place_w2(w2):
    z = jnp.zeros_like(w2)
    return jnp.stack([jnp.concatenate([w2, z], axis=1), jnp.concatenate([z, w2], axis=1)]).astype(BF16)


class _Tiles(NamedTuple):
    inproj: int
    key: int
    fox_q: int
    nsa_q: int
    dilated: int
    ffn: int
    ffn_cols: int


def _tiles(seq):
    base = min(4 * LANES, seq)
    tiles = _Tiles(inproj=min(2 * base, seq), key=base, fox_q=base, nsa_q=min(2 * LANES, base),
                   dilated=DIL_DILATIONS[-1] * DIL_SPAN, ffn=base, ffn_cols=2 * LANES)
    assert seq % tiles.dilated == 0 and seq % tiles.inproj == 0, "sequence must be a multiple of 2048"
    assert seq // SEL_BLOCK <= LANES, "selection blocks must fit one lane tile"
    return tiles


def kernel(x, positions, attn_pre_norm, attn_post_norm, ffn_pre_norm, ffn_post_norm, w_in, b_forget, b_nsa_gate, cmp_pos_k, cmp_w1_k, cmp_w2_k, cmp_pos_v, cmp_w1_v, cmp_w2_v, w_out, w_up, conv_w, conv_b, w_down):
    batch, seq, d = x.shape
    depth = w_in.shape[0]
    n = batch * seq
    tiles = _tiles(seq)

    rc, rs1, rs2 = _rope_tables(positions)
    w_in_k = _regroup_w_in(w_in)
    xf = x.reshape(n, d)
    for l in range(depth):
        bias = jnp.concatenate([b_forget[l], b_nsa_gate[l],
                                jnp.zeros((LANES - FOX_HEADS - 3 * NSA_HEADS,), F32)]).reshape(1, LANES)
        fox, nq, nkv, dil, dil4, dil16, aux, ck_t, xk, xv = _inproj(
            xf, attn_pre_norm[l].reshape(1, d), w_in_k[l], rc, rs1, rs2, bias,
            batch=batch, seq=seq, tm=tiles.inproj)
        fox_o = _fox(fox, ck_t, batch=batch, seq=seq, tq=tiles.fox_q, tk=tiles.key)

        kcmp, vcmp = _compress(
            xk, xv, _place_w1(cmp_w1_k[l]), _place_w2(cmp_w2_k[l]), _place_pos(cmp_pos_k[l]),
            _place_w1(cmp_w1_v[l]), _place_w2(cmp_w2_v[l]), _place_pos(cmp_pos_v[l]))
        nsa_o = _nsa(nq, nkv, kcmp, vcmp, aux, batch=batch, seq=seq, tq=tiles.nsa_q, tk=tiles.key)

        dil_o = _dilated(dil.reshape(batch, 1, seq, dil.shape[1]), dil4, dil16,
                         batch=batch, seq=seq, tile=tiles.dilated)

        xf = _mix_ffn(xf, fox_o, nsa_o, dil_o, _regroup_w_out(w_out[l]), attn_post_norm[l].reshape(1, d),
                      ffn_pre_norm[l].reshape(1, d), w_up[l].astype(BF16), conv_w[l],
                      conv_b[l].reshape(1, -1), w_down[l].astype(BF16), ffn_post_norm[l].reshape(1, d),
                      batch=batch, seq=seq, tm=tiles.ffn, chunk=tiles.ffn_cols)
    return xf.reshape(batch, seq, d)
```

```python
import functools
import math
from typing import NamedTuple

import numpy as np
import jax
import jax.numpy as jnp
from jax import lax
from jax.experimental import pallas as pl
from jax.experimental.pallas import tpu as pltpu

HEAD_DIM = 64
FOX_HEADS = 4
NSA_HEADS = 8
NSA_GROUPS = 2
NSA_REP = NSA_HEADS // NSA_GROUPS
DIL_HEADS = 4
DIL_DILATIONS = (1, 4, 16)
DIL_SPAN = 128
ROPE_THETA = 500000.0
ROPE_DIM = HEAD_DIM // 4
ROPE_HALF = ROPE_DIM // 2
CMP_BLOCK = 32
CMP_STRIDE = 16
SEL_BLOCK = 64
SEL_TOPK = 16
NSA_WINDOW = 512
CONV_WIDTH = 3
RMS_EPS = 1e-6
NEG_INF = -1e30
FORCE_SCORE = 1e9
ATTN_SCALE = HEAD_DIM ** -0.5
LOG2E = math.log2(math.e)
Q_SCALE = ATTN_SCALE * LOG2E
LANES = 128
VMEM_LIMIT = 56 * 1024 * 1024
FLASH_GROUPS = (3, 1)
NSA_CHAINS = 4

F32 = jnp.float32
BF16 = jnp.bfloat16


def _dot_nt(a, b):
    return lax.dot_general(a, b, (((1,), (1,)), ((), ())), preferred_element_type=F32)


def _dot(a, b):
    return jnp.dot(a, b, preferred_element_type=F32)


def _rms(x, g):
    return x * lax.rsqrt(jnp.mean(x * x, axis=-1, keepdims=True) + RMS_EPS) * g


def _params(*sem):
    return pltpu.CompilerParams(dimension_semantics=sem, vmem_limit_bytes=VMEM_LIMIT)


def _with_ones(v):
    return jnp.concatenate([v, jnp.ones((v.shape[0], LANES), v.dtype)], axis=1)


def _softmax_pv(s, v1):
    m = jnp.max(s, axis=-1, keepdims=True)
    return m, _dot(jnp.exp2((s - m).astype(BF16)), v1)


def _flash_tiles(n_before, first, score_fn, value_fn):
    state = [_softmax_pv(s, v1) for s, v1 in zip(score_fn(first, True), value_fn(first))]

    def step(j, state):
        out = []
        for s, v1, (m, acc) in zip(score_fn(j, False), value_fn(j), state):
            m_new = jnp.maximum(m, jnp.max(s, axis=-1, keepdims=True))
            p = jnp.exp2((s - m_new).astype(BF16))
            out.append((m_new, jnp.exp2(m - m_new) * acc + _dot(p, v1)))
        return tuple(out)

    state = tuple(state)
    done = 0
    for size in FLASH_GROUPS:
        def group(t, state, size=size, done=done):
            for u in range(size):
                state = step(done + t * size + u, state)
            return state

        trips = (n_before - done) // size
        state = lax.fori_loop(0, trips, group, state)
        done = done + trips * size
    return [acc for _, acc in state]


_ROPE_CHUNKS = {"fox": (), "nq": (0, 1, 2, 3), "nkv": (0, 2, 4), "dil": (0, 1, 2, 3)}
_QUERY_CHUNKS = {"fox": (0, 1), "nq": (0, 1, 2, 3), "nkv": (), "dil": (0, 1)}
_SEG_WIDTH = {"fox": 768, "nq": 512, "nkv": 768, "dil": 768}
_SEG_ORDER = ("fox", "nq", "nkv", "dil")
_AUX_OFFSET = sum(_SEG_WIDTH.values())
_W_IN_COLS = _AUX_OFFSET + 2 * LANES


def _inproj_kernel(x_ref, g_ref, w_ref, rc_ref, rs1_ref, rs2_ref, bias_ref,
                   fox_ref, nq_ref, nkv_ref, dil_ref, dil4_ref, dil16_ref, aux_ref, ck_ref,
                   xk_ref, xv_ref, carry_ref, stage_ref, *, tm):
    @pl.when(pl.program_id(1) == 0)
    def _():
        carry_ref[...] = jnp.zeros_like(carry_ref)

    h = _rms(x_ref[...], g_ref[...]).astype(BF16)
    rc, rs1, rs2 = rc_ref[...], rs1_ref[...], rs2_ref[...]
    outs = {"fox": fox_ref, "nq": nq_ref, "nkv": nkv_ref, "dil": dil_ref}
    col = 0
    for name in _SEG_ORDER:
        for c0 in range(0, _SEG_WIDTH[name], 2 * LANES):
            y = _dot(h, w_ref[:, col + c0:col + c0 + 2 * LANES])
            for half in range(2):
                chunk = c0 // LANES + half
                yc = y[:, half * LANES:(half + 1) * LANES]
                if chunk in _QUERY_CHUNKS[name]:
                    yc = yc * Q_SCALE
                if chunk in _ROPE_CHUNKS[name]:
                    yc = (yc * rc + pltpu.roll(yc, ROPE_HALF, 1) * rs1
                          + pltpu.roll(yc, LANES - ROPE_HALF, 1) * rs2)
                outs[name][:, chunk * LANES:(chunk + 1) * LANES] = yc.astype(BF16)
                if name == "nkv" and chunk < 2:
                    stage_ref[...] = yc
                    for t in range(CMP_STRIDE):
                        part = stage_ref[pl.ds(t, tm // CMP_STRIDE, stride=CMP_STRIDE), :]
                        (xk_ref, xv_ref)[chunk][:, t * LANES:(t + 1) * LANES] = part.astype(BF16)
                if name == "dil":
                    stage_ref[...] = yc
                    for dilation, ref in ((DIL_DILATIONS[1], dil4_ref), (DIL_DILATIONS[2], dil16_ref)):
                        for c in range(dilation):
                            part = stage_ref[pl.ds(c, tm // dilation, stride=dilation), :]
                            ref[c, :, chunk * LANES:(chunk + 1) * LANES] = part.astype(BF16)
        col += _SEG_WIDTH[name]

    ya = _dot(h, w_ref[:, _AUX_OFFSET:_AUX_OFFSET + 2 * LANES])[:, :LANES] + bias_ref[...]
    aux_ref[...] = jax.nn.sigmoid(ya)
    logf = jnp.minimum(ya, 0.0) - jnp.log1p(jnp.exp(-jnp.abs(ya)))
    csum = logf.T[:8, :]
    lane = lax.broadcasted_iota(jnp.int32, csum.shape, 1)
    shift = 1
    while shift < tm:
        csum = csum + jnp.where(lane >= shift, pltpu.roll(csum, shift, 1), 0.0)
        shift *= 2
    csum = csum + jnp.concatenate([carry_ref[...]] * (tm // LANES), axis=1)
    carry_ref[...] = jnp.broadcast_to(csum[:, tm - 1:tm], carry_ref.shape)
    ck_ref[...] = csum * LOG2E


def _inproj(x, g, w, rc, rs1, rs2, bias, *, batch, seq, tm):
    n, d = x.shape
    nt = seq // tm
    tok = lambda width: pl.BlockSpec((tm, width), lambda b, i: (b * nt + i, 0))
    const = lambda shape: pl.BlockSpec(shape, lambda b, i: (0, 0))
    cls = lambda dil: pl.BlockSpec((None, dil, tm // dil, 768), lambda b, i: (b, 0, i, 0))
    cls_shape = lambda dil: jax.ShapeDtypeStruct((batch, dil, seq // dil, 768), BF16)
    d4, d16 = DIL_DILATIONS[1], DIL_DILATIONS[2]
    blk16 = pl.BlockSpec((None, tm // CMP_STRIDE, CMP_STRIDE * LANES), lambda b, i: (b, i, 0))
    blk16_shape = jax.ShapeDtypeStruct((batch, seq // CMP_STRIDE, CMP_STRIDE * LANES), BF16)
    return pl.pallas_call(
        functools.partial(_inproj_kernel, tm=tm),
        grid=(batch, nt),
        in_specs=[tok(d), const((1, d)), const(w.shape), tok(LANES), tok(LANES), tok(LANES),
                  const((1, LANES))],
        out_specs=[tok(768), tok(512), tok(768), tok(768), cls(d4), cls(d16), tok(LANES),
                   pl.BlockSpec((None, 8, tm), lambda b, i: (b, 0, i)), blk16, blk16],
        out_shape=[jax.ShapeDtypeStruct((n, 768), BF16), jax.ShapeDtypeStruct((n, 512), BF16),
                   jax.ShapeDtypeStruct((n, 768), BF16), jax.ShapeDtypeStruct((n, 768), BF16),
                   cls_shape(d4), cls_shape(d16), jax.ShapeDtypeStruct((n, LANES), F32),
                   jax.ShapeDtypeStruct((batch, 8, seq), F32), blk16_shape, blk16_shape],
        scratch_shapes=[pltpu.VMEM((8, LANES), F32), pltpu.VMEM((tm, LANES), F32)],
        compiler_params=_params("arbitrary", "arbitrary"),
        name="inproj",
    )(x, g, w, rc, rs1, rs2, bias)


def _stack_heads(q, heads):
    lane = lax.broadcasted_iota(jnp.int32, (1, q.shape[1]), 1)
    blocks = []
    for h in range(heads):
        head = (lane >= h * HEAD_DIM) & (lane < (h + 1) * HEAD_DIM)
        blocks.append(jnp.where(head, q, jnp.zeros_like(q)))
    return jnp.concatenate(blocks, axis=0)


def _pair_values(v):
    return _with_ones(v[:, :LANES]), _with_ones(v[:, LANES:])


def _pair_pv(p, v_pair):
    half = p.shape[0] // 2
    return jnp.concatenate([_dot(p[:half], v_pair[0]), _dot(p[half:], v_pair[1])], axis=0)


def _unstack_pairs(x, tq):
    return jnp.concatenate(_unstack_halves(x, tq), axis=1)


def _unstack_halves(x, tq):
    lane = lax.broadcasted_iota(jnp.int32, (1, LANES), 1)
    return [jnp.where(lane < HEAD_DIM, x[(2 * c) * tq:(2 * c + 1) * tq],
                      x[(2 * c + 1) * tq:(2 * c + 2) * tq]) for c in range(2)]


def _fox_kernel(q_ref, k_ref, v_ref, ck_ref, o_ref, *, tq, tk):
    i = pl.program_id(1)
    start = i * tq
    qst = _stack_heads(q_ref[...], FOX_HEADS)
    jd = start // tk
    qpos = start + lax.broadcasted_iota(jnp.int32, (tq, 1), 0)
    kpos = jd * tk + lax.broadcasted_iota(jnp.int32, (1, tk), 1)
    diag_bias = jnp.where(kpos <= qpos, 0.0, NEG_INF).astype(F32)

    def scores(j, masked):
        ks0 = pl.multiple_of(j * tk, tk)
        k = k_ref[pl.ds(ks0, tk), :]
        out = []
        for h in range(FOX_HEADS):
            s = _dot_nt(qst[h * tq:(h + 1) * tq], k) - ck_ref[h:h + 1, pl.ds(ks0, tk)]
            out.append(s + diag_bias if masked else s)
        return out

    def values(j):
        v_pair = _pair_values(v_ref[pl.ds(pl.multiple_of(j * tk, tk), tk), :])
        return [v_pair[h // 2] for h in range(FOX_HEADS)]

    acc = jnp.concatenate(_flash_tiles(jd, jd, scores, values), axis=0)
    o_ref[...] = _unstack_pairs(acc[:, :LANES] / acc[:, LANES:], tq).astype(o_ref.dtype)


def _fox(fox, ck_t, *, batch, seq, tq, tk):
    n = fox.shape[0]
    nt = seq // tq
    width = FOX_HEADS * HEAD_DIM
    return pl.pallas_call(
        functools.partial(_fox_kernel, tq=tq, tk=tk),
        grid=(batch, nt),
        in_specs=[pl.BlockSpec((tq, width), lambda b, i: (b * nt + i, 0)),
                  pl.BlockSpec((seq, width), lambda b, i: (b, 1)),
                  pl.BlockSpec((seq, width), lambda b, i: (b, 2)),
                  pl.BlockSpec((None, 8, seq), lambda b, i: (b, 0, 0))],
        out_specs=pl.BlockSpec((tq, width), lambda b, i: (b * nt + i, 0)),
        out_shape=jax.ShapeDtypeStruct((n, width), BF16),
        compiler_params=_params("arbitrary", "arbitrary"),
        name="fox",
    )(fox, fox, fox, ck_t)


def _gelu_tanh(x):
    return 0.5 * x * (1.0 + jnp.tanh(math.sqrt(2.0 / math.pi) * (x + 0.044715 * (x * x * x))))


def _compress_kernel(xk_ref, xv_ref, w1k_ref, w2k_ref, pk_ref, w1v_ref, w2v_ref, pv_ref,
                     kc_ref, vc_ref):
    feat = CMP_STRIDE * LANES

    def run(x_ref, w1_ref, w2_ref, p_ref, o_ref):
        x = x_ref[...]
        n16 = x.shape[0]
        p = p_ref[...].astype(BF16)
        posb = (_dot(p[:, :feat], w1_ref[0, 0]) + _dot(p[:, feat:], w1_ref[0, 1]))[0:1, :]
        out = None
        for g in range(NSA_GROUPS):
            hid = _dot(x, w1_ref[g, 0]) + pltpu.roll(_dot(x, w1_ref[g, 1]), n16 - 1, 0) + posb
            a = _gelu_tanh(hid).astype(BF16)
            og = _dot(a, w2_ref[g])
            out = og if out is None else out + og
        o_ref[...] = out.astype(o_ref.dtype)

    run(xk_ref, w1k_ref, w2k_ref, pk_ref, kc_ref)
    run(xv_ref, w1v_ref, w2v_ref, pv_ref, vc_ref)


def _compress(xk, xv, w1k, w2k, pk, w1v, w2v, pv):
    batch, n16, feat = xk.shape
    xspec = pl.BlockSpec((None, n16, feat), lambda b: (b, 0, 0))
    full = lambda a: pl.BlockSpec(a.shape, lambda b: (0,) * a.ndim)
    ospec = pl.BlockSpec((None, n16, LANES), lambda b: (b, 0, 0))
    return pl.pallas_call(
        _compress_kernel,
        grid=(batch,),
        in_specs=[xspec, xspec, full(w1k), full(w2k), full(pk), full(w1v), full(w2v), full(pv)],
        out_specs=[ospec, ospec],
        out_shape=[jax.ShapeDtypeStruct((batch, n16, LANES), BF16)] * 2,
        compiler_params=_params("arbitrary"),
        name="nsa_compress",
    )(xk, xv, w1k, w2k, pk, w1v, w2v, pv)


def _nsa_kernel(q_ref, kcmp_ref, vcmp_ref, ks_ref, vs_ref, kw_ref, vw_ref, aux_ref, onehot_ref,
                o_ref, *, tq, tk, seq, chains):
    i = pl.program_id(1)
    start = i * tq
    n_cmp = kcmp_ref.shape[0]
    wspan = NSA_WINDOW + tq

    rows = NSA_HEADS * tq
    crow = rows // chains
    n_sel = seq // SEL_BLOCK

    lane1 = lax.broadcasted_iota(jnp.int32, (1, LANES), 1)
    qpos_col = start + (lax.broadcasted_iota(jnp.int32, (rows, 1), 0) & (tq - 1))
    qpos_row = start + (lax.broadcasted_iota(jnp.int32, (1, NSA_GROUPS * tq), 1) & (tq - 1))

    kcmp = kcmp_ref[...]
    vcmp = vcmp_ref[...]
    cmp_end = lax.broadcasted_iota(jnp.int32, (1, n_cmp), 1) * CMP_STRIDE + (CMP_BLOCK - 1)
    cmp_bias = jnp.where(cmp_end <= qpos_col[:tq], 0.0, NEG_INF).astype(F32)
    cs = lax.broadcasted_iota(jnp.int32, (n_cmp, LANES), 0) * CMP_STRIDE
    ss = lax.broadcasted_iota(jnp.int32, (n_cmp, LANES), 1) * SEL_BLOCK
    overlap = jnp.where((cs < ss + SEL_BLOCK) & (cs + CMP_BLOCK > ss), 1.0, 0.0).astype(F32)

    blk = lax.broadcasted_iota(jnp.int32, (LANES, NSA_GROUPS * tq), 0)
    cur = qpos_row // SEL_BLOCK
    forced = (blk == 0) | (blk == cur) | (blk == cur - 1)
    future = blk * SEL_BLOCK > qpos_row

    wstart = pl.multiple_of(jnp.maximum(start - NSA_WINDOW, 0), tq)
    kwin = kw_ref[pl.ds(wstart, wspan), :]
    vwin = vw_ref[pl.ds(wstart, wspan), :]
    kpos_w = wstart + lax.broadcasted_iota(jnp.int32, (1, wspan), 1)
    win_ok = (kpos_w <= qpos_col) & (kpos_w > qpos_col - NSA_WINDOW)

    jd = start // tk
    col_tk = jd * tk + lax.broadcasted_iota(jnp.int32, (1, tk), 1)
    diag_bias = jnp.concatenate(
        [jnp.where(col_tk <= qpos_col[:tq], 0.0, NEG_INF).astype(F32)] * (crow // tq), axis=0)

    q_blocks = []
    for g in range(NSA_GROUPS):
        grp = (lane1 >= g * HEAD_DIM) & (lane1 < (g + 1) * HEAD_DIM)
        for r in range(NSA_REP):
            qc = q_ref[:, r * LANES:(r + 1) * LANES]
            q_blocks.append(jnp.where(grp, qc, jnp.zeros_like(qc)))
    qst = jnp.concatenate(q_blocks, axis=0)

    s = _dot_nt(qst, kcmp) + jnp.concatenate([cmp_bias] * NSA_HEADS, axis=0)
    m = jnp.max(s, axis=-1, keepdims=True)
    e = jnp.exp2(s - m)
    has_key = qpos_col >= CMP_BLOCK - 1
    p = e * jnp.where(has_key, 1.0 / jnp.sum(e, axis=-1, keepdims=True), 0.0)
    o_cmp = _dot(p.astype(BF16), vcmp)

    p_sum = []
    for g in range(NSA_GROUPS):
        blocks = [p[(g * NSA_REP + r) * tq:(g * NSA_REP + r + 1) * tq] for r in range(NSA_REP)]
        p_sum.append((blocks[0] + blocks[1]) + (blocks[2] + blocks[3]))
    imp = jnp.dot(jnp.concatenate(p_sum, axis=0), overlap, preferred_element_type=F32,
                  precision=lax.Precision.HIGHEST)
    imp_t = jnp.concatenate([imp[g * tq:(g + 1) * tq].T for g in range(NSA_GROUPS)], axis=1)
    imp_t = jnp.where(future, NEG_INF, jnp.where(forced, FORCE_SCORE, imp_t))
    sub = 8
    vals = [imp_t[r0:r0 + sub, :] for r0 in range(0, n_sel, sub)]
    row8 = lax.broadcasted_iota(jnp.int32, vals[0].shape, 0)
    ranks = [jnp.zeros(v.shape, jnp.int32) for v in vals]
    vwin1 = _with_ones(vwin)
    accs = []
    for c in range(chains):
        s = jnp.where(win_ok[:crow], _dot_nt(qst[c * crow:(c + 1) * crow], kwin), NEG_INF)
        accs.append(_softmax_pv(s, vwin1)[1])
        for jp in range(c * n_sel // chains, (c + 1) * n_sel // chains):
            other = imp_t[jp:jp + 1, :]
            for t, v in enumerate(vals):
                if t > jp // sub:
                    beats = other >= v
                elif t < jp // sub:
                    beats = other > v
                else:
                    beats = (other > v) | ((other == v) & (row8 > jp % sub))
                ranks[t] = ranks[t] + jnp.where(beats, 1, 0)
    rank = jnp.concatenate(ranks, axis=0)
    acc = jnp.concatenate(accs, axis=0)
    o_win = acc[:, :LANES] / acc[:, LANES:]

    aux = aux_ref[...]
    gate = lambda head, branch: aux[:, FOX_HEADS + 3 * head + branch:FOX_HEADS + 3 * head + branch + 1]
    partial = [gate(head, 0) * o_cmp[head * tq:(head + 1) * tq]
               + gate(head, 2) * o_win[head * tq:(head + 1) * tq] for head in range(NSA_HEADS)]

    sb_blocks = []
    for g in range(NSA_GROUPS):
        chosen = rank[:, g * tq:(g + 1) * tq] < min(SEL_TOPK, n_sel)
        sb_t = jnp.concatenate([jnp.where(chosen, 0.0, NEG_INF).astype(F32),
                                jnp.full((LANES - n_sel, tq), NEG_INF, F32)], axis=0)
        sb_blocks.extend([sb_t.T.astype(BF16)] * NSA_REP)
    qa = jnp.concatenate([qst, jnp.concatenate(sb_blocks, axis=0)], axis=1)

    def sel_scores(j, masked):
        ks0 = pl.multiple_of(j * tk, tk)
        ka = jnp.concatenate([ks_ref[pl.ds(ks0, tk), :], onehot_ref[pl.ds(ks0, tk), :]], axis=1)
        scores = []
        for c in range(chains):
            s = _dot_nt(qa[c * crow:(c + 1) * crow], ka)
            if masked:
                s = s + diag_bias
            scores.append(s)
        return scores

    def sel_values(j):
        return [_with_ones(vs_ref[pl.ds(pl.multiple_of(j * tk, tk), tk), :])] * chains

    acc = jnp.concatenate(_flash_tiles(jd, jd, sel_scores, sel_values), axis=0)
    o_sel = acc[:, :LANES] / acc[:, LANES:]

    for r in range(NSA_REP):
        per_group = []
        for g in range(NSA_GROUPS):
            head = g * NSA_REP + r
            per_group.append(partial[head] + gate(head, 1) * o_sel[head * tq:(head + 1) * tq])
        o = jnp.where(lane1 < HEAD_DIM, per_group[0], per_group[1])
        o_ref[:, r * LANES:(r + 1) * LANES] = o.astype(o_ref.dtype)


def _nsa(nq, nkv, kcmp, vcmp, aux, *, batch, seq, tq, tk):
    n = nq.shape[0]
    onehot = (jnp.arange(seq)[:, None] // SEL_BLOCK == jnp.arange(LANES)[None, :]).astype(BF16)
    nt = seq // tq
    n16 = kcmp.shape[1]
    kv = lambda c: pl.BlockSpec((seq, LANES), lambda b, i: (b, c))
    cmp_spec = pl.BlockSpec((None, n16, LANES), lambda b, i: (b, 0, 0))
    width = NSA_HEADS * HEAD_DIM
    return pl.pallas_call(
        functools.partial(_nsa_kernel, tq=tq, tk=tk, seq=seq, chains=NSA_CHAINS),
        grid=(batch, nt),
        in_specs=[pl.BlockSpec((tq, width), lambda b, i: (b * nt + i, 0)),
                  cmp_spec, cmp_spec, kv(2), kv(3), kv(4), kv(5),
                  pl.BlockSpec((tq, LANES), lambda b, i: (b * nt + i, 0)),
                  pl.BlockSpec((seq, LANES), lambda b, i: (0, 0))],
        out_specs=pl.BlockSpec((tq, width), lambda b, i: (b * nt + i, 0)),
        out_shape=jax.ShapeDtypeStruct((n, width), BF16),
        compiler_params=_params("arbitrary", "arbitrary"),
        name="nsa",
    )(nq, kcmp, vcmp, nkv, nkv, nkv, nkv, aux, onehot)


def _dilated_kernel(q1_ref, k1_ref, v1_ref, q4_ref, k4_ref, v4_ref, q16_ref, k16_ref, v16_ref,
                    o_ref, osc_ref, lsc_ref, *, tile, seq):
    i = pl.program_id(1)
    tq = DIL_SPAN
    rows = DIL_HEADS * tq
    row_u = lax.broadcasted_iota(jnp.int32, (rows, 1), 0) & (tq - 1)
    refs = ((q1_ref, k1_ref, v1_ref), (q4_ref, k4_ref, v4_ref), (q16_ref, k16_ref, v16_ref))
    for pat, (dilation, (q_ref, k_ref, v_ref)) in enumerate(zip(DIL_DILATIONS, refs)):
        length = seq // dilation
        span = min(2 * tq, length)
        per_class = tile // (tq * dilation)
        n_sub = tile // tq
        base = i * (tile // dilation)
        rel = row_u - lax.broadcasted_iota(jnp.int32, (1, span), 1)

        def band(offset):
            return jnp.where((rel + offset >= 0) & (rel + offset <= DIL_SPAN), 0.0, NEG_INF).astype(F32)

        bias_shifted, bias_aligned = band(min(DIL_SPAN, length - span + DIL_SPAN)), band(0)

        def sub_tile(sub):
            c = sub // per_class
            w = sub % per_class
            u0 = w * tq
            ks0 = pl.multiple_of(jnp.maximum(base + u0 - DIL_SPAN, 0), tq)
            qst = _stack_heads(q_ref[c, pl.ds(u0, tq), :], DIL_HEADS)
            bias = jnp.where(base == 0, bias_aligned, bias_shifted) if u0 == 0 else bias_shifted
            s = _dot_nt(qst, k_ref[c, pl.ds(ks0, span), :]) + bias
            m = jnp.max(s, axis=-1, keepdims=True)
            acc = _pair_pv(jnp.exp2((s - m).astype(BF16)),
                           _pair_values(v_ref[c, pl.ds(ks0, span), :]))
            den = acc[:, LANES:]
            dst = pl.ds(u0 * dilation + c, tq, stride=dilation)
            for half, (o, lse) in enumerate(zip(_unstack_halves(acc[:, :LANES] / den, tq),
                                                _unstack_halves(m + jnp.log2(den), tq))):
                osc_ref[pat, half, dst, :] = o
                lsc_ref[pat, half, dst, :] = lse

        for sub in range(n_sub):
            sub_tile(sub)

    chunk = 2 * tq
    for r0 in range(0, tile, chunk):
        for half in range(2):
            ls = [lsc_ref[p, half, r0:r0 + chunk, :] for p in range(3)]
            top = jnp.maximum(jnp.maximum(ls[0], ls[1]), ls[2])
            es = [jnp.exp2(l - top) for l in ls]
            num = (es[0] * osc_ref[0, half, r0:r0 + chunk, :] + es[1] * osc_ref[1, half, r0:r0 + chunk, :]
                   + es[2] * osc_ref[2, half, r0:r0 + chunk, :])
            o_ref[r0:r0 + chunk, half * LANES:(half + 1) * LANES] = (
                num / (es[0] + es[1] + es[2])).astype(o_ref.dtype)


def _dilated(dil1, dil4, dil16, *, batch, seq, tile):
    width = DIL_HEADS * HEAD_DIM
    nt = seq // tile
    specs = []
    for arr in (dil1, dil4, dil16):
        dilation, length = arr.shape[1], arr.shape[2]
        specs.append(pl.BlockSpec((None, dilation, tile // dilation, width), lambda b, i: (b, 0, i, 0)))
        for part in (1, 2):
            specs.append(pl.BlockSpec((None, dilation, length, width),
                                      lambda b, i, part=part: (b, 0, 0, part)))
    return pl.pallas_call(
        functools.partial(_dilated_kernel, tile=tile, seq=seq),
        grid=(batch, nt),
        in_specs=specs,
        out_specs=pl.BlockSpec((tile, width), lambda b, i: (b * nt + i, 0)),
        out_shape=jax.ShapeDtypeStruct((batch * seq, width), BF16),
        scratch_shapes=[pltpu.VMEM((3, 2, tile, LANES), F32), pltpu.VMEM((3, 2, tile, LANES), F32)],
        compiler_params=_params("arbitrary", "arbitrary"),
        name="dilated",
    )(dil1, dil1, dil1, dil4, dil4, dil4, dil16, dil16, dil16)


SUBLANES = 8
ROW_BLOCK = SUBLANES * SUBLANES


def _swap_rows(x, stage_ref):
    rows, width = x.shape
    for c in range(width // LANES):
        stage_ref[c] = x[:, c * LANES:(c + 1) * LANES]
    cols = []
    for c in range(width // LANES):
        slabs = [stage_ref[c, pl.ds(b0 + k, SUBLANES, stride=SUBLANES), :]
                 for b0 in range(0, rows, ROW_BLOCK) for k in range(SUBLANES)]
        cols.append(jnp.concatenate(slabs, axis=0))
    return jnp.concatenate(cols, axis=1)


def _mix_ffn_kernel(x_ref, fox_ref, nsa_ref, dil_ref, wout_ref, gmix_ref,
                    gpre_ref, wup_ref, cw_ref, cb_ref, wdn_ref, gpost_ref, o_ref,
                    tail_ref, stage_ref, act_ref, *, tm, d_ff, chunk):
    @pl.when(pl.program_id(1) == 0)
    def _():
        tail_ref[...] = jnp.zeros_like(tail_ref)

    fw = FOX_HEADS * HEAD_DIM
    nw = NSA_HEADS * HEAD_DIM
    mix = (_dot(fox_ref[...], wout_ref[:fw, :]) + _dot(nsa_ref[...], wout_ref[fw:fw + nw, :])
           + _dot(dil_ref[...], wout_ref[fw + nw:, :]))
    x = x_ref[...] + _rms(mix, gmix_ref[...])
    h = _swap_rows(_rms(x, gpre_ref[...]), stage_ref).astype(BF16)
    row8 = lax.broadcasted_iota(jnp.int32, (SUBLANES, chunk), 0)

    def down_one(vreg_row, prev_last):
        return jnp.where(row8 == 0, prev_last, pltpu.roll(vreg_row, 1, 0))

    def conv(c0):
        u = _dot(h, wup_ref[:, c0:c0 + chunk])
        tail = tail_ref[:, c0:c0 + chunk]
        prev6, prev7 = tail[SUBLANES - 1:SUBLANES], tail[2 * SUBLANES - 1:]
        u1, u2 = [], []
        for b0 in range(0, tm, ROW_BLOCK):
            v = [u[b0 + k * SUBLANES:b0 + (k + 1) * SUBLANES] for k in range(SUBLANES)]
            s6, s7 = down_one(v[6], prev6), down_one(v[7], prev7)
            u1 += [s7] + v[:7]
            u2 += [s6, s7] + v[:6]
            prev6, prev7 = v[6][SUBLANES - 1:], v[7][SUBLANES - 1:]
        tail_ref[:, c0:c0 + chunk] = u[tm - 2 * SUBLANES:, :]
        return (cb_ref[:, c0:c0 + chunk] + jnp.concatenate(u2, axis=0) * cw_ref[0:1, c0:c0 + chunk]
                + jnp.concatenate(u1, axis=0) * cw_ref[1:2, c0:c0 + chunk] + u * cw_ref[2:3, c0:c0 + chunk])

    for c0 in range(0, d_ff, chunk):
        a = conv(c0)
        b = conv(d_ff + c0)
        act_ref[:, c0:c0 + chunk] = (a * jax.nn.sigmoid(a) * b).astype(BF16)
    y = _dot(act_ref[...], wdn_ref[...])
    o_ref[...] = x + _swap_rows(_rms(y, gpost_ref[...]), stage_ref)


def _mix_ffn(x, fox_o, nsa_o, dil_o, wout, gmix, gpre, wup, cw, cb, wdn, gpost, *, batch, seq, tm, chunk):
    n, d = x.shape
    d_ff = wdn.shape[0]
    nt = seq // tm
    tok = lambda width: pl.BlockSpec((tm, width), lambda b, i: (b * nt + i, 0))
    const = lambda a: pl.BlockSpec(a.shape, lambda b, i: (0, 0), pipeline_mode=pl.Buffered(1))
    return pl.pallas_call(
        functools.partial(_mix_ffn_kernel, tm=tm, d_ff=d_ff, chunk=chunk),
        grid=(batch, nt),
        in_specs=[tok(d), tok(fox_o.shape[1]), tok(nsa_o.shape[1]), tok(dil_o.shape[1]),
                  const(wout), const(gmix), const(gpre), const(wup), const(cw), const(cb),
                  const(wdn), const(gpost)],
        out_specs=tok(d),
        out_shape=jax.ShapeDtypeStruct((n, d), F32),
        scratch_shapes=[pltpu.VMEM((2 * SUBLANES, 2 * d_ff), F32), pltpu.VMEM((d // LANES, tm, LANES), F32),
                        pltpu.VMEM((tm, d_ff), BF16)],
        compiler_params=_params("arbitrary", "arbitrary"),
        name="mix_ffn",
    )(x, fox_o, nsa_o, dil_o, wout, gmix, gpre, wup, cw, cb, wdn, gpost)


def _rope_tables(positions):
    inv_freq = ROPE_THETA ** (-2.0 * jnp.arange(ROPE_HALF, dtype=F32) / ROPE_DIM)
    ang = positions.astype(F32).reshape(-1, 1) * inv_freq
    cos, sin = jnp.cos(ang), jnp.sin(ang)
    within = np.arange(LANES) % HEAD_DIM
    freq = np.arange(ROPE_HALF)[:, None]
    place = lambda hit: jnp.asarray(hit, F32)
    spread = lambda t, p: jnp.dot(t, p, precision=lax.Precision.HIGHEST)
    rc = spread(cos, place((within % ROPE_HALF == freq) & (within < ROPE_DIM))) + place(within >= ROPE_DIM)
    rs1 = spread(sin, place(within - ROPE_HALF == freq))
    rs2 = -spread(sin, place(within == freq))
    return rc, rs1, rs2


def _nsa_head_perm():
    cols = []
    for r in range(NSA_REP):
        for g in range(NSA_GROUPS):
            h = g * NSA_REP + r
            cols.extend(range(h * HEAD_DIM, (h + 1) * HEAD_DIM))
    return np.asarray(cols)


def _regroup_w_in(w_all):
    fw, nw, kvw, dw = FOX_HEADS * HEAD_DIM, NSA_HEADS * HEAD_DIM, NSA_GROUPS * HEAD_DIM, DIL_HEADS * HEAD_DIM
    sizes = [fw, fw, fw, FOX_HEADS, nw] + [kvw] * 6 + [3 * NSA_HEADS, dw, dw, dw]
    offs = np.concatenate([[0], np.cumsum(sizes)])
    part = lambda idx: np.arange(offs[idx], offs[idx + 1])
    fq, fk, fv, ff, nq = (part(t) for t in range(5))
    kvs = [part(t) for t in range(5, 11)]
    ng, dq, dk, dv = (part(t) for t in range(11, 15))
    pad = np.full((2 * LANES - FOX_HEADS - 3 * NSA_HEADS,), -1)
    src = np.concatenate([fq, fk, fv, nq[_nsa_head_perm()]] + kvs + [dq, dk, dv, ff, ng, pad])
    select = (jnp.arange(w_all.shape[2])[:, None] == jnp.asarray(src)[None, :]).astype(BF16)
    return jnp.einsum("ldc,cn->ldn", w_all.astype(BF16), select, preferred_element_type=BF16)


def _regroup_w_out(w):
    fw, nw = FOX_HEADS * HEAD_DIM, NSA_HEADS * HEAD_DIM
    return jnp.concatenate([w[:fw], w[fw:fw + nw][_nsa_head_perm()], w[fw + nw:]], axis=0).astype(BF16)


def _place_w1(w1):
    hidden = w1.shape[1]
    w = w1.reshape(CMP_BLOCK // CMP_STRIDE, CMP_STRIDE, HEAD_DIM, hidden)
    z = jnp.zeros_like(w)
    per_group = [jnp.concatenate([w, z], axis=2), jnp.concatenate([z, w], axis=2)]
    return jnp.stack(per_group).reshape(NSA_GROUPS, CMP_BLOCK // CMP_STRIDE, CMP_STRIDE * LANES,
                                        hidden).astype(BF16)


def _place_pos(p):
    z = jnp.zeros_like(p)
    row = jnp.concatenate([p, z], axis=1).reshape(1, -1)
    return jnp.concatenate([row, jnp.zeros((7, row.shape[1]), row.dtype)], axis=0)


def _place_w2(w2):
    z = jnp.zeros_like(w2)
    return jnp.stack([jnp.concatenate([w2, z], axis=1), jnp.concatenate([z, w2], axis=1)]).astype(BF16)


class _Tiles(NamedTuple):
    inproj: int
    key: int
    fox_q: int
    nsa_q: int
    dilated: int
    ffn: int
    ffn_cols: int


def _tiles(seq):
    base = min(4 * LANES, seq)
    tiles = _Tiles(inproj=min(2 * base, seq), key=base, fox_q=base, nsa_q=min(2 * LANES, base),
                   dilated=DIL_DILATIONS[-1] * DIL_SPAN, ffn=base, ffn_cols=2 * LANES)
    assert seq % tiles.dilated == 0 and seq % tiles.inproj == 0, "sequence must be a multiple of 2048"
    assert seq // SEL_BLOCK <= LANES, "selection blocks must fit one lane tile"
    return tiles


def kernel(x, positions, attn_pre_norm, attn_post_norm, ffn_pre_norm, ffn_post_norm, w_in, b_forget, b_nsa_gate, cmp_pos_k, cmp_w1_k, cmp_w2_k, cmp_pos_v, cmp_w1_v, cmp_w2_v, w_out, w_up, conv_w, conv_b, w_down):
    batch, seq, d = x.shape
    depth = w_in.shape[0]
    n = batch * seq
    tiles = _tiles(seq)

    rc, rs1, rs2 = _rope_tables(positions)
    w_in_k = _regroup_w_in(w_in)
    xf = x.reshape(n, d)
    for l in range(depth):
        bias = jnp.concatenate([b_forget[l], b_nsa_gate[l],
                                jnp.zeros((LANES - FOX_HEADS - 3 * NSA_HEADS,), F32)]).reshape(1, LANES)
        fox, nq, nkv, dil, dil4, dil16, aux, ck_t, xk, xv = _inproj(
            xf, attn_pre_norm[l].reshape(1, d), w_in_k[l], rc, rs1, rs2, bias,
            batch=batch, seq=seq, tm=tiles.inproj)
        fox_o = _fox(fox, ck_t, batch=batch, seq=seq, tq=tiles.fox_q, tk=tiles.key)

        kcmp, vcmp = _compress(
            xk, xv, _place_w1(cmp_w1_k[l]), _place_w2(cmp_w2_k[l]), _place_pos(cmp_pos_k[l]),
            _place_w1(cmp_w1_v[l]), _place_w2(cmp_w2_v[l]), _place_pos(cmp_pos_v[l]))
        nsa_o = _nsa(nq, nkv, kcmp, vcmp, aux, batch=batch, seq=seq, tq=tiles.nsa_q, tk=tiles.key)

        dil_o = _dilated(dil.reshape(batch, 1, seq, dil.shape[1]), dil4, dil16,
                         batch=batch, seq=seq, tile=tiles.dilated)

        xf = _mix_ffn(xf, fox_o, nsa_o, dil_o, _regroup_w_out(w_out[l]), attn_post_norm[l].reshape(1, d),
                      ffn_pre_norm[l].reshape(1, d), w_up[l].astype(BF16), conv_w[l],
                      conv_b[l].reshape(1, -1), w_down[l].astype(BF16), ffn_post_norm[l].reshape(1, d),
                      batch=batch, seq=seq, tm=tiles.ffn, chunk=tiles.ffn_cols)
    return xf.reshape(batch, seq, d)
```

```python
import functools
import math
from typing import NamedTuple

import numpy as np
import jax
import jax.numpy as jnp
from jax import lax
from jax.experimental import pallas as pl
from jax.experimental.pallas import tpu as pltpu

HEAD_DIM = 64
FOX_HEADS = 4
NSA_HEADS = 8
NSA_GROUPS = 2
NSA_REP = NSA_HEADS // NSA_GROUPS
DIL_HEADS = 4
DIL_DILATIONS = (1, 4, 16)
DIL_SPAN = 128
ROPE_THETA = 500000.0
ROPE_DIM = HEAD_DIM // 4
ROPE_HALF = ROPE_DIM // 2
CMP_BLOCK = 32
CMP_STRIDE = 16
SEL_BLOCK = 64
SEL_TOPK = 16
NSA_WINDOW = 512
CONV_WIDTH = 3
RMS_EPS = 1e-6
NEG_INF = -1e30
FORCE_SCORE = 1e9
ATTN_SCALE = HEAD_DIM ** -0.5
LOG2E = math.log2(math.e)
Q_SCALE = ATTN_SCALE * LOG2E
LANES = 128
VMEM_LIMIT = 56 * 1024 * 1024
FLASH_GROUPS = (3, 1)
NSA_CHAINS = 4

F32 = jnp.float32
BF16 = jnp.bfloat16


def _dot_nt(a, b):
    return lax.dot_general(a, b, (((1,), (1,)), ((), ())), preferred_element_type=F32)


def _dot(a, b):
    return jnp.dot(a, b, preferred_element_type=F32)


def _rms(x, g):
    return x * lax.rsqrt(jnp.mean(x * x, axis=-1, keepdims=True) + RMS_EPS) * g


def _params(*sem):
    return pltpu.CompilerParams(dimension_semantics=sem, vmem_limit_bytes=VMEM_LIMIT)


def _with_ones(v):
    return jnp.concatenate([v, jnp.ones((v.shape[0], LANES), v.dtype)], axis=1)


def _softmax_pv(s, v1):
    m = jnp.max(s, axis=-1, keepdims=True)
    return m, _dot(jnp.exp2((s - m).astype(BF16)), v1)


def _flash_tiles(n_before, first, score_fn, value_fn):
    state = [_softmax_pv(s, v1) for s, v1 in zip(score_fn(first, True), value_fn(first))]

    def step(j, state):
        out = []
        for s, v1, (m, acc) in zip(score_fn(j, False), value_fn(j), state):
            m_new = jnp.maximum(m, jnp.max(s, axis=-1, keepdims=True))
            p = jnp.exp2((s - m_new).astype(BF16))
            out.append((m_new, jnp.exp2(m - m_new) * acc + _dot(p, v1)))
        return tuple(out)

    state = tuple(state)
    done = 0
    for size in FLASH_GROUPS:
        def group(t, state, size=size, done=done):
            for u in range(size):
                state = step(done + t * size + u, state)
            return state

        trips = (n_before - done) // size
        state = lax.fori_loop(0, trips, group, state)
        done = done + trips * size
    return [acc for _, acc in state]


_ROPE_CHUNKS = {"fox": (), "nq": (0, 1, 2, 3), "nkv": (0, 2, 4), "dil": (0, 1, 2, 3)}
_QUERY_CHUNKS = {"fox": (0, 1), "nq": (0, 1, 2, 3), "nkv": (), "dil": (0, 1)}
_SEG_WIDTH = {"fox": 768, "nq": 512, "nkv": 768, "dil": 768}
_SEG_ORDER = ("fox", "nq", "nkv", "dil")
_AUX_OFFSET = sum(_SEG_WIDTH.values())
_W_IN_COLS = _AUX_OFFSET + 2 * LANES


def _inproj_kernel(x_ref, g_ref, w_ref, rc_ref, rs1_ref, rs2_ref, bias_ref,
                   fox_ref, nq_ref, nkv_ref, dil_ref, dil4_ref, dil16_ref, aux_ref, ck_ref,
                   xk_ref, xv_ref, carry_ref, stage_ref, *, tm):
    @pl.when(pl.program_id(1) == 0)
    def _():
        carry_ref[...] = jnp.zeros_like(carry_ref)

    h = _rms(x_ref[...], g_ref[...]).astype(BF16)
    rc, rs1, rs2 = rc_ref[...], rs1_ref[...], rs2_ref[...]
    outs = {"fox": fox_ref, "nq": nq_ref, "nkv": nkv_ref, "dil": dil_ref}
    col = 0
    for name in _SEG_ORDER:
        for c0 in range(0, _SEG_WIDTH[name], 2 * LANES):
            y = _dot(h, w_ref[:, col + c0:col + c0 + 2 * LANES])
            for half in range(2):
                chunk = c0 // LANES + half
                yc = y[:, half * LANES:(half + 1) * LANES]
                if chunk in _QUERY_CHUNKS[name]:
                    yc = yc * Q_SCALE
                if chunk in _ROPE_CHUNKS[name]:
                    yc = (yc * rc + pltpu.roll(yc, ROPE_HALF, 1) * rs1
                          + pltpu.roll(yc, LANES - ROPE_HALF, 1) * rs2)
                outs[name][:, chunk * LANES:(chunk + 1) * LANES] = yc.astype(BF16)
                if name == "nkv" and chunk < 2:
                    stage_ref[...] = yc
                    for t in range(CMP_STRIDE):
                        part = stage_ref[pl.ds(t, tm // CMP_STRIDE, stride=CMP_STRIDE), :]
                        (xk_ref, xv_ref)[chunk][:, t * LANES:(t + 1) * LANES] = part.astype(BF16)
                if name == "dil":
                    stage_ref[...] = yc
                    for dilation, ref in ((DIL_DILATIONS[1], dil4_ref), (DIL_DILATIONS[2], dil16_ref)):
                        for c in range(dilation):
                            part = stage_ref[pl.ds(c, tm // dilation, stride=dilation), :]
                            ref[c, :, chunk * LANES:(chunk + 1) * LANES] = part.astype(BF16)
        col += _SEG_WIDTH[name]

    ya = _dot(h, w_ref[:, _AUX_OFFSET:_AUX_OFFSET + 2 * LANES])[:, :LANES] + bias_ref[...]
    aux_ref[...] = jax.nn.sigmoid(ya)
    logf = jnp.minimum(ya, 0.0) - jnp.log1p(jnp.exp(-jnp.abs(ya)))
    csum = logf.T[:8, :]
    lane = lax.broadcasted_iota(jnp.int32, csum.shape, 1)
    shift = 1
    while shift < tm:
        csum = csum + jnp.where(lane >= shift, pltpu.roll(csum, shift, 1), 0.0)
        shift *= 2
    csum = csum + jnp.concatenate([carry_ref[...]] * (tm // LANES), axis=1)
    carry_ref[...] = jnp.broadcast_to(csum[:, tm - 1:tm], carry_ref.shape)
    ck_ref[...] = csum * LOG2E


def _inproj(x, g, w_all, rc, rs1, rs2, bias, *, layer, batch, seq, tm):
    n, d = x.shape
    nt = seq // tm
    tok = lambda width: pl.BlockSpec((tm, width), lambda b, i: (b * nt + i, 0))
    const = lambda shape: pl.BlockSpec(shape, lambda b, i: (0, 0))
    w_spec = pl.BlockSpec((None,) + w_all.shape[1:], lambda b, i: (layer, 0, 0))
    cls = lambda dil: pl.BlockSpec((None, dil, tm // dil, 768), lambda b, i: (b, 0, i, 0))
    cls_shape = lambda dil: jax.ShapeDtypeStruct((batch, dil, seq // dil, 768), BF16)
    d4, d16 = DIL_DILATIONS[1], DIL_DILATIONS[2]
    blk16 = pl.BlockSpec((None, tm // CMP_STRIDE, CMP_STRIDE * LANES), lambda b, i: (b, i, 0))
    blk16_shape = jax.ShapeDtypeStruct((batch, seq // CMP_STRIDE, CMP_STRIDE * LANES), BF16)
    return pl.pallas_call(
        functools.partial(_inproj_kernel, tm=tm),
        grid=(batch, nt),
        in_specs=[tok(d), const((1, d)), w_spec, tok(LANES), tok(LANES), tok(LANES),
                  const((1, LANES))],
        out_specs=[tok(768), tok(512), tok(768), tok(768), cls(d4), cls(d16), tok(LANES),
                   pl.BlockSpec((None, 8, tm), lambda b, i: (b, 0, i)), blk16, blk16],
        out_shape=[jax.ShapeDtypeStruct((n, 768), BF16), jax.ShapeDtypeStruct((n, 512), BF16),
                   jax.ShapeDtypeStruct((n, 768), BF16), jax.ShapeDtypeStruct((n, 768), BF16),
                   cls_shape(d4), cls_shape(d16), jax.ShapeDtypeStruct((n, LANES), F32),
                   jax.ShapeDtypeStruct((batch, 8, seq), F32), blk16_shape, blk16_shape],
        scratch_shapes=[pltpu.VMEM((8, LANES), F32), pltpu.VMEM((tm, LANES), F32)],
        compiler_params=_params("arbitrary", "arbitrary"),
        name="inproj",
    )(x, g, w_all, rc, rs1, rs2, bias)


def _stack_heads(q, heads):
    lane = lax.broadcasted_iota(jnp.int32, (1, q.shape[1]), 1)
    blocks = []
    for h in range(heads):
        head = (lane >= h * HEAD_DIM) & (lane < (h + 1) * HEAD_DIM)
        blocks.append(jnp.where(head, q, jnp.zeros_like(q)))
    return jnp.concatenate(blocks, axis=0)


def _pair_values(v):
    return _with_ones(v[:, :LANES]), _with_ones(v[:, LANES:])


def _pair_pv(p, v_pair):
    half = p.shape[0] // 2
    return jnp.concatenate([_dot(p[:half], v_pair[0]), _dot(p[half:], v_pair[1])], axis=0)


def _unstack_pairs(x, tq):
    return jnp.concatenate(_unstack_halves(x, tq), axis=1)


def _unstack_halves(x, tq):
    lane = lax.broadcasted_iota(jnp.int32, (1, LANES), 1)
    return [jnp.where(lane < HEAD_DIM, x[(2 * c) * tq:(2 * c + 1) * tq],
                      x[(2 * c + 1) * tq:(2 * c + 2) * tq]) for c in range(2)]


def _fox_kernel(q_ref, k_ref, v_ref, ck_ref, o_ref, *, tq, tk):
    i = pl.program_id(1)
    start = i * tq
    qst = _stack_heads(q_ref[...], FOX_HEADS)
    jd = start // tk
    qpos = start + lax.broadcasted_iota(jnp.int32, (tq, 1), 0)
    kpos = jd * tk + lax.broadcasted_iota(jnp.int32, (1, tk), 1)
    diag_bias = jnp.where(kpos <= qpos, 0.0, NEG_INF).astype(F32)

    def scores(j, masked):
        ks0 = pl.multiple_of(j * tk, tk)
        k = k_ref[pl.ds(ks0, tk), :]
        out = []
        for h in range(FOX_HEADS):
            s = _dot_nt(qst[h * tq:(h + 1) * tq], k) - ck_ref[h:h + 1, pl.ds(ks0, tk)]
            out.append(s + diag_bias if masked else s)
        return out

    def values(j):
        v_pair = _pair_values(v_ref[pl.ds(pl.multiple_of(j * tk, tk), tk), :])
        return [v_pair[h // 2] for h in range(FOX_HEADS)]

    acc = jnp.concatenate(_flash_tiles(jd, jd, scores, values), axis=0)
    o_ref[...] = _unstack_pairs(acc[:, :LANES] / acc[:, LANES:], tq).astype(o_ref.dtype)


def _fox(fox, ck_t, *, batch, seq, tq, tk):
    n = fox.shape[0]
    nt = seq // tq
    width = FOX_HEADS * HEAD_DIM
    return pl.pallas_call(
        functools.partial(_fox_kernel, tq=tq, tk=tk),
        grid=(batch, nt),
        in_specs=[pl.BlockSpec((tq, width), lambda b, i: (b * nt + i, 0)),
                  pl.BlockSpec((seq, width), lambda b, i: (b, 1)),
                  pl.BlockSpec((seq, width), lambda b, i: (b, 2)),
                  pl.BlockSpec((None, 8, seq), lambda b, i: (b, 0, 0))],
        out_specs=pl.BlockSpec((tq, width), lambda b, i: (b * nt + i, 0)),
        out_shape=jax.ShapeDtypeStruct((n, width), BF16),
        compiler_params=_params("arbitrary", "arbitrary"),
        name="fox",
    )(fox, fox, fox, ck_t)


def _gelu_tanh(x):
    return 0.5 * x * (1.0 + jnp.tanh(math.sqrt(2.0 / math.pi) * (x + 0.044715 * (x * x * x))))


def _compress_kernel(xk_ref, xv_ref, w1k_ref, w2k_ref, pk_ref, w1v_ref, w2v_ref, pv_ref,
                     kc_ref, vc_ref):
    feat = CMP_STRIDE * LANES

    def run(x_ref, w1_ref, w2_ref, p_ref, o_ref):
        x = x_ref[...]
        n16 = x.shape[0]
        p = p_ref[...].astype(BF16)
        posb = (_dot(p[:, :feat], w1_ref[0, 0]) + _dot(p[:, feat:], w1_ref[0, 1]))[0:1, :]
        out = None
        for g in range(NSA_GROUPS):
            hid = _dot(x, w1_ref[g, 0]) + pltpu.roll(_dot(x, w1_ref[g, 1]), n16 - 1, 0) + posb
            a = _gelu_tanh(hid).astype(BF16)
            og = _dot(a, w2_ref[g])
            out = og if out is None else out + og
        o_ref[...] = out.astype(o_ref.dtype)

    run(xk_ref, w1k_ref, w2k_ref, pk_ref, kc_ref)
    run(xv_ref, w1v_ref, w2v_ref, pv_ref, vc_ref)


def _compress(xk, xv, w1k, w2k, pk, w1v, w2v, pv):
    batch, n16, feat = xk.shape
    xspec = pl.BlockSpec((None, n16, feat), lambda b: (b, 0, 0))
    full = lambda a: pl.BlockSpec(a.shape, lambda b: (0,) * a.ndim)
    ospec = pl.BlockSpec((None, n16, LANES), lambda b: (b, 0, 0))
    return pl.pallas_call(
        _compress_kernel,
        grid=(batch,),
        in_specs=[xspec, xspec, full(w1k), full(w2k), full(pk), full(w1v), full(w2v), full(pv)],
        out_specs=[ospec, ospec],
        out_shape=[jax.ShapeDtypeStruct((batch, n16, LANES), BF16)] * 2,
        compiler_params=_params("arbitrary"),
        name="nsa_compress",
    )(xk, xv, w1k, w2k, pk, w1v, w2v, pv)


def _nsa_kernel(q_ref, kcmp_ref, vcmp_ref, ks_ref, vs_ref, kw_ref, vw_ref, aux_ref, onehot_ref,
                o_ref, *, tq, tk, seq, chains):
    i = pl.program_id(1)
    start = i * tq
    n_cmp = kcmp_ref.shape[0]
    wspan = NSA_WINDOW + tq

    rows = NSA_HEADS * tq
    crow = rows // chains
    n_sel = seq // SEL_BLOCK

    lane1 = lax.broadcasted_iota(jnp.int32, (1, LANES), 1)
    qpos_col = start + (lax.broadcasted_iota(jnp.int32, (rows, 1), 0) & (tq - 1))
    qpos_row = start + (lax.broadcasted_iota(jnp.int32, (1, NSA_GROUPS * tq), 1) & (tq - 1))

    kcmp = kcmp_ref[...]
    vcmp = vcmp_ref[...]
    cmp_end = lax.broadcasted_iota(jnp.int32, (1, n_cmp), 1) * CMP_STRIDE + (CMP_BLOCK - 1)
    cmp_bias = jnp.where(cmp_end <= qpos_col[:tq], 0.0, NEG_INF).astype(F32)
    cs = lax.broadcasted_iota(jnp.int32, (n_cmp, LANES), 0) * CMP_STRIDE
    ss = lax.broadcasted_iota(jnp.int32, (n_cmp, LANES), 1) * SEL_BLOCK
    overlap = jnp.where((cs < ss + SEL_BLOCK) & (cs + CMP_BLOCK > ss), 1.0, 0.0).astype(F32)

    blk = lax.broadcasted_iota(jnp.int32, (LANES, NSA_GROUPS * tq), 0)
    cur = qpos_row // SEL_BLOCK
    forced = (blk == 0) | (blk == cur) | (blk == cur - 1)
    future = blk * SEL_BLOCK > qpos_row

    wstart = pl.multiple_of(jnp.maximum(start - NSA_WINDOW, 0), tq)
    kwin = kw_ref[pl.ds(wstart, wspan), :]
    vwin = vw_ref[pl.ds(wstart, wspan), :]
    kpos_w = wstart + lax.broadcasted_iota(jnp.int32, (1, wspan), 1)
    win_ok = (kpos_w <= qpos_col) & (kpos_w > qpos_col - NSA_WINDOW)

    jd = start // tk
    col_tk = jd * tk + lax.broadcasted_iota(jnp.int32, (1, tk), 1)
    diag_bias = jnp.concatenate(
        [jnp.where(col_tk <= qpos_col[:tq], 0.0, NEG_INF).astype(F32)] * (crow // tq), axis=0)

    q_blocks = []
    for g in range(NSA_GROUPS):
        grp = (lane1 >= g * HEAD_DIM) & (lane1 < (g + 1) * HEAD_DIM)
        for r in range(NSA_REP):
            qc = q_ref[:, r * LANES:(r + 1) * LANES]
            q_blocks.append(jnp.where(grp, qc, jnp.zeros_like(qc)))
    qst = jnp.concatenate(q_blocks, axis=0)

    s = _dot_nt(qst, kcmp) + jnp.concatenate([cmp_bias] * NSA_HEADS, axis=0)
    m = jnp.max(s, axis=-1, keepdims=True)
    e = jnp.exp2(s - m)
    has_key = qpos_col >= CMP_BLOCK - 1
    p = e * jnp.where(has_key, 1.0 / jnp.sum(e, axis=-1, keepdims=True), 0.0)
    o_cmp = _dot(p.astype(BF16), vcmp)

    p_sum = []
    for g in range(NSA_GROUPS):
        blocks = [p[(g * NSA_REP + r) * tq:(g * NSA_REP + r + 1) * tq] for r in range(NSA_REP)]
        p_sum.append((blocks[0] + blocks[1]) + (blocks[2] + blocks[3]))
    imp = jnp.dot(jnp.concatenate(p_sum, axis=0), overlap, preferred_element_type=F32,
                  precision=lax.Precision.HIGHEST)
    imp_t = jnp.concatenate([imp[g * tq:(g + 1) * tq].T for g in range(NSA_GROUPS)], axis=1)
    imp_t = jnp.where(future, NEG_INF, jnp.where(forced, FORCE_SCORE, imp_t))
    sub = 8
    vals = [imp_t[r0:r0 + sub, :] for r0 in range(0, n_sel, sub)]
    row8 = lax.broadcasted_iota(jnp.int32, vals[0].shape, 0)
    ranks = [jnp.zeros(v.shape, jnp.int32) for v in vals]
    vwin1 = _with_ones(vwin)
    accs = []
    for c in range(chains):
        s = jnp.where(win_ok[:crow], _dot_nt(qst[c * crow:(c + 1) * crow], kwin), NEG_INF)
        accs.append(_softmax_pv(s, vwin1)[1])
        for jp in range(c * n_sel // chains, (c + 1) * n_sel // chains):
            other = imp_t[jp:jp + 1, :]
            for t, v in enumerate(vals):
                if t > jp // sub:
                    beats = other >= v
                elif t < jp // sub:
                    beats = other > v
                else:
                    beats = (other > v) | ((other == v) & (row8 > jp % sub))
                ranks[t] = ranks[t] + jnp.where(beats, 1, 0)
    rank = jnp.concatenate(ranks, axis=0)
    acc = jnp.concatenate(accs, axis=0)
    o_win = acc[:, :LANES] / acc[:, LANES:]

    aux = aux_ref[...]
    gate = lambda head, branch: aux[:, FOX_HEADS + 3 * head + branch:FOX_HEADS + 3 * head + branch + 1]
    partial = [gate(head, 0) * o_cmp[head * tq:(head + 1) * tq]
               + gate(head, 2) * o_win[head * tq:(head + 1) * tq] for head in range(NSA_HEADS)]

    sb_blocks = []
    for g in range(NSA_GROUPS):
        chosen = rank[:, g * tq:(g + 1) * tq] < min(SEL_TOPK, n_sel)
        sb_t = jnp.concatenate([jnp.where(chosen, 0.0, NEG_INF).astype(F32),
                                jnp.full((LANES - n_sel, tq), NEG_INF, F32)], axis=0)
        sb_blocks.extend([sb_t.T.astype(BF16)] * NSA_REP)
    qa = jnp.concatenate([qst, jnp.concatenate(sb_blocks, axis=0)], axis=1)

    def sel_scores(j, masked):
        ks0 = pl.multiple_of(j * tk, tk)
        ka = jnp.concatenate([ks_ref[pl.ds(ks0, tk), :], onehot_ref[pl.ds(ks0, tk), :]], axis=1)
        scores = []
        for c in range(chains):
            s = _dot_nt(qa[c * crow:(c + 1) * crow], ka)
            if masked:
                s = s + diag_bias
            scores.append(s)
        return scores

    def sel_values(j):
        return [_with_ones(vs_ref[pl.ds(pl.multiple_of(j * tk, tk), tk), :])] * chains

    acc = jnp.concatenate(_flash_tiles(jd, jd, sel_scores, sel_values), axis=0)
    o_sel = acc[:, :LANES] / acc[:, LANES:]

    for r in range(NSA_REP):
        per_group = []
        for g in range(NSA_GROUPS):
            head = g * NSA_REP + r
            per_group.append(partial[head] + gate(head, 1) * o_sel[head * tq:(head + 1) * tq])
        o = jnp.where(lane1 < HEAD_DIM, per_group[0], per_group[1])
        o_ref[:, r * LANES:(r + 1) * LANES] = o.astype(o_ref.dtype)


def _nsa(nq, nkv, kcmp, vcmp, aux, *, batch, seq, tq, tk):
    n = nq.shape[0]
    onehot = (jnp.arange(seq)[:, None] // SEL_BLOCK == jnp.arange(LANES)[None, :]).astype(BF16)
    nt = seq // tq
    n16 = kcmp.shape[1]
    kv = lambda c: pl.BlockSpec((seq, LANES), lambda b, i: (b, c))
    cmp_spec = pl.BlockSpec((None, n16, LANES), lambda b, i: (b, 0, 0))
    width = NSA_HEADS * HEAD_DIM
    return pl.pallas_call(
        functools.partial(_nsa_kernel, tq=tq, tk=tk, seq=seq, chains=NSA_CHAINS),
        grid=(batch, nt),
        in_specs=[pl.BlockSpec((tq, width), lambda b, i: (b * nt + i, 0)),
                  cmp_spec, cmp_spec, kv(2), kv(3), kv(4), kv(5),
                  pl.BlockSpec((tq, LANES), lambda b, i: (b * nt + i, 0)),
                  pl.BlockSpec((seq, LANES), lambda b, i: (0, 0))],
        out_specs=pl.BlockSpec((tq, width), lambda b, i: (b * nt + i, 0)),
        out_shape=jax.ShapeDtypeStruct((n, width), BF16),
        compiler_params=_params("arbitrary", "arbitrary"),
        name="nsa",
    )(nq, kcmp, vcmp, nkv, nkv, nkv, nkv, aux, onehot)


def _dilated_kernel(q1_ref, k1_ref, v1_ref, q4_ref, k4_ref, v4_ref, q16_ref, k16_ref, v16_ref,
                    o_ref, osc_ref, lsc_ref, *, tile, seq):
    i = pl.program_id(1)
    tq = DIL_SPAN
    rows = DIL_HEADS * tq
    row_u = lax.broadcasted_iota(jnp.int32, (rows, 1), 0) & (tq - 1)
    refs = ((q1_ref, k1_ref, v1_ref), (q4_ref, k4_ref, v4_ref), (q16_ref, k16_ref, v16_ref))
    for pat, (dilation, (q_ref, k_ref, v_ref)) in enumerate(zip(DIL_DILATIONS, refs)):
        length = seq // dilation
        span = min(2 * tq, length)
        per_class = tile // (tq * dilation)
        n_sub = tile // tq
        base = i * (tile // dilation)
        rel = row_u - lax.broadcasted_iota(jnp.int32, (1, span), 1)

        def band(offset):
            return jnp.where((rel + offset >= 0) & (rel + offset <= DIL_SPAN), 0.0, NEG_INF).astype(F32)

        bias_shifted, bias_aligned = band(min(DIL_SPAN, length - span + DIL_SPAN)), band(0)

        def sub_tile(sub):
            c = sub // per_class
            w = sub % per_class
            u0 = w * tq
            ks0 = pl.multiple_of(jnp.maximum(base + u0 - DIL_SPAN, 0), tq)
            qst = _stack_heads(q_ref[c, pl.ds(u0, tq), :], DIL_HEADS)
            bias = jnp.where(base == 0, bias_aligned, bias_shifted) if u0 == 0 else bias_shifted
            s = _dot_nt(qst, k_ref[c, pl.ds(ks0, span), :]) + bias
            m = jnp.max(s, axis=-1, keepdims=True)
            acc = _pair_pv(jnp.exp2((s - m).astype(BF16)),
                           _pair_values(v_ref[c, pl.ds(ks0, span), :]))
            den = acc[:, LANES:]
            dst = pl.ds(u0 * dilation + c, tq, stride=dilation)
            for half, (o, lse) in enumerate(zip(_unstack_halves(acc[:, :LANES] / den, tq),
                                                _unstack_halves(m + jnp.log2(den), tq))):
                osc_ref[pat, half, dst, :] = o
                lsc_ref[pat, half, dst, :] = lse

        for sub in range(n_sub):
            sub_tile(sub)

    chunk = 2 * tq
    for r0 in range(0, tile, chunk):
        for half in range(2):
            ls = [lsc_ref[p, half, r0:r0 + chunk, :] for p in range(3)]
            top = jnp.maximum(jnp.maximum(ls[0], ls[1]), ls[2])
            es = [jnp.exp2(l - top) for l in ls]
            num = (es[0] * osc_ref[0, half, r0:r0 + chunk, :] + es[1] * osc_ref[1, half, r0:r0 + chunk, :]
                   + es[2] * osc_ref[2, half, r0:r0 + chunk, :])
            o_ref[r0:r0 + chunk, half * LANES:(half + 1) * LANES] = (
                num / (es[0] + es[1] + es[2])).astype(o_ref.dtype)


def _dilated(dil1, dil4, dil16, *, batch, seq, tile):
    width = DIL_HEADS * HEAD_DIM
    nt = seq // tile
    specs = []
    for arr in (dil1, dil4, dil16):
        dilation, length = arr.shape[1], arr.shape[2]
        specs.append(pl.BlockSpec((None, dilation, tile // dilation, width), lambda b, i: (b, 0, i, 0)))
        for part in (1, 2):
            specs.append(pl.BlockSpec((None, dilation, length, width),
                                      lambda b, i, part=part: (b, 0, 0, part)))
    return pl.pallas_call(
        functools.partial(_dilated_kernel, tile=tile, seq=seq),
        grid=(batch, nt),
        in_specs=specs,
        out_specs=pl.BlockSpec((tile, width), lambda b, i: (b * nt + i, 0)),
        out_shape=jax.ShapeDtypeStruct((batch * seq, width), BF16),
        scratch_shapes=[pltpu.VMEM((3, 2, tile, LANES), F32), pltpu.VMEM((3, 2, tile, LANES), F32)],
        compiler_params=_params("arbitrary", "arbitrary"),
        name="dilated",
    )(dil1, dil1, dil1, dil4, dil4, dil4, dil16, dil16, dil16)


SUBLANES = 8
ROW_BLOCK = SUBLANES * SUBLANES


def _swap_rows(x, stage_ref):
    rows, width = x.shape
    for c in range(width // LANES):
        stage_ref[c] = x[:, c * LANES:(c + 1) * LANES]
    cols = []
    for c in range(width // LANES):
        slabs = [stage_ref[c, pl.ds(b0 + k, SUBLANES, stride=SUBLANES), :]
                 for b0 in range(0, rows, ROW_BLOCK) for k in range(SUBLANES)]
        cols.append(jnp.concatenate(slabs, axis=0))
    return jnp.concatenate(cols, axis=1)


def _mix_ffn_kernel(x_ref, fox_ref, nsa_ref, dil_ref, wout_ref, gmix_ref,
                    gpre_ref, wup_ref, cw_ref, cb_ref, wdn_ref, gpost_ref, o_ref,
                    tail_ref, stage_ref, act_ref, *, tm, d_ff, chunk):
    @pl.when(pl.program_id(1) == 0)
    def _():
        tail_ref[...] = jnp.zeros_like(tail_ref)

    fw = FOX_HEADS * HEAD_DIM
    nw = NSA_HEADS * HEAD_DIM
    mix = (_dot(fox_ref[...], wout_ref[:fw, :]) + _dot(nsa_ref[...], wout_ref[fw:fw + nw, :])
           + _dot(dil_ref[...], wout_ref[fw + nw:, :]))
    x = x_ref[...] + _rms(mix, gmix_ref[...])
    h = _swap_rows(_rms(x, gpre_ref[...]), stage_ref).astype(BF16)
    row8 = lax.broadcasted_iota(jnp.int32, (SUBLANES, chunk), 0)

    def down_one(vreg_row, prev_last):
        return jnp.where(row8 == 0, prev_last, pltpu.roll(vreg_row, 1, 0))

    def conv(c0):
        u = _dot(h, wup_ref[:, c0:c0 + chunk])
        tail = tail_ref[:, c0:c0 + chunk]
        prev6, prev7 = tail[SUBLANES - 1:SUBLANES], tail[2 * SUBLANES - 1:]
        u1, u2 = [], []
        for b0 in range(0, tm, ROW_BLOCK):
            v = [u[b0 + k * SUBLANES:b0 + (k + 1) * SUBLANES] for k in range(SUBLANES)]
            s6, s7 = down_one(v[6], prev6), down_one(v[7], prev7)
            u1 += [s7] + v[:7]
            u2 += [s6, s7] + v[:6]
            prev6, prev7 = v[6][SUBLANES - 1:], v[7][SUBLANES - 1:]
        tail_ref[:, c0:c0 + chunk] = u[tm - 2 * SUBLANES:, :]
        return (cb_ref[:, c0:c0 + chunk] + jnp.concatenate(u2, axis=0) * cw_ref[0:1, c0:c0 + chunk]
                + jnp.concatenate(u1, axis=0) * cw_ref[1:2, c0:c0 + chunk] + u * cw_ref[2:3, c0:c0 + chunk])

    for c0 in range(0, d_ff, chunk):
        a = conv(c0)
        b = conv(d_ff + c0)
        act_ref[:, c0:c0 + chunk] = (a * jax.nn.sigmoid(a) * b).astype(BF16)
    y = _dot(act_ref[...], wdn_ref[...])
    o_ref[...] = x + _swap_rows(_rms(y, gpost_ref[...]), stage_ref)


def _mix_ffn(x, fox_o, nsa_o, dil_o, wout, gmix, gpre, wup_all, cw, cb, wdn_all, gpost,
             *, layer, batch, seq, tm, chunk):
    n, d = x.shape
    d_ff = wdn_all.shape[1]
    nt = seq // tm
    tok = lambda width: pl.BlockSpec((tm, width), lambda b, i: (b * nt + i, 0))
    const = lambda a: pl.BlockSpec(a.shape, lambda b, i: (0, 0), pipeline_mode=pl.Buffered(1))
    stacked = lambda a: pl.BlockSpec((None,) + a.shape[1:], lambda b, i: (layer, 0, 0),
                                     pipeline_mode=pl.Buffered(1))
    return pl.pallas_call(
        functools.partial(_mix_ffn_kernel, tm=tm, d_ff=d_ff, chunk=chunk),
        grid=(batch, nt),
        in_specs=[tok(d), tok(fox_o.shape[1]), tok(nsa_o.shape[1]), tok(dil_o.shape[1]),
                  const(wout), const(gmix), const(gpre), stacked(wup_all), const(cw), const(cb),
                  stacked(wdn_all), const(gpost)],
        out_specs=tok(d),
        out_shape=jax.ShapeDtypeStruct((n, d), F32),
        scratch_shapes=[pltpu.VMEM((2 * SUBLANES, 2 * d_ff), F32), pltpu.VMEM((d // LANES, tm, LANES), F32),
                        pltpu.VMEM((tm, d_ff), BF16)],
        compiler_params=_params("arbitrary", "arbitrary"),
        name="mix_ffn",
    )(x, fox_o, nsa_o, dil_o, wout, gmix, gpre, wup_all, cw, cb, wdn_all, gpost)


def _rope_tables(positions):
    inv_freq = ROPE_THETA ** (-2.0 * jnp.arange(ROPE_HALF, dtype=F32) / ROPE_DIM)
    ang = positions.astype(F32).reshape(-1, 1) * inv_freq
    cos, sin = jnp.cos(ang), jnp.sin(ang)
    within = np.arange(LANES) % HEAD_DIM
    freq = np.arange(ROPE_HALF)[:, None]
    place = lambda hit: jnp.asarray(hit, F32)
    spread = lambda t, p: jnp.dot(t, p, precision=lax.Precision.HIGHEST)
    rc = spread(cos, place((within % ROPE_HALF == freq) & (within < ROPE_DIM))) + place(within >= ROPE_DIM)
    rs1 = spread(sin, place(within - ROPE_HALF == freq))
    rs2 = -spread(sin, place(within == freq))
    return rc, rs1, rs2


def _nsa_head_perm():
    cols = []
    for r in range(NSA_REP):
        for g in range(NSA_GROUPS):
            h = g * NSA_REP + r
            cols.extend(range(h * HEAD_DIM, (h + 1) * HEAD_DIM))
    return np.asarray(cols)


def _regroup_w_in(w_all):
    fw, nw, kvw, dw = FOX_HEADS * HEAD_DIM, NSA_HEADS * HEAD_DIM, NSA_GROUPS * HEAD_DIM, DIL_HEADS * HEAD_DIM
    sizes = [fw, fw, fw, FOX_HEADS, nw] + [kvw] * 6 + [3 * NSA_HEADS, dw, dw, dw]
    offs = np.concatenate([[0], np.cumsum(sizes)])
    part = lambda idx: np.arange(offs[idx], offs[idx + 1])
    fq, fk, fv, ff, nq = (part(t) for t in range(5))
    kvs = [part(t) for t in range(5, 11)]
    ng, dq, dk, dv = (part(t) for t in range(11, 15))
    pad = np.full((2 * LANES - FOX_HEADS - 3 * NSA_HEADS,), -1)
    src = np.concatenate([fq, fk, fv, nq[_nsa_head_perm()]] + kvs + [dq, dk, dv, ff, ng, pad])
    select = (jnp.arange(w_all.shape[2])[:, None] == jnp.asarray(src)[None, :]).astype(BF16)
    return jnp.einsum("ldc,cn->ldn", w_all.astype(BF16), select, preferred_element_type=BF16)


def _regroup_w_out(w):
    fw, nw = FOX_HEADS * HEAD_DIM, NSA_HEADS * HEAD_DIM
    return jnp.concatenate([w[:fw], w[fw:fw + nw][_nsa_head_perm()], w[fw + nw:]], axis=0).astype(BF16)


def _place_w1(w1):
    hidden = w1.shape[1]
    w = w1.reshape(CMP_BLOCK // CMP_STRIDE, CMP_STRIDE, HEAD_DIM, hidden)
    z = jnp.zeros_like(w)
    per_group = [jnp.concatenate([w, z], axis=2), jnp.concatenate([z, w], axis=2)]
    return jnp.stack(per_group).reshape(NSA_GROUPS, CMP_BLOCK // CMP_STRIDE, CMP_STRIDE * LANES,
                                        hidden).astype(BF16)


def _place_pos(p):
    z = jnp.zeros_like(p)
    row = jnp.concatenate([p, z], axis=1).reshape(1, -1)
    return jnp.concatenate([row, jnp.zeros((7, row.shape[1]), row.dtype)], axis=0)


def _place_w2(w2):
    z = jnp.zeros_like(w2)
    return jnp.stack([jnp.concatenate([w2, z], axis=1), jnp.concatenate([z, w2], axis=1)]).astype(BF16)


class _Tiles(NamedTuple):
    inproj: int
    key: int
    fox_q: int
    nsa_q: int
    dilated: int
    ffn: int
    ffn_cols: int


def _tiles(seq):
    base = min(4 * LANES, seq)
    tiles = _Tiles(inproj=min(2 * base, seq), key=base, fox_q=base, nsa_q=min(2 * LANES, base),
                   dilated=DIL_DILATIONS[-1] * DIL_SPAN, ffn=base, ffn_cols=2 * LANES)
    assert seq % tiles.dilated == 0 and seq % tiles.inproj == 0, "sequence must be a multiple of 2048"
    assert seq // SEL_BLOCK <= LANES, "selection blocks must fit one lane tile"
    return tiles


def kernel(x, positions, attn_pre_norm, attn_post_norm, ffn_pre_norm, ffn_post_norm, w_in, b_forget, b_nsa_gate, cmp_pos_k, cmp_w1_k, cmp_w2_k, cmp_pos_v, cmp_w1_v, cmp_w2_v, w_out, w_up, conv_w, conv_b, w_down):
    batch, seq, d = x.shape
    depth = w_in.shape[0]
    n = batch * seq
    tiles = _tiles(seq)

    rc, rs1, rs2 = _rope_tables(positions)
    w_in_k = _regroup_w_in(w_in)
    w_up_k, w_down_k = w_up.astype(BF16), w_down.astype(BF16)
    xf = x.reshape(n, d)
    for l in range(depth):
        bias = jnp.concatenate([b_forget[l], b_nsa_gate[l],
                                jnp.zeros((LANES - FOX_HEADS - 3 * NSA_HEADS,), F32)]).reshape(1, LANES)
        fox, nq, nkv, dil, dil4, dil16, aux, ck_t, xk, xv = _inproj(
            xf, attn_pre_norm[l].reshape(1, d), w_in_k, rc, rs1, rs2, bias,
            layer=l, batch=batch, seq=seq, tm=tiles.inproj)
        fox_o = _fox(fox, ck_t, batch=batch, seq=seq, tq=tiles.fox_q, tk=tiles.key)

        kcmp, vcmp = _compress(
            xk, xv, _place_w1(cmp_w1_k[l]), _place_w2(cmp_w2_k[l]), _place_pos(cmp_pos_k[l]),
            _place_w1(cmp_w1_v[l]), _place_w2(cmp_w2_v[l]), _place_pos(cmp_pos_v[l]))
        nsa_o = _nsa(nq, nkv, kcmp, vcmp, aux, batch=batch, seq=seq, tq=tiles.nsa_q, tk=tiles.key)

        dil_o = _dilated(dil.reshape(batch, 1, seq, dil.shape[1]), dil4, dil16,
                         batch=batch, seq=seq, tile=tiles.dilated)

        xf = _mix_ffn(xf, fox_o, nsa_o, dil_o, _regroup_w_out(w_out[l]), attn_post_norm[l].reshape(1, d),
                      ffn_pre_norm[l].reshape(1, d), w_up_k, conv_w[l],
                      conv_b[l].reshape(1, -1), w_down_k, ffn_post_norm[l].reshape(1, d),
                      layer=l, batch=batch, seq=seq, tm=tiles.ffn, chunk=tiles.ffn_cols)
    return xf.reshape(batch, seq, d)
```

```python
import functools
import math
from typing import NamedTuple

import numpy as np
import jax
import jax.numpy as jnp
from jax import lax
from jax.experimental import pallas as pl
from jax.experimental.pallas import tpu as pltpu

HEAD_DIM = 64
FOX_HEADS = 4
NSA_HEADS = 8
NSA_GROUPS = 2
NSA_REP = NSA_HEADS // NSA_GROUPS
DIL_HEADS = 4
DIL_DILATIONS = (1, 4, 16)
DIL_SPAN = 128
ROPE_THETA = 500000.0
ROPE_DIM = HEAD_DIM // 4
ROPE_HALF = ROPE_DIM // 2
CMP_BLOCK = 32
CMP_STRIDE = 16
SEL_BLOCK = 64
SEL_TOPK = 16
NSA_WINDOW = 512
CONV_WIDTH = 3
RMS_EPS = 1e-6
NEG_INF = -1e30
FORCE_SCORE = 1e9
ATTN_SCALE = HEAD_DIM ** -0.5
LOG2E = math.log2(math.e)
Q_SCALE = ATTN_SCALE * LOG2E
LANES = 128
VMEM_LIMIT = 56 * 1024 * 1024
FLASH_GROUPS = (3, 1)
NSA_CHAINS = 4

F32 = jnp.float32
BF16 = jnp.bfloat16


def _dot_nt(a, b):
    return lax.dot_general(a, b, (((1,), (1,)), ((), ())), preferred_element_type=F32)


def _dot(a, b):
    return jnp.dot(a, b, preferred_element_type=F32)


def _rms(x, g):
    return x * lax.rsqrt(jnp.mean(x * x, axis=-1, keepdims=True) + RMS_EPS) * g


def _params(*sem):
    return pltpu.CompilerParams(dimension_semantics=sem, vmem_limit_bytes=VMEM_LIMIT)


def _with_ones(v):
    return jnp.concatenate([v, jnp.ones((v.shape[0], LANES), v.dtype)], axis=1)


def _softmax_pv(s, v1):
    m = jnp.max(s, axis=-1, keepdims=True)
    return m, _dot(jnp.exp2((s - m).astype(BF16)), v1)


def _flash_tiles(n_before, first, score_fn, value_fn):
    state = [_softmax_pv(s, v1) for s, v1 in zip(score_fn(first, True), value_fn(first))]

    def step(j, state):
        out = []
        for s, v1, (m, acc) in zip(score_fn(j, False), value_fn(j), state):
            m_new = jnp.maximum(m, jnp.max(s, axis=-1, keepdims=True))
            p = jnp.exp2((s - m_new).astype(BF16))
            out.append((m_new, jnp.exp2(m - m_new) * acc + _dot(p, v1)))
        return tuple(out)

    state = tuple(state)
    done = 0
    for size in FLASH_GROUPS:
        def group(t, state, size=size, done=done):
            for u in range(size):
                state = step(done + t * size + u, state)
            return state

        trips = (n_before - done) // size
        state = lax.fori_loop(0, trips, group, state)
        done = done + trips * size
    return [acc for _, acc in state]


_ROPE_CHUNKS = {"fox": (), "nq": (0, 1, 2, 3), "nkv": (0, 2, 4), "dil": (0, 1, 2, 3)}
_QUERY_CHUNKS = {"fox": (0, 1), "nq": (0, 1, 2, 3), "nkv": (), "dil": (0, 1)}
_SEG_WIDTH = {"fox": 768, "nq": 512, "nkv": 768, "dil": 768}
_SEG_ORDER = ("fox", "nq", "nkv", "dil")
_AUX_OFFSET = sum(_SEG_WIDTH.values())
_W_IN_COLS = _AUX_OFFSET + 2 * LANES


def _inproj_kernel(x_ref, g_ref, w_ref, rc_ref, rs1_ref, rs2_ref, bias_ref,
                   fox_ref, nq_ref, nkv_ref, dil_ref, dil4_ref, dil16_ref, aux_ref, ck_ref,
                   xk_ref, xv_ref, carry_ref, stage_ref, *, tm):
    @pl.when(pl.program_id(1) == 0)
    def _():
        carry_ref[...] = jnp.zeros_like(carry_ref)

    h = _rms(x_ref[...], g_ref[...]).astype(BF16)
    rc, rs1, rs2 = rc_ref[...], rs1_ref[...], rs2_ref[...]
    outs = {"fox": fox_ref, "nq": nq_ref, "nkv": nkv_ref, "dil": dil_ref}
    col = 0
    for name in _SEG_ORDER:
        for c0 in range(0, _SEG_WIDTH[name], 2 * LANES):
            y = _dot(h, w_ref[:, col + c0:col + c0 + 2 * LANES])
            for half in range(2):
                chunk = c0 // LANES + half
                yc = y[:, half * LANES:(half + 1) * LANES]
                if chunk in _QUERY_CHUNKS[name]:
                    yc = yc * Q_SCALE
                if chunk in _ROPE_CHUNKS[name]:
                    yc = (yc * rc + pltpu.roll(yc, ROPE_HALF, 1) * rs1
                          + pltpu.roll(yc, LANES - ROPE_HALF, 1) * rs2)
                outs[name][:, chunk * LANES:(chunk + 1) * LANES] = yc.astype(BF16)
                if name == "nkv" and chunk < 2:
                    stage_ref[...] = yc
                    for t in range(CMP_STRIDE):
                        part = stage_ref[pl.ds(t, tm // CMP_STRIDE, stride=CMP_STRIDE), :]
                        (xk_ref, xv_ref)[chunk][:, t * LANES:(t + 1) * LANES] = part.astype(BF16)
                if name == "dil":
                    stage_ref[...] = yc
                    for dilation, ref in ((DIL_DILATIONS[1], dil4_ref), (DIL_DILATIONS[2], dil16_ref)):
                        for c in range(dilation):
                            part = stage_ref[pl.ds(c, tm // dilation, stride=dilation), :]
                            ref[c, :, chunk * LANES:(chunk + 1) * LANES] = part.astype(BF16)
        col += _SEG_WIDTH[name]

    ya = _dot(h, w_ref[:, _AUX_OFFSET:_AUX_OFFSET + 2 * LANES])[:, :LANES] + bias_ref[...]
    aux_ref[...] = jax.nn.sigmoid(ya)
    logf = jnp.minimum(ya, 0.0) - jnp.log1p(jnp.exp(-jnp.abs(ya)))
    csum = logf.T[:8, :]
    lane = lax.broadcasted_iota(jnp.int32, csum.shape, 1)
    shift = 1
    while shift < tm:
        csum = csum + jnp.where(lane >= shift, pltpu.roll(csum, shift, 1), 0.0)
        shift *= 2
    csum = csum + jnp.concatenate([carry_ref[...]] * (tm // LANES), axis=1)
    carry_ref[...] = jnp.broadcast_to(csum[:, tm - 1:tm], carry_ref.shape)
    ck_ref[...] = csum * LOG2E


def _inproj(x, g, w_all, rc, rs1, rs2, bias, *, layer, batch, seq, tm):
    n, d = x.shape
    nt = seq // tm
    tok = lambda width: pl.BlockSpec((tm, width), lambda b, i: (b * nt + i, 0))
    const = lambda shape: pl.BlockSpec(shape, lambda b, i: (0, 0))
    w_spec = pl.BlockSpec((None,) + w_all.shape[1:], lambda b, i: (layer, 0, 0))
    cls = lambda dil: pl.BlockSpec((None, dil, tm // dil, 768), lambda b, i: (b, 0, i, 0))
    cls_shape = lambda dil: jax.ShapeDtypeStruct((batch, dil, seq // dil, 768), BF16)
    d4, d16 = DIL_DILATIONS[1], DIL_DILATIONS[2]
    blk16 = pl.BlockSpec((None, tm // CMP_STRIDE, CMP_STRIDE * LANES), lambda b, i: (b, i, 0))
    blk16_shape = jax.ShapeDtypeStruct((batch, seq // CMP_STRIDE, CMP_STRIDE * LANES), BF16)
    return pl.pallas_call(
        functools.partial(_inproj_kernel, tm=tm),
        grid=(batch, nt),
        in_specs=[tok(d), const((1, d)), w_spec, tok(LANES), tok(LANES), tok(LANES),
                  const((1, LANES))],
        out_specs=[tok(768), tok(512), tok(768), tok(768), cls(d4), cls(d16), tok(LANES),
                   pl.BlockSpec((None, 8, tm), lambda b, i: (b, 0, i)), blk16, blk16],
        out_shape=[jax.ShapeDtypeStruct((n, 768), BF16), jax.ShapeDtypeStruct((n, 512), BF16),
                   jax.ShapeDtypeStruct((n, 768), BF16), jax.ShapeDtypeStruct((n, 768), BF16),
                   cls_shape(d4), cls_shape(d16), jax.ShapeDtypeStruct((n, LANES), F32),
                   jax.ShapeDtypeStruct((batch, 8, seq), F32), blk16_shape, blk16_shape],
        scratch_shapes=[pltpu.VMEM((8, LANES), F32), pltpu.VMEM((tm, LANES), F32)],
        compiler_params=_params("arbitrary", "arbitrary"),
        name="inproj",
    )(x, g, w_all, rc, rs1, rs2, bias)


def _stack_heads(q, heads):
    lane = lax.broadcasted_iota(jnp.int32, (1, q.shape[1]), 1)
    blocks = []
    for h in range(heads):
        head = (lane >= h * HEAD_DIM) & (lane < (h + 1) * HEAD_DIM)
        blocks.append(jnp.where(head, q, jnp.zeros_like(q)))
    return jnp.concatenate(blocks, axis=0)


def _pair_values(v):
    return _with_ones(v[:, :LANES]), _with_ones(v[:, LANES:])


def _pair_pv(p, v_pair):
    half = p.shape[0] // 2
    return jnp.concatenate([_dot(p[:half], v_pair[0]), _dot(p[half:], v_pair[1])], axis=0)


def _unstack_pairs(x, tq):
    return jnp.concatenate(_unstack_halves(x, tq), axis=1)


def _unstack_halves(x, tq):
    lane = lax.broadcasted_iota(jnp.int32, (1, LANES), 1)
    return [jnp.where(lane < HEAD_DIM, x[(2 * c) * tq:(2 * c + 1) * tq],
                      x[(2 * c + 1) * tq:(2 * c + 2) * tq]) for c in range(2)]


def _fox_kernel(q_ref, k_ref, v_ref, ck_ref, o_ref, *, tq, tk):
    i = pl.program_id(1)
    start = i * tq
    qst = _stack_heads(q_ref[...], FOX_HEADS)
    jd = start // tk
    qpos = start + lax.broadcasted_iota(jnp.int32, (tq, 1), 0)
    kpos = jd * tk + lax.broadcasted_iota(jnp.int32, (1, tk), 1)
    diag_bias = jnp.where(kpos <= qpos, 0.0, NEG_INF).astype(F32)

    def scores(j, masked):
        ks0 = pl.multiple_of(j * tk, tk)
        k = k_ref[pl.ds(ks0, tk), :]
        out = []
        for h in range(FOX_HEADS):
            s = _dot_nt(qst[h * tq:(h + 1) * tq], k) - ck_ref[h:h + 1, pl.ds(ks0, tk)]
            out.append(s + diag_bias if masked else s)
        return out

    def values(j):
        v_pair = _pair_values(v_ref[pl.ds(pl.multiple_of(j * tk, tk), tk), :])
        return [v_pair[h // 2] for h in range(FOX_HEADS)]

    acc = jnp.concatenate(_flash_tiles(jd, jd, scores, values), axis=0)
    o_ref[...] = _unstack_pairs(acc[:, :LANES] / acc[:, LANES:], tq).astype(o_ref.dtype)


def _fox(fox, ck_t, *, batch, seq, tq, tk):
    n = fox.shape[0]
    nt = seq // tq
    width = FOX_HEADS * HEAD_DIM
    return pl.pallas_call(
        functools.partial(_fox_kernel, tq=tq, tk=tk),
        grid=(batch, nt),
        in_specs=[pl.BlockSpec((tq, width), lambda b, i: (b * nt + i, 0)),
                  pl.BlockSpec((seq, width), lambda b, i: (b, 1)),
                  pl.BlockSpec((seq, width), lambda b, i: (b, 2)),
                  pl.BlockSpec((None, 8, seq), lambda b, i: (b, 0, 0))],
        out_specs=pl.BlockSpec((tq, width), lambda b, i: (b * nt + i, 0)),
        out_shape=jax.ShapeDtypeStruct((n, width), BF16),
        compiler_params=_params("arbitrary", "arbitrary"),
        name="fox",
    )(fox, fox, fox, ck_t)


def _gelu_tanh(x):
    return 0.5 * x * (1.0 + jnp.tanh(math.sqrt(2.0 / math.pi) * (x + 0.044715 * (x * x * x))))


def _compress_kernel(xk_ref, xv_ref, w1k_ref, w2k_ref, pk_ref, w1v_ref, w2v_ref, pv_ref,
                     kc_ref, vc_ref):
    feat = CMP_STRIDE * LANES

    def run(x_ref, w1_ref, w2_ref, p_ref, o_ref):
        x = x_ref[...]
        n16 = x.shape[0]
        p = p_ref[...].astype(BF16)
        posb = (_dot(p[:, :feat], w1_ref[0, 0]) + _dot(p[:, feat:], w1_ref[0, 1]))[0:1, :]
        out = None
        for g in range(NSA_GROUPS):
            hid = _dot(x, w1_ref[g, 0]) + pltpu.roll(_dot(x, w1_ref[g, 1]), n16 - 1, 0) + posb
            a = _gelu_tanh(hid).astype(BF16)
            og = _dot(a, w2_ref[g])
            out = og if out is None else out + og
        o_ref[...] = out.astype(o_ref.dtype)

    run(xk_ref, w1k_ref, w2k_ref, pk_ref, kc_ref)
    run(xv_ref, w1v_ref, w2v_ref, pv_ref, vc_ref)


def _compress(xk, xv, w1k, w2k, pk, w1v, w2v, pv):
    batch, n16, feat = xk.shape
    xspec = pl.BlockSpec((None, n16, feat), lambda b: (b, 0, 0))
    full = lambda a: pl.BlockSpec(a.shape, lambda b: (0,) * a.ndim)
    ospec = pl.BlockSpec((None, n16, LANES), lambda b: (b, 0, 0))
    return pl.pallas_call(
        _compress_kernel,
        grid=(batch,),
        in_specs=[xspec, xspec, full(w1k), full(w2k), full(pk), full(w1v), full(w2v), full(pv)],
        out_specs=[ospec, ospec],
        out_shape=[jax.ShapeDtypeStruct((batch, n16, LANES), BF16)] * 2,
        compiler_params=_params("arbitrary"),
        name="nsa_compress",
    )(xk, xv, w1k, w2k, pk, w1v, w2v, pv)


def _nsa_stack_q(q_ref):
    lane1 = lax.broadcasted_iota(jnp.int32, (1, LANES), 1)
    q_blocks = []
    for g in range(NSA_GROUPS):
        grp = (lane1 >= g * HEAD_DIM) & (lane1 < (g + 1) * HEAD_DIM)
        for r in range(NSA_REP):
            qc = q_ref[:, r * LANES:(r + 1) * LANES]
            q_blocks.append(jnp.where(grp, qc, jnp.zeros_like(qc)))
    return jnp.concatenate(q_blocks, axis=0)


def _nsa_window_kernel(q_ref, kw_ref, vw_ref, aux_ref, o_ref, *, tq, chains):
    start = pl.program_id(1) * tq
    rows = NSA_HEADS * tq
    crow = rows // chains
    wspan = NSA_WINDOW + tq
    qpos = start + (lax.broadcasted_iota(jnp.int32, (crow, 1), 0) & (tq - 1))
    wstart = pl.multiple_of(jnp.maximum(start - NSA_WINDOW, 0), tq)
    kwin = kw_ref[pl.ds(wstart, wspan), :]
    vwin1 = _with_ones(vw_ref[pl.ds(wstart, wspan), :])
    kpos = wstart + lax.broadcasted_iota(jnp.int32, (1, wspan), 1)
    bias = jnp.where((kpos <= qpos) & (kpos > qpos - NSA_WINDOW), 0.0, NEG_INF).astype(F32)
    qst = _nsa_stack_q(q_ref)
    acc = jnp.concatenate([_softmax_pv(_dot_nt(qst[c * crow:(c + 1) * crow], kwin) + bias, vwin1)[1]
                           for c in range(chains)], axis=0)
    o_win = acc[:, :LANES] / acc[:, LANES:]
    aux = aux_ref[...]
    lane1 = lax.broadcasted_iota(jnp.int32, (1, LANES), 1)
    for r in range(NSA_REP):
        per_group = []
        for g in range(NSA_GROUPS):
            head = g * NSA_REP + r
            c = FOX_HEADS + 3 * head + 2
            per_group.append(aux[:, c:c + 1] * o_win[head * tq:(head + 1) * tq])
        o = jnp.where(lane1 < HEAD_DIM, per_group[0], per_group[1])
        o_ref[:, r * LANES:(r + 1) * LANES] = o.astype(o_ref.dtype)


def _nsa_window(nq, nkv, aux, *, batch, seq, tq):
    n = nq.shape[0]
    nt = seq // tq
    width = NSA_HEADS * HEAD_DIM
    kv = lambda c: pl.BlockSpec((seq, LANES), lambda b, i: (b, c))
    return pl.pallas_call(
        functools.partial(_nsa_window_kernel, tq=tq, chains=NSA_CHAINS),
        grid=(batch, nt),
        in_specs=[pl.BlockSpec((tq, width), lambda b, i: (b * nt + i, 0)), kv(4), kv(5),
                  pl.BlockSpec((tq, LANES), lambda b, i: (b * nt + i, 0))],
        out_specs=pl.BlockSpec((tq, width), lambda b, i: (b * nt + i, 0)),
        out_shape=jax.ShapeDtypeStruct((n, width), BF16),
        compiler_params=_params("arbitrary", "arbitrary"),
        name="nsa_window",
    )(nq, nkv, nkv, aux)


def _nsa_kernel(q_ref, kcmp_ref, vcmp_ref, ks_ref, vs_ref, win_ref, aux_ref, onehot_ref,
                o_ref, *, tq, tk, seq, chains):
    i = pl.program_id(1)
    start = i * tq
    n_cmp = kcmp_ref.shape[0]
    wspan = NSA_WINDOW + tq

    rows = NSA_HEADS * tq
    crow = rows // chains
    n_sel = seq // SEL_BLOCK

    lane1 = lax.broadcasted_iota(jnp.int32, (1, LANES), 1)
    qpos_col = start + (lax.broadcasted_iota(jnp.int32, (rows, 1), 0) & (tq - 1))
    qpos_row = start + (lax.broadcasted_iota(jnp.int32, (1, NSA_GROUPS * tq), 1) & (tq - 1))

    kcmp = kcmp_ref[...]
    vcmp = vcmp_ref[...]
    cmp_end = lax.broadcasted_iota(jnp.int32, (1, n_cmp), 1) * CMP_STRIDE + (CMP_BLOCK - 1)
    cmp_bias = jnp.where(cmp_end <= qpos_col[:tq], 0.0, NEG_INF).astype(F32)
    cs = lax.broadcasted_iota(jnp.int32, (n_cmp, LANES), 0) * CMP_STRIDE
    ss = lax.broadcasted_iota(jnp.int32, (n_cmp, LANES), 1) * SEL_BLOCK
    overlap = jnp.where((cs < ss + SEL_BLOCK) & (cs + CMP_BLOCK > ss), 1.0, 0.0).astype(F32)

    blk = lax.broadcasted_iota(jnp.int32, (LANES, NSA_GROUPS * tq), 0)
    cur = qpos_row // SEL_BLOCK
    forced = (blk == 0) | (blk == cur) | (blk == cur - 1)
    future = blk * SEL_BLOCK > qpos_row

    jd = start // tk
    col_tk = jd * tk + lax.broadcasted_iota(jnp.int32, (1, tk), 1)
    diag_bias = jnp.concatenate(
        [jnp.where(col_tk <= qpos_col[:tq], 0.0, NEG_INF).astype(F32)] * (crow // tq), axis=0)

    qst = _nsa_stack_q(q_ref)

    s = _dot_nt(qst, kcmp) + jnp.concatenate([cmp_bias] * NSA_HEADS, axis=0)
    m = jnp.max(s, axis=-1, keepdims=True)
    e = jnp.exp2(s - m)
    has_key = qpos_col >= CMP_BLOCK - 1
    p = e * jnp.where(has_key, 1.0 / jnp.sum(e, axis=-1, keepdims=True), 0.0)
    o_cmp = _dot(p.astype(BF16), vcmp)

    p_sum = []
    for g in range(NSA_GROUPS):
        blocks = [p[(g * NSA_REP + r) * tq:(g * NSA_REP + r + 1) * tq] for r in range(NSA_REP)]
        p_sum.append((blocks[0] + blocks[1]) + (blocks[2] + blocks[3]))
    imp = jnp.dot(jnp.concatenate(p_sum, axis=0), overlap, preferred_element_type=F32,
                  precision=lax.Precision.HIGHEST)
    imp_t = jnp.concatenate([imp[g * tq:(g + 1) * tq].T for g in range(NSA_GROUPS)], axis=1)
    imp_t = jnp.where(future, NEG_INF, jnp.where(forced, FORCE_SCORE, imp_t))
    sub = 8
    vals = [imp_t[r0:r0 + sub, :] for r0 in range(0, n_sel, sub)]
    row8 = lax.broadcasted_iota(jnp.int32, vals[0].shape, 0)
    ranks = [jnp.zeros(v.shape, jnp.int32) for v in vals]
    for jp in range(n_sel):
        other = imp_t[jp:jp + 1, :]
        for t, v in enumerate(vals):
            if t > jp // sub:
                beats = other >= v
            elif t < jp // sub:
                beats = other > v
            else:
                beats = (other > v) | ((other == v) & (row8 > jp % sub))
            ranks[t] = ranks[t] + jnp.where(beats, 1, 0)
    rank = jnp.concatenate(ranks, axis=0)

    aux = aux_ref[...]
    gate = lambda head, branch: aux[:, FOX_HEADS + 3 * head + branch:FOX_HEADS + 3 * head + branch + 1]
    partial = [gate(head, 0) * o_cmp[head * tq:(head + 1) * tq] for head in range(NSA_HEADS)]

    sb_blocks = []
    for g in range(NSA_GROUPS):
        chosen = rank[:, g * tq:(g + 1) * tq] < min(SEL_TOPK, n_sel)
        sb_t = jnp.concatenate([jnp.where(chosen, 0.0, NEG_INF).astype(F32),
                                jnp.full((LANES - n_sel, tq), NEG_INF, F32)], axis=0)
        sb_blocks.extend([sb_t.T.astype(BF16)] * NSA_REP)
    qa = jnp.concatenate([qst, jnp.concatenate(sb_blocks, axis=0)], axis=1)

    def sel_scores(j, masked):
        ks0 = pl.multiple_of(j * tk, tk)
        ka = jnp.concatenate([ks_ref[pl.ds(ks0, tk), :], onehot_ref[pl.ds(ks0, tk), :]], axis=1)
        scores = []
        for c in range(chains):
            s = _dot_nt(qa[c * crow:(c + 1) * crow], ka)
            if masked:
                s = s + diag_bias
            scores.append(s)
        return scores

    def sel_values(j):
        return [_with_ones(vs_ref[pl.ds(pl.multiple_of(j * tk, tk), tk), :])] * chains

    acc = jnp.concatenate(_flash_tiles(jd, jd, sel_scores, sel_values), axis=0)
    o_sel = acc[:, :LANES] / acc[:, LANES:]

    for r in range(NSA_REP):
        per_group = []
        for g in range(NSA_GROUPS):
            head = g * NSA_REP + r
            per_group.append(partial[head] + gate(head, 1) * o_sel[head * tq:(head + 1) * tq])
        o = (jnp.where(lane1 < HEAD_DIM, per_group[0], per_group[1])
             + win_ref[:, r * LANES:(r + 1) * LANES].astype(F32))
        o_ref[:, r * LANES:(r + 1) * LANES] = o.astype(o_ref.dtype)


def _nsa(nq, nkv, kcmp, vcmp, win, aux, *, batch, seq, tq, tk):
    n = nq.shape[0]
    onehot = (jnp.arange(seq)[:, None] // SEL_BLOCK == jnp.arange(LANES)[None, :]).astype(BF16)
    nt = seq // tq
    n16 = kcmp.shape[1]
    kv = lambda c: pl.BlockSpec((seq, LANES), lambda b, i: (b, c))
    cmp_spec = pl.BlockSpec((None, n16, LANES), lambda b, i: (b, 0, 0))
    width = NSA_HEADS * HEAD_DIM
    return pl.pallas_call(
        functools.partial(_nsa_kernel, tq=tq, tk=tk, seq=seq, chains=NSA_CHAINS),
        grid=(batch, nt),
        in_specs=[pl.BlockSpec((tq, width), lambda b, i: (b * nt + i, 0)),
                  cmp_spec, cmp_spec, kv(2), kv(3),
                  pl.BlockSpec((tq, width), lambda b, i: (b * nt + i, 0)),
                  pl.BlockSpec((tq, LANES), lambda b, i: (b * nt + i, 0)),
                  pl.BlockSpec((seq, LANES), lambda b, i: (0, 0))],
        out_specs=pl.BlockSpec((tq, width), lambda b, i: (b * nt + i, 0)),
        out_shape=jax.ShapeDtypeStruct((n, width), BF16),
        compiler_params=_params("arbitrary", "arbitrary"),
        name="nsa",
    )(nq, kcmp, vcmp, nkv, nkv, win, aux, onehot)


def _dilated_kernel(q1_ref, k1_ref, v1_ref, q4_ref, k4_ref, v4_ref, q16_ref, k16_ref, v16_ref,
                    o_ref, osc_ref, lsc_ref, *, tile, seq):
    i = pl.program_id(1)
    tq = DIL_SPAN
    rows = DIL_HEADS * tq
    row_u = lax.broadcasted_iota(jnp.int32, (rows, 1), 0) & (tq - 1)
    refs = ((q1_ref, k1_ref, v1_ref), (q4_ref, k4_ref, v4_ref), (q16_ref, k16_ref, v16_ref))
    for pat, (dilation, (q_ref, k_ref, v_ref)) in enumerate(zip(DIL_DILATIONS, refs)):
        length = seq // dilation
        span = min(2 * tq, length)
        per_class = tile // (tq * dilation)
        n_sub = tile // tq
        base = i * (tile // dilation)
        rel = row_u - lax.broadcasted_iota(jnp.int32, (1, span), 1)

        def band(offset):
            return jnp.where((rel + offset >= 0) & (rel + offset <= DIL_SPAN), 0.0, NEG_INF).astype(F32)

        bias_shifted, bias_aligned = band(min(DIL_SPAN, length - span + DIL_SPAN)), band(0)

        def sub_tile(sub):
            c = sub // per_class
            w = sub % per_class
            u0 = w * tq
            ks0 = pl.multiple_of(jnp.maximum(base + u0 - DIL_SPAN, 0), tq)
            qst = _stack_heads(q_ref[c, pl.ds(u0, tq), :], DIL_HEADS)
            bias = jnp.where(base == 0, bias_aligned, bias_shifted) if u0 == 0 else bias_shifted
            s = _dot_nt(qst, k_ref[c, pl.ds(ks0, span), :]) + bias
            m = jnp.max(s, axis=-1, keepdims=True)
            acc = _pair_pv(jnp.exp2((s - m).astype(BF16)),
                           _pair_values(v_ref[c, pl.ds(ks0, span), :]))
            den = acc[:, LANES:]
            dst = pl.ds(u0 * dilation + c, tq, stride=dilation)
            for half, (o, lse) in enumerate(zip(_unstack_halves(acc[:, :LANES] / den, tq),
                                                _unstack_halves(m + jnp.log2(den), tq))):
                osc_ref[pat, half, dst, :] = o
                lsc_ref[pat, half, dst, :] = lse

        for sub in range(n_sub):
            sub_tile(sub)

    chunk = 2 * tq
    for r0 in range(0, tile, chunk):
        for half in range(2):
            ls = [lsc_ref[p, half, r0:r0 + chunk, :] for p in range(3)]
            top = jnp.maximum(jnp.maximum(ls[0], ls[1]), ls[2])
            es = [jnp.exp2(l - top) for l in ls]
            num = (es[0] * osc_ref[0, half, r0:r0 + chunk, :] + es[1] * osc_ref[1, half, r0:r0 + chunk, :]
                   + es[2] * osc_ref[2, half, r0:r0 + chunk, :])
            o_ref[r0:r0 + chunk, half * LANES:(half + 1) * LANES] = (
                num / (es[0] + es[1] + es[2])).astype(o_ref.dtype)


def _dilated(dil1, dil4, dil16, *, batch, seq, tile):
    width = DIL_HEADS * HEAD_DIM
    nt = seq // tile
    specs = []
    for arr in (dil1, dil4, dil16):
        dilation, length = arr.shape[1], arr.shape[2]
        specs.append(pl.BlockSpec((None, dilation, tile // dilation, width), lambda b, i: (b, 0, i, 0)))
        for part in (1, 2):
            specs.append(pl.BlockSpec((None, dilation, length, width),
                                      lambda b, i, part=part: (b, 0, 0, part)))
    return pl.pallas_call(
        functools.partial(_dilated_kernel, tile=tile, seq=seq),
        grid=(batch, nt),
        in_specs=specs,
        out_specs=pl.BlockSpec((tile, width), lambda b, i: (b * nt + i, 0)),
        out_shape=jax.ShapeDtypeStruct((batch * seq, width), BF16),
        scratch_shapes=[pltpu.VMEM((3, 2, tile, LANES), F32), pltpu.VMEM((3, 2, tile, LANES), F32)],
        compiler_params=_params("arbitrary", "arbitrary"),
        name="dilated",
    )(dil1, dil1, dil1, dil4, dil4, dil4, dil16, dil16, dil16)


SUBLANES = 8
ROW_BLOCK = SUBLANES * SUBLANES


def _swap_rows(x, stage_ref):
    rows, width = x.shape
    for c in range(width // LANES):
        stage_ref[c] = x[:, c * LANES:(c + 1) * LANES]
    cols = []
    for c in range(width // LANES):
        slabs = [stage_ref[c, pl.ds(b0 + k, SUBLANES, stride=SUBLANES), :]
                 for b0 in range(0, rows, ROW_BLOCK) for k in range(SUBLANES)]
        cols.append(jnp.concatenate(slabs, axis=0))
    return jnp.concatenate(cols, axis=1)


def _mix_ffn_kernel(x_ref, fox_ref, nsa_ref, dil_ref, wout_ref, gmix_ref,
                    gpre_ref, wup_ref, cw_ref, cb_ref, wdn_ref, gpost_ref, o_ref,
                    tail_ref, stage_ref, act_ref, *, tm, d_ff, chunk):
    @pl.when(pl.program_id(1) == 0)
    def _():
        tail_ref[...] = jnp.zeros_like(tail_ref)

    fw = FOX_HEADS * HEAD_DIM
    nw = NSA_HEADS * HEAD_DIM
    mix = (_dot(fox_ref[...], wout_ref[:fw, :]) + _dot(nsa_ref[...], wout_ref[fw:fw + nw, :])
           + _dot(dil_ref[...], wout_ref[fw + nw:, :]))
    x = x_ref[...] + _rms(mix, gmix_ref[...])
    h = _swap_rows(_rms(x, gpre_ref[...]), stage_ref).astype(BF16)
    row8 = lax.broadcasted_iota(jnp.int32, (SUBLANES, chunk), 0)

    def down_one(vreg_row, prev_last):
        return jnp.where(row8 == 0, prev_last, pltpu.roll(vreg_row, 1, 0))

    def conv(c0):
        u = _dot(h, wup_ref[:, c0:c0 + chunk])
        tail = tail_ref[:, c0:c0 + chunk]
        prev6, prev7 = tail[SUBLANES - 1:SUBLANES], tail[2 * SUBLANES - 1:]
        u1, u2 = [], []
        for b0 in range(0, tm, ROW_BLOCK):
            v = [u[b0 + k * SUBLANES:b0 + (k + 1) * SUBLANES] for k in range(SUBLANES)]
            s6, s7 = down_one(v[6], prev6), down_one(v[7], prev7)
            u1 += [s7] + v[:7]
            u2 += [s6, s7] + v[:6]
            prev6, prev7 = v[6][SUBLANES - 1:], v[7][SUBLANES - 1:]
        tail_ref[:, c0:c0 + chunk] = u[tm - 2 * SUBLANES:, :]
        return (cb_ref[:, c0:c0 + chunk] + jnp.concatenate(u2, axis=0) * cw_ref[0:1, c0:c0 + chunk]
                + jnp.concatenate(u1, axis=0) * cw_ref[1:2, c0:c0 + chunk] + u * cw_ref[2:3, c0:c0 + chunk])

    for c0 in range(0, d_ff, chunk):
        a = conv(c0)
        b = conv(d_ff + c0)
        act_ref[:, c0:c0 + chunk] = (a * jax.nn.sigmoid(a) * b).astype(BF16)
    y = _dot(act_ref[...], wdn_ref[...])
    o_ref[...] = x + _swap_rows(_rms(y, gpost_ref[...]), stage_ref)


def _mix_ffn(x, fox_o, nsa_o, dil_o, wout, gmix, gpre, wup_all, cw, cb, wdn_all, gpost,
             *, layer, batch, seq, tm, chunk):
    n, d = x.shape
    d_ff = wdn_all.shape[1]
    nt = seq // tm
    tok = lambda width: pl.BlockSpec((tm, width), lambda b, i: (b * nt + i, 0))
    const = lambda a: pl.BlockSpec(a.shape, lambda b, i: (0, 0), pipeline_mode=pl.Buffered(1))
    stacked = lambda a: pl.BlockSpec((None,) + a.shape[1:], lambda b, i: (layer, 0, 0),
                                     pipeline_mode=pl.Buffered(1))
    return pl.pallas_call(
        functools.partial(_mix_ffn_kernel, tm=tm, d_ff=d_ff, chunk=chunk),
        grid=(batch, nt),
        in_specs=[tok(d), tok(fox_o.shape[1]), tok(nsa_o.shape[1]), tok(dil_o.shape[1]),
                  const(wout), const(gmix), const(gpre), stacked(wup_all), const(cw), const(cb),
                  stacked(wdn_all), const(gpost)],
        out_specs=tok(d),
        out_shape=jax.ShapeDtypeStruct((n, d), F32),
        scratch_shapes=[pltpu.VMEM((2 * SUBLANES, 2 * d_ff), F32), pltpu.VMEM((d // LANES, tm, LANES), F32),
                        pltpu.VMEM((tm, d_ff), BF16)],
        compiler_params=_params("arbitrary", "arbitrary"),
        name="mix_ffn",
    )(x, fox_o, nsa_o, dil_o, wout, gmix, gpre, wup_all, cw, cb, wdn_all, gpost)


def _rope_tables(positions):
    inv_freq = ROPE_THETA ** (-2.0 * jnp.arange(ROPE_HALF, dtype=F32) / ROPE_DIM)
    ang = positions.astype(F32).reshape(-1, 1) * inv_freq
    cos, sin = jnp.cos(ang), jnp.sin(ang)
    within = np.arange(LANES) % HEAD_DIM
    freq = np.arange(ROPE_HALF)[:, None]
    place = lambda hit: jnp.asarray(hit, F32)
    spread = lambda t, p: jnp.dot(t, p, precision=lax.Precision.HIGHEST)
    rc = spread(cos, place((within % ROPE_HALF == freq) & (within < ROPE_DIM))) + place(within >= ROPE_DIM)
    rs1 = spread(sin, place(within - ROPE_HALF == freq))
    rs2 = -spread(sin, place(within == freq))
    return rc, rs1, rs2


def _nsa_head_perm():
    cols = []
    for r in range(NSA_REP):
        for g in range(NSA_GROUPS):
            h = g * NSA_REP + r
            cols.extend(range(h * HEAD_DIM, (h + 1) * HEAD_DIM))
    return np.asarray(cols)


def _regroup_w_in(w_all):
    fw, nw, kvw, dw = FOX_HEADS * HEAD_DIM, NSA_HEADS * HEAD_DIM, NSA_GROUPS * HEAD_DIM, DIL_HEADS * HEAD_DIM
    sizes = [fw, fw, fw, FOX_HEADS, nw] + [kvw] * 6 + [3 * NSA_HEADS, dw, dw, dw]
    offs = np.concatenate([[0], np.cumsum(sizes)])
    part = lambda idx: np.arange(offs[idx], offs[idx + 1])
    fq, fk, fv, ff, nq = (part(t) for t in range(5))
    kvs = [part(t) for t in range(5, 11)]
    ng, dq, dk, dv = (part(t) for t in range(11, 15))
    pad = np.full((2 * LANES - FOX_HEADS - 3 * NSA_HEADS,), -1)
    src = np.concatenate([fq, fk, fv, nq[_nsa_head_perm()]] + kvs + [dq, dk, dv, ff, ng, pad])
    select = (jnp.arange(w_all.shape[2])[:, None] == jnp.asarray(src)[None, :]).astype(BF16)
    return jnp.einsum("ldc,cn->ldn", w_all.astype(BF16), select, preferred_element_type=BF16)


def _regroup_w_out(w):
    fw, nw = FOX_HEADS * HEAD_DIM, NSA_HEADS * HEAD_DIM
    return jnp.concatenate([w[:fw], w[fw:fw + nw][_nsa_head_perm()], w[fw + nw:]], axis=0).astype(BF16)


def _place_w1(w1):
    hidden = w1.shape[1]
    w = w1.reshape(CMP_BLOCK // CMP_STRIDE, CMP_STRIDE, HEAD_DIM, hidden)
    z = jnp.zeros_like(w)
    per_group = [jnp.concatenate([w, z], axis=2), jnp.concatenate([z, w], axis=2)]
    return jnp.stack(per_group).reshape(NSA_GROUPS, CMP_BLOCK // CMP_STRIDE, CMP_STRIDE * LANES,
                                        hidden).astype(BF16)


def _place_pos(p):
    z = jnp.zeros_like(p)
    row = jnp.concatenate([p, z], axis=1).reshape(1, -1)
    return jnp.concatenate([row, jnp.zeros((7, row.shape[1]), row.dtype)], axis=0)


def _place_w2(w2):
    z = jnp.zeros_like(w2)
    return jnp.stack([jnp.concatenate([w2, z], axis=1), jnp.concatenate([z, w2], axis=1)]).astype(BF16)


class _Tiles(NamedTuple):
    inproj: int
    key: int
    fox_q: int
    nsa_q: int
    dilated: int
    ffn: int
    ffn_cols: int


def _tiles(seq):
    base = min(4 * LANES, seq)
    tiles = _Tiles(inproj=min(2 * base, seq), key=base, fox_q=base, nsa_q=min(2 * LANES, base),
                   dilated=DIL_DILATIONS[-1] * DIL_SPAN, ffn=base, ffn_cols=2 * LANES)
    assert seq % tiles.dilated == 0 and seq % tiles.inproj == 0, "sequence must be a multiple of 2048"
    assert seq // SEL_BLOCK <= LANES, "selection blocks must fit one lane tile"
    return tiles


def kernel(x, positions, attn_pre_norm, attn_post_norm, ffn_pre_norm, ffn_post_norm, w_in, b_forget, b_nsa_gate, cmp_pos_k, cmp_w1_k, cmp_w2_k, cmp_pos_v, cmp_w1_v, cmp_w2_v, w_out, w_up, conv_w, conv_b, w_down):
    batch, seq, d = x.shape
    depth = w_in.shape[0]
    n = batch * seq
    tiles = _tiles(seq)

    rc, rs1, rs2 = _rope_tables(positions)
    w_in_k = _regroup_w_in(w_in)
    w_up_k, w_down_k = w_up.astype(BF16), w_down.astype(BF16)
    xf = x.reshape(n, d)
    for l in range(depth):
        bias = jnp.concatenate([b_forget[l], b_nsa_gate[l],
                                jnp.zeros((LANES - FOX_HEADS - 3 * NSA_HEADS,), F32)]).reshape(1, LANES)
        fox, nq, nkv, dil, dil4, dil16, aux, ck_t, xk, xv = _inproj(
            xf, attn_pre_norm[l].reshape(1, d), w_in_k, rc, rs1, rs2, bias,
            layer=l, batch=batch, seq=seq, tm=tiles.inproj)
        fox_o = _fox(fox, ck_t, batch=batch, seq=seq, tq=tiles.fox_q, tk=tiles.key)

        kcmp, vcmp = _compress(
            xk, xv, _place_w1(cmp_w1_k[l]), _place_w2(cmp_w2_k[l]), _place_pos(cmp_pos_k[l]),
            _place_w1(cmp_w1_v[l]), _place_w2(cmp_w2_v[l]), _place_pos(cmp_pos_v[l]))
        win = _nsa_window(nq, nkv, aux, batch=batch, seq=seq, tq=tiles.nsa_q)
        nsa_o = _nsa(nq, nkv, kcmp, vcmp, win, aux, batch=batch, seq=seq, tq=tiles.nsa_q, tk=tiles.key)

        dil_o = _dilated(dil.reshape(batch, 1, seq, dil.shape[1]), dil4, dil16,
                         batch=batch, seq=seq, tile=tiles.dilated)

        xf = _mix_ffn(xf, fox_o, nsa_o, dil_o, _regroup_w_out(w_out[l]), attn_post_norm[l].reshape(1, d),
                      ffn_pre_norm[l].reshape(1, d), w_up_k, conv_w[l],
                      conv_b[l].reshape(1, -1), w_down_k, ffn_post_norm[l].reshape(1, d),
                      layer=l, batch=batch, seq=seq, tm=tiles.ffn, chunk=tiles.ffn_cols)
    return xf.reshape(batch, seq, d)
```
